```python
import jax
import jax.numpy as jnp
from jax import lax
import numpy as np

D_MODEL = 1024
BATCH = 1
SEQ = 16384
DEPTH = 2
DEC_BATCH = 8
DEC_SEQ = 8192
PAST_LEN = 128

CONV_W = D_MODEL // 4
ML_HEADS = 4
ML_HD = D_MODEL // 8
ML_W = ML_HEADS * ML_HD
NA_HEADS = 4
NA_HD = D_MODEL // 16
NA_W = NA_HEADS * NA_HD
D_MIX = CONV_W + ML_W + NA_W
OFF_ML = 2 * CONV_W
OFF_NA = OFF_ML + 4 * ML_W + 4 * ML_HEADS
IN_COLS = OFF_NA + 3 * NA_W
CONV_K = 31
ML_QK_CONV_K = 3
ML_CHUNK = 128
GRID_W = 64
NA_WIN_ROWS = 8
NA_WIN_COLS = 16
N_GROUPS = 8
EXPERTS_PER_GROUP = 8
N_EXPERTS = N_GROUPS * EXPERTS_PER_GROUP
TOP_K = 2
D_EXPERT = D_MODEL // 2
MOE_BLOCK = 256
EPS = 1e-6

kernel_name = 'hybrid_bidir_conv_mlstm_natten_hmoe'


def _rms(x, g):
    xf = x.astype(jnp.float32)
    y = xf * lax.rsqrt(jnp.mean(xf * xf, axis=-1, keepdims=True) + EPS)
    return (y * g.astype(jnp.float32)).astype(x.dtype)


def _dwconv(x, w, b):
    k = w.shape[0]
    y = lax.conv_general_dilated(x, w[:, None, :].astype(x.dtype), window_strides=(1,),
                                 padding=[(k // 2, k // 2)],
                                 dimension_numbers=('NWC', 'WIO', 'NWC'),
                                 feature_group_count=x.shape[-1])
    return y + b.astype(x.dtype)


def _mlstm_dir(q, k, v, ig, lf):
    B, H, T, d = q.shape
    L = ML_CHUNK
    NC = T // L
    q = q.reshape(B, H, NC, L, d)
    k = k.reshape(B, H, NC, L, d)
    v = v.reshape(B, H, NC, L, d)
    ig = ig.reshape(B, H, NC, L)
    lf = lf.reshape(B, H, NC, L)
    b = jnp.cumsum(lf, axis=-1)
    b_last = b[..., -1]
    w = b_last[..., None] - b + ig
    m_loc = jnp.max(w, axis=-1)
    a = jnp.exp(w - m_loc[..., None])
    S_C = jnp.einsum('bhcsk,bhcsv->bhckv', k * a[..., None], v)
    S_n = jnp.einsum('bhcs,bhcsk->bhck', a, k)

    def step(carry, inp):
        C, n, m = carry
        Sc, Sn, ml, bl = inp
        m_new = jnp.maximum(bl + m, ml)
        dec = jnp.exp(bl + m - m_new)
        inc = jnp.exp(ml - m_new)
        C_new = dec[..., None, None] * C + inc[..., None, None] * Sc
        n_new = dec[..., None] * n + inc[..., None] * Sn
        return (C_new, n_new, m_new), (C, n, m)

    init = (jnp.zeros((B, H, d, d), jnp.float32), jnp.zeros((B, H, d), jnp.float32),
            jnp.zeros((B, H), jnp.float32))
    xs = (jnp.moveaxis(S_C, 2, 0), jnp.moveaxis(S_n, 2, 0),
          jnp.moveaxis(m_loc, 2, 0), jnp.moveaxis(b_last, 2, 0))
    _, (C_prev, n_prev, m_prev) = lax.scan(step, init, xs)
    C_prev = jnp.moveaxis(C_prev, 0, 2)
    n_prev = jnp.moveaxis(n_prev, 0, 2)
    m_prev = jnp.moveaxis(m_prev, 0, 2)
    causal = jnp.tril(jnp.ones((L, L), dtype=bool))
    D = jnp.where(causal, b[..., :, None] - b[..., None, :] + ig[..., None, :], -jnp.inf)
    inter = b + m_prev[..., None]
    m_j = jnp.maximum(inter, jnp.max(D, axis=-1))
    P = jnp.exp(D - m_j[..., None])
    e_inter = jnp.exp(inter - m_j)
    qk = jnp.einsum('bhcjd,bhcsd->bhcjs', q, k) * P
    num = (jnp.einsum('bhcjs,bhcsd->bhcjd', qk, v)
           + e_inter[..., None] * jnp.einsum('bhcjk,bhckv->bhcjv', q, C_prev))
    den = jnp.sum(qk, axis=-1) + e_inter * jnp.einsum('bhcjk,bhck->bhcj', q, n_prev)
    h = num / jnp.maximum(jnp.abs(den), jnp.exp(-m_j))[..., None]
    return h.reshape(B, H, T, d)


def _neigh_attn(q, k, v, rpb):
    B, T, H, hd = q.shape
    R = T // GRID_W
    KR = min(NA_WIN_ROWS, R)
    q = q.reshape(B, R, GRID_W, H, hd)
    k = k.reshape(B, R, GRID_W, H, hd)
    v = v.reshape(B, R, GRID_W, H, hd)
    r = jnp.arange(R)
    row_idx = jnp.clip(r - KR // 2, 0, R - KR)[:, None] + jnp.arange(KR)[None, :]
    kg = k[:, row_idx]
    vg = v[:, row_idx]
    s = jnp.einsum('brqhd,brikhd->brhqik', q, kg, preferred_element_type=jnp.float32) * (hd ** -0.5)
    c = jnp.arange(GRID_W)
    cs = jnp.clip(c - NA_WIN_COLS // 2, 0, GRID_W - NA_WIN_COLS)
    col_ok = (c[None, :] >= cs[:, None]) & (c[None, :] < cs[:, None] + NA_WIN_COLS)
    ro = row_idx - r[:, None] + (NA_WIN_ROWS - 1)
    co = jnp.clip(c[None, :] - c[:, None] + (NA_WIN_COLS - 1), 0, 2 * NA_WIN_COLS - 2)
    bias = rpb.astype(jnp.float32)[:, ro[:, None, :, None], co[None, :, None, :]]
    s = s + jnp.transpose(bias, (1, 0, 2, 3, 4))[None]
    s = jnp.where(col_ok[:, None, :], s, -jnp.inf)
    p = jax.nn.softmax(s.reshape(B, R, H, GRID_W, KR * GRID_W), axis=-1)
    p = p.reshape(s.shape).astype(v.dtype)
    o = jnp.einsum('brhqik,brikhd->brqhd', p, vg)
    return o.reshape(B, T, H * hd)


def _mixer(h, w_in, conv_dw_w, conv_dw_b, conv_norm_g, ml_qk_conv_w, ml_qk_conv_b,
           ml_gate_b, ml_norm_g, na_q_norm_g, na_k_norm_g, na_rpb, w_out):
    B, T, _ = h.shape
    f32 = jnp.float32
    z = h @ w_in
    u = z[..., 0:CONV_W] * jax.nn.sigmoid(z[..., CONV_W:2 * CONV_W])
    u = _dwconv(u, conv_dw_w, conv_dw_b)
    y_conv = jax.nn.silu(_rms(u, conv_norm_g))
    zm = z[..., OFF_ML:OFF_NA]
    qk = jax.nn.silu(_dwconv(zm[..., :2 * ML_W], ml_qk_conv_w, ml_qk_conv_b))
    o_gate = zm[..., 3 * ML_W:4 * ML_W]
    gates = zm[..., 4 * ML_W:].astype(f32).reshape(B, T, 4, ML_HEADS) + ml_gate_b.astype(f32)
    gates = jnp.transpose(gates, (2, 0, 3, 1))

    def heads(t):
        return jnp.transpose(t.astype(f32).reshape(B, T, ML_HEADS, ML_HD), (0, 2, 1, 3))

    qh = heads(qk[..., :ML_W])
    kh = heads(qk[..., ML_W:]) * (ML_HD ** -0.5)
    vh = heads(zm[..., 2 * ML_W:3 * ML_W])
    h_f = _mlstm_dir(qh, kh, vh, gates[0], jax.nn.log_sigmoid(gates[1]))
    h_b = jnp.flip(_mlstm_dir(jnp.flip(qh, 2), jnp.flip(kh, 2), jnp.flip(vh, 2),
                              jnp.flip(gates[2], 2), jnp.flip(jax.nn.log_sigmoid(gates[3]), 2)), 2)
    hm = jnp.transpose(h_f + h_b, (0, 2, 1, 3))
    hm = _rms(hm, ml_norm_g.reshape(ML_HEADS, ML_HD)).reshape(B, T, ML_W)
    y_ml = hm.astype(h.dtype) * jax.nn.sigmoid(o_gate)
    zn = z[..., OFF_NA:]
    qn = _rms(zn[..., :NA_W].reshape(B, T, NA_HEADS, NA_HD), na_q_norm_g)
    kn = _rms(zn[..., NA_W:2 * NA_W].reshape(B, T, NA_HEADS, NA_HD), na_k_norm_g)
    vn = zn[..., 2 * NA_W:].reshape(B, T, NA_HEADS, NA_HD)
    y_na = _neigh_attn(qn, kn, vn, na_rpb)
    y = jnp.concatenate([y_conv, y_ml, y_na], axis=-1)
    return y @ w_out


def _moe(x, w_router_group, b_router_group, w_router_expert, b_router_expert,
         w_exp_gate, w_exp_up, w_exp_down):
    B, T, D = x.shape
    N = B * T
    M = N * TOP_K
    f32 = jnp.float32
    xt = x.reshape(N, D)
    gp = jax.nn.softmax((xt @ w_router_group).astype(f32) + b_router_group.astype(f32), axis=-1)
    pg, gsel = lax.top_k(gp, 1)
    el = ((xt @ w_router_expert).astype(f32) + b_router_expert.astype(f32)).reshape(N, N_GROUPS, EXPERTS_PER_GROUP)
    el = jnp.take_along_axis(el, gsel[:, :, None], axis=1)[:, 0]
    pe, esel = lax.top_k(jax.nn.softmax(el, axis=-1), TOP_K)
    gate = pg * pe / jnp.sum(pe, axis=-1, keepdims=True)
    eid = (gsel * EXPERTS_PER_GROUP + esel).reshape(-1)
    order = jnp.argsort(eid)
    es = eid[order]
    tok = order // TOP_K
    gs = gate.reshape(-1)[order]
    sizes = jnp.bincount(eid, length=N_EXPERTS)
    psz = (sizes + MOE_BLOCK - 1) // MOE_BLOCK * MOE_BLOCK
    start = jnp.cumsum(sizes) - sizes
    pend = jnp.cumsum(psz)
    pstart = pend - psz
    dest = pstart[es] + jnp.arange(M) - start[es]
    n_blocks = -(-M // MOE_BLOCK) + N_EXPERTS
    buf = jnp.zeros((n_blocks * MOE_BLOCK, D), x.dtype).at[dest].set(xt[tok])
    blk_e = jnp.minimum(jnp.searchsorted(pend, jnp.arange(n_blocks) * MOE_BLOCK, side='right'), N_EXPERTS - 1)

    def expert_block(args):
        xb, e = args
        hb = jax.nn.silu(xb @ w_exp_gate[e]) * (xb @ w_exp_up[e])
        return hb @ w_exp_down[e]

    yb = lax.map(expert_block, (buf.reshape(n_blocks, MOE_BLOCK, D), blk_e))
    y = yb.reshape(n_blocks * MOE_BLOCK, D)[dest] * gs[:, None].astype(yb.dtype)
    out = jnp.zeros((N, D), x.dtype).at[tok].add(y.astype(x.dtype))
    return out.reshape(B, T, D)


def _layer(x, norm_mix_g, w_in, conv_dw_w, conv_dw_b, conv_norm_g, ml_qk_conv_w, ml_qk_conv_b,
           ml_gate_b, ml_norm_g, na_q_norm_g, na_k_norm_g, na_rpb, w_out, norm_ffn_g,
           w_router_group, b_router_group, w_router_expert, b_router_expert,
           w_exp_gate, w_exp_up, w_exp_down):
    x = x + _mixer(_rms(x, norm_mix_g), w_in, conv_dw_w, conv_dw_b, conv_norm_g, ml_qk_conv_w,
                   ml_qk_conv_b, ml_gate_b, ml_norm_g, na_q_norm_g, na_k_norm_g, na_rpb, w_out)
    x = x + _moe(_rms(x, norm_ffn_g), w_router_group, b_router_group, w_router_expert,
                 b_router_expert, w_exp_gate, w_exp_up, w_exp_down)
    return x


def setup_inputs(seed: int = 0) -> dict:
    key = jax.random.key(seed)
    ks = jax.random.split(key, 24)
    f32 = jnp.float32

    def nrm(k, shape, scale):
        return jax.random.normal(k, shape, f32) * scale

    L = DEPTH
    f_bias = jnp.linspace(3.0, 6.0, ML_HEADS, dtype=f32)
    gk = jax.random.split(ks[9], 4)
    ml_gate_b = jnp.stack([nrm(gk[0], (L, ML_HEADS), 0.1),
                           f_bias + nrm(gk[1], (L, ML_HEADS), 0.1),
                           nrm(gk[2], (L, ML_HEADS), 0.1),
                           f_bias + nrm(gk[3], (L, ML_HEADS), 0.1)], axis=1)
    return {
        'x_prompt': nrm(ks[0], (BATCH, SEQ, D_MODEL), 1.0),
        'x_sample': nrm(ks[1], (DEC_BATCH, DEC_SEQ, D_MODEL), 1.0),
        'norm_mix_g': 1.0 + nrm(ks[2], (L, D_MODEL), 0.05),
        'w_in': nrm(ks[3], (L, D_MODEL, IN_COLS), D_MODEL ** -0.5),
        'conv_dw_w': nrm(ks[4], (L, CONV_K, CONV_W), CONV_K ** -0.5),
        'conv_dw_b': nrm(ks[5], (L, CONV_W), 0.02),
        'conv_norm_g': 1.0 + nrm(ks[6], (L, CONV_W), 0.05),
        'ml_qk_conv_w': nrm(ks[7], (L, ML_QK_CONV_K, 2 * ML_W), ML_QK_CONV_K ** -0.5),
        'ml_qk_conv_b': nrm(ks[8], (L, 2 * ML_W), 0.02),
        'ml_gate_b': ml_gate_b,
        'ml_norm_g': 1.0 + nrm(ks[10], (L, ML_W), 0.05),
        'na_q_norm_g': 1.0 + nrm(ks[11], (L, NA_HD), 0.05),
        'na_k_norm_g': 1.0 + nrm(ks[12], (L, NA_HD), 0.05),
        'na_rpb': nrm(ks[13], (L, NA_HEADS, 2 * NA_WIN_ROWS - 1, 2 * NA_WIN_COLS - 1), 0.1),
        'w_out': nrm(ks[14], (L, D_MIX, D_MODEL), D_MIX ** -0.5),
        'norm_ffn_g': 1.0 + nrm(ks[15], (L, D_MODEL), 0.05),
        'w_router_group': nrm(ks[16], (L, D_MODEL, N_GROUPS), D_MODEL ** -0.5),
        'b_router_group': nrm(ks[17], (L, N_GROUPS), 0.01),
        'w_router_expert': nrm(ks[18], (L, D_MODEL, N_EXPERTS), D_MODEL ** -0.5),
        'b_router_expert': nrm(ks[19], (L, N_EXPERTS), 0.01),
        'w_exp_gate': nrm(ks[20], (L, N_EXPERTS, D_MODEL, D_EXPERT), D_MODEL ** -0.5),
        'w_exp_up': nrm(ks[21], (L, N_EXPERTS, D_MODEL, D_EXPERT), D_MODEL ** -0.5),
        'w_exp_down': nrm(ks[22], (L, N_EXPERTS, D_EXPERT, D_MODEL), D_EXPERT ** -0.5),
    }


def reference(x_prompt, x_sample, norm_mix_g, w_in, conv_dw_w, conv_dw_b, conv_norm_g,
              ml_qk_conv_w, ml_qk_conv_b, ml_gate_b, ml_norm_g, na_q_norm_g, na_k_norm_g,
              na_rpb, w_out, norm_ffn_g, w_router_group, b_router_group, w_router_expert,
              b_router_expert, w_exp_gate, w_exp_up, w_exp_down):
    params = (norm_mix_g, w_in, conv_dw_w, conv_dw_b, conv_norm_g, ml_qk_conv_w, ml_qk_conv_b,
              ml_gate_b, ml_norm_g, na_q_norm_g, na_k_norm_g, na_rpb, w_out, norm_ffn_g,
              w_router_group, b_router_group, w_router_expert, b_router_expert,
              w_exp_gate, w_exp_up, w_exp_down)
    y_prompt = x_prompt
    y_sample = x_sample
    for l in range(DEPTH):
        layer_p = [p[l] for p in params]
        y_prompt = _layer(y_prompt, *layer_p)
        y_sample = _layer(y_sample, *layer_p)
    return (y_prompt, y_sample)
```

```python
import functools

import jax
import jax.numpy as jnp
from jax import lax
from jax.experimental import pallas as pl
from jax.experimental.pallas import tpu as pltpu

F32 = jnp.float32
BF16 = jnp.bfloat16

ML_HEADS = 4
NA_HEADS = 4
CONV_K = 31
ML_CHUNK = 128
GRID_W = 64
NA_WIN_ROWS = 8
NA_WIN_COLS = 16
N_GROUPS = 8
EXPERTS_PER_GROUP = 8
N_EXPERTS = N_GROUPS * EXPERTS_PER_GROUP
TOP_K = 2
EPS = 1e-6

LANES = 128
VMEM_LIMIT = 48 * 1024 * 1024

TOKEN_TILE = 512
EXPERT_BLOCK = 256


def _cparams(*sem):
    return pltpu.CompilerParams(dimension_semantics=sem, vmem_limit_bytes=VMEM_LIMIT)


def _full(shape):
    return pl.BlockSpec(shape, lambda *_: (0,) * len(shape))


def _rows(tm, width):
    return pl.BlockSpec((tm, width), lambda i: (i, 0))


def _group_mean(zz, bd):
    hi = zz.astype(BF16)
    lo = (zz - hi.astype(F32)).astype(BF16)
    return (jnp.dot(hi, bd, preferred_element_type=F32) + jnp.dot(lo, bd, preferred_element_type=F32))


def _in_proj_kernel(x_ref, g_ref, w_ref, gb_ref, nq_ref, nk_ref, bd_ref,
                    u_ref, qk_ref, v_ref, o_ref, naq_ref, nak_ref, nav_ref, gt_ref, *, dims):
    cw, mw, nw = dims
    x = x_ref[...]
    ms = jnp.mean(x * x, axis=-1, keepdims=True)
    xn = (x * lax.rsqrt(ms + EPS) * g_ref[...]).astype(BF16)

    def proj(lo, width):
        return jnp.dot(xn, w_ref[:, lo:lo + width], preferred_element_type=F32)

    zc = proj(0, 2 * cw)
    u_ref[...] = zc[:, :cw] * jax.nn.sigmoid(zc[:, cw:])
    off = 2 * cw
    qk_ref[...] = proj(off, 2 * mw)
    v_ref[...] = proj(off + 2 * mw, mw).astype(BF16)
    o_ref[...] = proj(off + 3 * mw, mw)
    off += 4 * mw
    bd = bd_ref[...]
    zq = proj(off, nw)
    zk = proj(off + nw, nw)
    hd = nw // NA_HEADS
    naq_ref[...] = (zq * lax.rsqrt(_group_mean(zq * zq, bd) + EPS) * nq_ref[...] * (hd ** -0.5)).astype(BF16)
    nak_ref[...] = (zk * lax.rsqrt(_group_mean(zk * zk, bd) + EPS) * nk_ref[...]).astype(BF16)
    nav_ref[...] = proj(off + 2 * nw, nw).astype(BF16)
    off += 3 * nw
    gt_ref[...] = proj(off, LANES) + gb_ref[...]


def _in_proj(x, g, w, gate_b, nq, nk, bd, dims):
    n, d = x.shape
    cw, mw, nw = dims
    tm = min(TOKEN_TILE, n)
    outs = [(cw, F32), (2 * mw, F32), (mw, BF16), (mw, F32), (nw, BF16), (nw, BF16), (nw, BF16), (LANES, F32)]
    return pl.pallas_call(
        functools.partial(_in_proj_kernel, dims=dims),
        grid=(n // tm,),
        in_specs=[_rows(tm, d), _full(g.shape), _full(w.shape), _full(gate_b.shape),
                  _full(nq.shape), _full(nk.shape), _full(bd.shape)],
        out_specs=[_rows(tm, wd) for wd, _ in outs],
        out_shape=[jax.ShapeDtypeStruct((n, wd), dt) for wd, dt in outs],
        compiler_params=_cparams("parallel"),
        name="in_proj",
    )(x, g, w, gate_b, nq, nk, bd)


def _out_proj_kernel(yc_ref, hf_ref, hb_ref, o_ref, yna_ref, x_ref, mg_ref, w_ref, fg_ref, wr_ref, br_ref,
                     x1_ref, xn_ref, lg_ref, *, dims):
    cw, mw, nw = dims
    hd = mw // ML_HEADS
    hm = hf_ref[...] + hb_ref[...]
    parts = []
    for h in range(ML_HEADS):
        seg = hm[:, h * hd:(h + 1) * hd]
        parts.append(seg * lax.rsqrt(jnp.mean(seg * seg, axis=-1, keepdims=True) + EPS))
    yml = jnp.concatenate(parts, axis=-1) * mg_ref[...] * jax.nn.sigmoid(o_ref[...])
    y = jnp.dot(yc_ref[...].astype(BF16), w_ref[0:cw, :], preferred_element_type=F32)
    y += jnp.dot(yml.astype(BF16), w_ref[cw:cw + mw, :], preferred_element_type=F32)
    y += jnp.dot(yna_ref[...].astype(BF16), w_ref[cw + mw:cw + mw + nw, :], preferred_element_type=F32)
    x1 = x_ref[...] + y
    x1_ref[...] = x1
    xn = x1 * lax.rsqrt(jnp.mean(x1 * x1, axis=-1, keepdims=True) + EPS) * fg_ref[...]
    xn_ref[...] = xn
    lg_ref[...] = jnp.dot(xn, wr_ref[...], preferred_element_type=F32, precision=lax.Precision.HIGHEST) + br_ref[...]


def _out_proj(yc, hf, hb, o, yna, x, mg, w, fg, wr, br, dims):
    n, d = x.shape
    cw, mw, nw = dims
    tm = min(TOKEN_TILE, n)
    return pl.pallas_call(
        functools.partial(_out_proj_kernel, dims=dims),
        grid=(n // tm,),
        in_specs=[_rows(tm, cw), _rows(tm, mw), _rows(tm, mw), _rows(tm, mw), _rows(tm, nw), _rows(tm, d),
                  _full(mg.shape), _full(w.shape), _full(fg.shape), _full(wr.shape), _full(br.shape)],
        out_specs=[_rows(tm, d), _rows(tm, d), _rows(tm, LANES)],
        out_shape=[jax.ShapeDtypeStruct((n, d), F32), jax.ShapeDtypeStruct((n, d), F32),
                   jax.ShapeDtypeStruct((n, LANES), F32)],
        compiler_params=_cparams("parallel"),
        name="out_proj",
    )(yc, hf, hb, o, yna, x, mg, w, fg, wr, br)


def _expert_kernel(be_ref, nb_ref, x_ref, wg_ref, wu_ref, wd_ref, y_ref):
    i = pl.program_id(0)

    @pl.when(i < nb_ref[0])
    def _():
        xb = x_ref[...].astype(BF16)
        g = jnp.dot(xb, wg_ref[0], preferred_element_type=F32)
        u = jnp.dot(xb, wu_ref[0], preferred_element_type=F32)
        h = (g * jax.nn.sigmoid(g) * u).astype(BF16)
        y_ref[...] = jnp.dot(h, wd_ref[0], preferred_element_type=F32)

    @pl.when(i >= nb_ref[0])
    def _():
        y_ref[...] = jnp.zeros_like(y_ref)


def _experts(buf, blk_e, n_used, wg, wu, wd):
    p, d = buf.shape
    de = wg.shape[-1]
    bm = EXPERT_BLOCK
    grid_spec = pltpu.PrefetchScalarGridSpec(
        num_scalar_prefetch=2,
        grid=(p // bm,),
        in_specs=[pl.BlockSpec((bm, d), lambda i, be, nb: (i, 0)),
                  pl.BlockSpec((1, d, de), lambda i, be, nb: (be[i], 0, 0)),
                  pl.BlockSpec((1, d, de), lambda i, be, nb: (be[i], 0, 0)),
                  pl.BlockSpec((1, de, d), lambda i, be, nb: (be[i], 0, 0))],
        out_specs=pl.BlockSpec((bm, d), lambda i, be, nb: (i, 0)),
    )
    return pl.pallas_call(
        _expert_kernel,
        grid_spec=grid_spec,
        out_shape=jax.ShapeDtypeStruct((p, d), F32),
        compiler_params=_cparams("arbitrary"),
        name="experts",
    )(blk_e, n_used, buf, wg, wu, wd)


def _rms(x, g):
    y = x * lax.rsqrt(jnp.mean(x * x, axis=-1, keepdims=True) + EPS)
    return y * g


def _dwconv(x, w, b):
    k = w.shape[0]
    y = lax.conv_general_dilated(x, w[:, None, :], window_strides=(1,), padding=[(k // 2, k // 2)],
                                 dimension_numbers=('NWC', 'WIO', 'NWC'), feature_group_count=x.shape[-1])
    return y + b


def _mlstm_dir(q, k, v, ig, lf):
    B, H, T, d = q.shape
    L = ML_CHUNK
    NC = T // L
    q = q.reshape(B, H, NC, L, d)
    k = k.reshape(B, H, NC, L, d)
    v = v.reshape(B, H, NC, L, d)
    ig = ig.reshape(B, H, NC, L)
    lf = lf.reshape(B, H, NC, L)
    b = jnp.cumsum(lf, axis=-1)
    b_last = b[..., -1]
    w = b_last[..., None] - b + ig
    m_loc = jnp.max(w, axis=-1)
    a = jnp.exp(w - m_loc[..., None])
    S_C = jnp.einsum('bhcsk,bhcsv->bhckv', k * a[..., None], v)
    S_n = jnp.einsum('bhcs,bhcsk->bhck', a, k)

    def step(carry, inp):
        C, n, m = carry
        Sc, Sn, ml, bl = inp
        m_new = jnp.maximum(bl + m, ml)
        dec = jnp.exp(bl + m - m_new)
        inc = jnp.exp(ml - m_new)
        C_new = dec[..., None, None] * C + inc[..., None, None] * Sc
        n_new = dec[..., None] * n + inc[..., None] * Sn
        return (C_new, n_new, m_new), (C, n, m)

    init = (jnp.zeros((B, H, d, d), F32), jnp.zeros((B, H, d), F32), jnp.zeros((B, H), F32))
    xs = (jnp.moveaxis(S_C, 2, 0), jnp.moveaxis(S_n, 2, 0), jnp.moveaxis(m_loc, 2, 0), jnp.moveaxis(b_last, 2, 0))
    _, (C_prev, n_prev, m_prev) = lax.scan(step, init, xs)
    C_prev = jnp.moveaxis(C_prev, 0, 2)
    n_prev = jnp.moveaxis(n_prev, 0, 2)
    m_prev = jnp.moveaxis(m_prev, 0, 2)
    causal = jnp.tril(jnp.ones((L, L), dtype=bool))
    D = jnp.where(causal, b[..., :, None] - b[..., None, :] + ig[..., None, :], -jnp.inf)
    inter = b + m_prev[..., None]
    m_j = jnp.maximum(inter, jnp.max(D, axis=-1))
    P = jnp.exp(D - m_j[..., None])
    e_inter = jnp.exp(inter - m_j)
    qk = jnp.einsum('bhcjd,bhcsd->bhcjs', q, k) * P
    num = (jnp.einsum('bhcjs,bhcsd->bhcjd', qk, v)
           + e_inter[..., None] * jnp.einsum('bhcjk,bhckv->bhcjv', q, C_prev))
    den = jnp.sum(qk, axis=-1) + e_inter * jnp.einsum('bhcjk,bhck->bhcj', q, n_prev)
    h = num / jnp.maximum(jnp.abs(den), jnp.exp(-m_j))[..., None]
    return h.reshape(B, H, T, d)


def _neigh_attn(q, k, v, rpb):
    B, T, H, hd = q.shape
    R = T // GRID_W
    KR = min(NA_WIN_ROWS, R)
    q = q.reshape(B, R, GRID_W, H, hd)
    k = k.reshape(B, R, GRID_W, H, hd)
    v = v.reshape(B, R, GRID_W, H, hd)
    r = jnp.arange(R)
    row_idx = jnp.clip(r - KR // 2, 0, R - KR)[:, None] + jnp.arange(KR)[None, :]
    kg = k[:, row_idx]
    vg = v[:, row_idx]
    s = jnp.einsum('brqhd,brikhd->brhqik', q, kg, preferred_element_type=F32)
    c = jnp.arange(GRID_W)
    cs = jnp.clip(c - NA_WIN_COLS // 2, 0, GRID_W - NA_WIN_COLS)
    col_ok = (c[None, :] >= cs[:, None]) & (c[None, :] < cs[:, None] + NA_WIN_COLS)
    ro = row_idx - r[:, None] + (NA_WIN_ROWS - 1)
    co = jnp.clip(c[None, :] - c[:, None] + (NA_WIN_COLS - 1), 0, 2 * NA_WIN_COLS - 2)
    bias = rpb.astype(F32)[:, ro[:, None, :, None], co[None, :, None, :]]
    s = s + jnp.transpose(bias, (1, 0, 2, 3, 4))[None]
    s = jnp.where(col_ok[:, None, :], s, -jnp.inf)
    p = jax.nn.softmax(s.reshape(B, R, H, GRID_W, KR * GRID_W), axis=-1)
    p = p.reshape(s.shape).astype(v.dtype)
    o = jnp.einsum('brhqik,brikhd->brqhd', p, vg)
    return o.reshape(B, T, H * hd)


def _mixers_xla(B, T, u, qkpre, v, o, naq, nak, nav, gates, p):
    cw = u.shape[-1]
    mw = v.shape[-1]
    hd = mw // ML_HEADS
    u3 = _dwconv(u.reshape(B, T, cw), p['conv_dw_w'], p['conv_dw_b'])
    y_conv = jax.nn.silu(_rms(u3, p['conv_norm_g'])).reshape(B * T, cw)
    qk = jax.nn.silu(_dwconv(qkpre.reshape(B, T, 2 * mw), p['ml_qk_conv_w'], p['ml_qk_conv_b']))
    g = jnp.transpose(gates[:, :16].reshape(B, T, 4, ML_HEADS), (2, 0, 3, 1))

    def heads(t):
        return jnp.transpose(t.astype(F32).reshape(B, T, ML_HEADS, hd), (0, 2, 1, 3))

    qh = heads(qk[..., :mw])
    kh = heads(qk[..., mw:]) * (hd ** -0.5)
    vh = heads(v.reshape(B, T, mw))
    h_f = _mlstm_dir(qh, kh, vh, g[0], jax.nn.log_sigmoid(g[1]))
    h_b = jnp.flip(_mlstm_dir(jnp.flip(qh, 2), jnp.flip(kh, 2), jnp.flip(vh, 2),
                              jnp.flip(g[2], 2), jnp.flip(jax.nn.log_sigmoid(g[3]), 2)), 2)
    hf = jnp.transpose(h_f, (0, 2, 1, 3)).reshape(B * T, mw)
    hb = jnp.transpose(h_b, (0, 2, 1, 3)).reshape(B * T, mw)
    nhd = naq.shape[-1] // NA_HEADS
    y_na = _neigh_attn(naq.astype(F32).reshape(B, T, NA_HEADS, nhd), nak.astype(F32).reshape(B, T, NA_HEADS, nhd),
                       nav.astype(F32).reshape(B, T, NA_HEADS, nhd), p['na_rpb'])
    return y_conv, hf, hb, y_na.reshape(B * T, -1)


def _route_xla(logits):
    n = logits.shape[0]
    gp = jax.nn.softmax(logits[:, :N_GROUPS], axis=-1)
    pg, gsel = lax.top_k(gp, 1)
    el = logits[:, N_GROUPS:N_GROUPS + N_EXPERTS].reshape(n, N_GROUPS, EXPERTS_PER_GROUP)
    el = jnp.take_along_axis(el, gsel[:, :, None], axis=1)[:, 0]
    pe, esel = lax.top_k(jax.nn.softmax(el, axis=-1), TOP_K)
    gate = pg * pe / jnp.sum(pe, axis=-1, keepdims=True)
    eid = gsel * EXPERTS_PER_GROUP + esel
    oh = jax.nn.one_hot(eid, N_EXPERTS, dtype=jnp.int32).sum(axis=1)
    prefix = jnp.cumsum(oh, axis=0) - oh
    rank = jnp.take_along_axis(prefix, eid, axis=1)
    counts = oh.sum(axis=0)
    return eid, gate, rank, counts


def _moe(x1, xn, logits, wg, wu, wd):
    n, d = x1.shape
    bm = EXPERT_BLOCK
    eid, gate, rank, counts = _route_xla(logits)
    psz = (counts + bm - 1) // bm * bm
    pend = jnp.cumsum(psz)
    pstart = pend - psz
    dest = pstart[eid] + rank
    n_blocks = -(-(n * TOP_K) // bm) + N_EXPERTS
    blk_e = jnp.minimum(jnp.searchsorted(pend, jnp.arange(n_blocks) * bm, side='right'), N_EXPERTS - 1).astype(jnp.int32)
    n_used = (pend[-1] // bm).astype(jnp.int32).reshape(1)
    buf = jnp.zeros((n_blocks * bm, d), F32).at[dest.reshape(-1)].set(jnp.repeat(xn, TOP_K, axis=0))
    y = _experts(buf, blk_e, n_used, wg, wu, wd)
    return x1 + (y[dest[:, 0]] * gate[:, 0:1] + y[dest[:, 1]] * gate[:, 1:2])


def _prep_layer(l, P):
    d = P['w_in'].shape[1]
    cw = P['conv_dw_w'].shape[-1]
    mw = P['ml_norm_g'].shape[-1]
    nw = NA_HEADS * P['na_q_norm_g'].shape[-1]
    off_g = 2 * cw + 4 * mw
    ng = 4 * ML_HEADS
    w = P['w_in'][l]
    w_perm = jnp.concatenate([w[:, :off_g], w[:, off_g + ng:], w[:, off_g:off_g + ng],
                              jnp.zeros((d, LANES - ng), F32)], axis=1).astype(BF16)
    nhd = nw // NA_HEADS
    head_id = jnp.arange(nw) // nhd
    bd = jnp.where(head_id[:, None] == head_id[None, :], 1.0 / nhd, 0.0).astype(BF16)
    wr = jnp.concatenate([P['w_router_group'][l], P['w_router_expert'][l],
                          jnp.zeros((d, LANES - N_GROUPS - N_EXPERTS), F32)], axis=1)
    br = jnp.concatenate([P['b_router_group'][l], P['b_router_expert'][l],
                          jnp.zeros((LANES - N_GROUPS - N_EXPERTS,), F32)])[None, :]
    return dict(
        dims=(cw, mw, nw),
        norm_mix_g=P['norm_mix_g'][l][None, :], w_in=w_perm,
        gate_b=jnp.pad(P['ml_gate_b'][l].reshape(-1), (0, LANES - ng))[None, :],
        nq=jnp.tile(P['na_q_norm_g'][l], NA_HEADS)[None, :], nk=jnp.tile(P['na_k_norm_g'][l], NA_HEADS)[None, :],
        bd=bd,
        conv_dw_w=P['conv_dw_w'][l], conv_dw_b=P['conv_dw_b'][l], conv_norm_g=P['conv_norm_g'][l],
        ml_qk_conv_w=P['ml_qk_conv_w'][l], ml_qk_conv_b=P['ml_qk_conv_b'][l],
        ml_norm_g=P['ml_norm_g'][l][None, :], na_rpb=P['na_rpb'][l],
        w_out=P['w_out'][l].astype(BF16), norm_ffn_g=P['norm_ffn_g'][l][None, :], wr=wr, br=br,
        wg=P['w_exp_gate'][l].astype(BF16), wu=P['w_exp_up'][l].astype(BF16), wd=P['w_exp_down'][l].astype(BF16),
    )


def _layer(x, B, T, p):
    dims = p['dims']
    u, qkpre, v, o, naq, nak, nav, gates = _in_proj(x, p['norm_mix_g'], p['w_in'], p['gate_b'], p['nq'], p['nk'],
                                                    p['bd'], dims)
    yc, hf, hb, yna = _mixers_xla(B, T, u, qkpre, v, o, naq, nak, nav, gates, p)
    x1, xn, logits = _out_proj(yc, hf, hb, o, yna, x, p['ml_norm_g'], p['w_out'], p['norm_ffn_g'], p['wr'], p['br'],
                               dims)
    return _moe(x1, xn, logits, p['wg'], p['wu'], p['wd'])


def kernel(x_prompt, x_sample, norm_mix_g, w_in, conv_dw_w, conv_dw_b, conv_norm_g, ml_qk_conv_w, ml_qk_conv_b,
           ml_gate_b, ml_norm_g, na_q_norm_g, na_k_norm_g, na_rpb, w_out, norm_ffn_g, w_router_group,
           b_router_group, w_router_expert, b_router_expert, w_exp_gate, w_exp_up, w_exp_down):
    P = dict(norm_mix_g=norm_mix_g, w_in=w_in, conv_dw_w=conv_dw_w, conv_dw_b=conv_dw_b, conv_norm_g=conv_norm_g,
             ml_qk_conv_w=ml_qk_conv_w, ml_qk_conv_b=ml_qk_conv_b, ml_gate_b=ml_gate_b, ml_norm_g=ml_norm_g,
             na_q_norm_g=na_q_norm_g, na_k_norm_g=na_k_norm_g, na_rpb=na_rpb, w_out=w_out, norm_ffn_g=norm_ffn_g,
             w_router_group=w_router_group, b_router_group=b_router_group, w_router_expert=w_router_expert,
             b_router_expert=b_router_expert, w_exp_gate=w_exp_gate, w_exp_up=w_exp_up, w_exp_down=w_exp_down)
    depth = w_in.shape[0]
    d = x_prompt.shape[-1]
    groups = [(x_prompt.shape[0], x_prompt.shape[1], x_prompt.reshape(-1, d)),
              (x_sample.shape[0], x_sample.shape[1], x_sample.reshape(-1, d))]
    for l in range(depth):
        p = _prep_layer(l, P)
        groups = [(B, T, _layer(x, B, T, p)) for B, T, x in groups]
    return tuple(x.reshape(B, T, d) for B, T, x in groups)
```

```python
import functools

import jax
import jax.numpy as jnp
from jax import lax
from jax.experimental import pallas as pl
from jax.experimental.pallas import tpu as pltpu

F32 = jnp.float32
BF16 = jnp.bfloat16

ML_HEADS = 4
NA_HEADS = 4
CONV_K = 31
ML_CHUNK = 128
GRID_W = 64
NA_WIN_ROWS = 8
NA_WIN_COLS = 16
N_GROUPS = 8
EXPERTS_PER_GROUP = 8
N_EXPERTS = N_GROUPS * EXPERTS_PER_GROUP
TOP_K = 2
EPS = 1e-6

LANES = 128
VMEM_LIMIT = 48 * 1024 * 1024

TOKEN_TILE = 512
EXPERT_BLOCK = 256


def _cparams(*sem):
    return pltpu.CompilerParams(dimension_semantics=sem, vmem_limit_bytes=VMEM_LIMIT)


def _full(shape):
    return pl.BlockSpec(shape, lambda *_: (0,) * len(shape))


def _rows(tm, width):
    return pl.BlockSpec((tm, width), lambda i: (i, 0))


def _group_mean(zz, bd):
    hi = zz.astype(BF16)
    lo = (zz - hi.astype(F32)).astype(BF16)
    return (jnp.dot(hi, bd, preferred_element_type=F32) + jnp.dot(lo, bd, preferred_element_type=F32))


def _in_proj_kernel(x_ref, g_ref, w_ref, gb_ref, nq_ref, nk_ref, bd_ref,
                    u_ref, qk_ref, v_ref, o_ref, naq_ref, nak_ref, nav_ref, gt_ref, *, dims):
    cw, mw, nw = dims
    x = x_ref[...]
    ms = jnp.mean(x * x, axis=-1, keepdims=True)
    xn = (x * lax.rsqrt(ms + EPS) * g_ref[...]).astype(BF16)

    def proj(lo, width):
        return jnp.dot(xn, w_ref[:, lo:lo + width], preferred_element_type=F32)

    zc = proj(0, 2 * cw)
    u_ref[...] = zc[:, :cw] * jax.nn.sigmoid(zc[:, cw:])
    off = 2 * cw
    qk_ref[...] = proj(off, 2 * mw)
    v_ref[...] = proj(off + 2 * mw, mw).astype(BF16)
    o_ref[...] = proj(off + 3 * mw, mw)
    off += 4 * mw
    bd = bd_ref[...]
    zq = proj(off, nw)
    zk = proj(off + nw, nw)
    hd = nw // NA_HEADS
    naq_ref[...] = (zq * lax.rsqrt(_group_mean(zq * zq, bd) + EPS) * nq_ref[...] * (hd ** -0.5)).astype(BF16)
    nak_ref[...] = (zk * lax.rsqrt(_group_mean(zk * zk, bd) + EPS) * nk_ref[...]).astype(BF16)
    nav_ref[...] = proj(off + 2 * nw, nw).astype(BF16)
    off += 3 * nw
    gt_ref[...] = proj(off, LANES) + gb_ref[...]


def _in_proj(x, g, w, gate_b, nq, nk, bd, dims):
    n, d = x.shape
    cw, mw, nw = dims
    tm = min(TOKEN_TILE, n)
    outs = [(cw, F32), (2 * mw, F32), (mw, BF16), (mw, F32), (nw, BF16), (nw, BF16), (nw, BF16), (LANES, F32)]
    return pl.pallas_call(
        functools.partial(_in_proj_kernel, dims=dims),
        grid=(n // tm,),
        in_specs=[_rows(tm, d), _full(g.shape), _full(w.shape), _full(gate_b.shape),
                  _full(nq.shape), _full(nk.shape), _full(bd.shape)],
        out_specs=[_rows(tm, wd) for wd, _ in outs],
        out_shape=[jax.ShapeDtypeStruct((n, wd), dt) for wd, dt in outs],
        compiler_params=_cparams("parallel"),
        name="in_proj",
    )(x, g, w, gate_b, nq, nk, bd)


def _out_proj_kernel(yc_ref, hf_ref, hb_ref, o_ref, yna_ref, x_ref, mg_ref, w_ref, fg_ref, wr_ref, br_ref,
                     x1_ref, xn_ref, lg_ref, *, dims):
    cw, mw, nw = dims
    hd = mw // ML_HEADS
    hm = hf_ref[...] + hb_ref[...]
    parts = []
    for h in range(ML_HEADS):
        seg = hm[:, h * hd:(h + 1) * hd]
        parts.append(seg * lax.rsqrt(jnp.mean(seg * seg, axis=-1, keepdims=True) + EPS))
    yml = jnp.concatenate(parts, axis=-1) * mg_ref[...] * jax.nn.sigmoid(o_ref[...])
    y = jnp.dot(yc_ref[...].astype(BF16), w_ref[0:cw, :], preferred_element_type=F32)
    y += jnp.dot(yml.astype(BF16), w_ref[cw:cw + mw, :], preferred_element_type=F32)
    y += jnp.dot(yna_ref[...].astype(BF16), w_ref[cw + mw:cw + mw + nw, :], preferred_element_type=F32)
    x1 = x_ref[...] + y
    x1_ref[...] = x1
    xn = x1 * lax.rsqrt(jnp.mean(x1 * x1, axis=-1, keepdims=True) + EPS) * fg_ref[...]
    xn_ref[...] = xn
    lg_ref[...] = jnp.dot(xn, wr_ref[...], preferred_element_type=F32, precision=lax.Precision.HIGHEST) + br_ref[...]


def _out_proj(yc, hf, hb, o, yna, x, mg, w, fg, wr, br, dims):
    n, d = x.shape
    cw, mw, nw = dims
    tm = min(TOKEN_TILE, n)
    return pl.pallas_call(
        functools.partial(_out_proj_kernel, dims=dims),
        grid=(n // tm,),
        in_specs=[_rows(tm, cw), _rows(tm, mw), _rows(tm, mw), _rows(tm, mw), _rows(tm, nw), _rows(tm, d),
                  _full(mg.shape), _full(w.shape), _full(fg.shape), _full(wr.shape), _full(br.shape)],
        out_specs=[_rows(tm, d), _rows(tm, d), _rows(tm, LANES)],
        out_shape=[jax.ShapeDtypeStruct((n, d), F32), jax.ShapeDtypeStruct((n, d), F32),
                   jax.ShapeDtypeStruct((n, LANES), F32)],
        compiler_params=_cparams("parallel"),
        name="out_proj",
    )(yc, hf, hb, o, yna, x, mg, w, fg, wr, br)


def _expert_kernel(be_ref, nb_ref, x_ref, wg_ref, wu_ref, wd_ref, y_ref):
    i = pl.program_id(0)

    @pl.when(i < nb_ref[0])
    def _():
        xb = x_ref[...].astype(BF16)
        g = jnp.dot(xb, wg_ref[0], preferred_element_type=F32)
        u = jnp.dot(xb, wu_ref[0], preferred_element_type=F32)
        h = (g * jax.nn.sigmoid(g) * u).astype(BF16)
        y_ref[...] = jnp.dot(h, wd_ref[0], preferred_element_type=F32)

    @pl.when(i >= nb_ref[0])
    def _():
        y_ref[...] = jnp.zeros_like(y_ref)


def _experts(buf, blk_e, n_used, wg, wu, wd):
    p, d = buf.shape
    de = wg.shape[-1]
    bm = EXPERT_BLOCK
    grid_spec = pltpu.PrefetchScalarGridSpec(
        num_scalar_prefetch=2,
        grid=(p // bm,),
        in_specs=[pl.BlockSpec((bm, d), lambda i, be, nb: (i, 0)),
                  pl.BlockSpec((1, d, de), lambda i, be, nb: (be[i], 0, 0)),
                  pl.BlockSpec((1, d, de), lambda i, be, nb: (be[i], 0, 0)),
                  pl.BlockSpec((1, de, d), lambda i, be, nb: (be[i], 0, 0))],
        out_specs=pl.BlockSpec((bm, d), lambda i, be, nb: (i, 0)),
    )
    return pl.pallas_call(
        _expert_kernel,
        grid_spec=grid_spec,
        out_shape=jax.ShapeDtypeStruct((p, d), F32),
        compiler_params=_cparams("arbitrary"),
        name="experts",
    )(blk_e, n_used, buf, wg, wu, wd)


NA_QROWS = 4
NEG_BIG = -1e30


def _na_bias_table(rpb):
    a = jnp.arange(NA_QROWS)[:, None]
    i = jnp.arange(3 * NA_QROWS)[None, :]
    half = NA_WIN_ROWS // 2
    a0 = 0 * a
    row_ok = jnp.stack([(i >= a0 + NA_QROWS) & (i < a0 + NA_QROWS + NA_WIN_ROWS),
                        (i >= a + NA_QROWS - half) & (i < a + NA_QROWS - half + NA_WIN_ROWS),
                        (i >= a0 + 2 * NA_QROWS - NA_WIN_ROWS) & (i < a0 + 2 * NA_QROWS)])
    ro = i - NA_QROWS - a + (NA_WIN_ROWS - 1)
    c = jnp.arange(GRID_W)
    cs = jnp.clip(c - NA_WIN_COLS // 2, 0, GRID_W - NA_WIN_COLS)
    col_ok = (c[None, :] >= cs[:, None]) & (c[None, :] < cs[:, None] + NA_WIN_COLS)
    co = jnp.clip(c[None, :] - c[:, None] + (NA_WIN_COLS - 1), 0, 2 * NA_WIN_COLS - 2)
    bias = rpb.astype(F32)[:, jnp.clip(ro, 0, 2 * NA_WIN_ROWS - 2)[:, None, :, None], co[None, :, None, :]]
    ok = row_ok[:, None, :, None, :, None] & col_ok[None, None, None, :, None, :]
    tab = jnp.where(ok, bias[None], NEG_BIG)
    h = rpb.shape[0]
    return tab.reshape(3, h, NA_QROWS * GRID_W, 3 * NA_QROWS * GRID_W)


def _na_kernel(q_ref, kp_ref, kc_ref, kn_ref, vp_ref, vc_ref, vn_ref, b_ref, o_ref):
    q = q_ref[...]
    kcat = jnp.concatenate([kp_ref[...], kc_ref[...], kn_ref[...]], axis=0)
    vcat = jnp.concatenate([vp_ref[...], vc_ref[...], vn_ref[...]], axis=0)
    nw = q.shape[-1]
    hd = nw // NA_HEADS
    lane_head = lax.broadcasted_iota(jnp.int32, q.shape, 1) // hd
    acc = jnp.zeros(q.shape, F32)
    for h in range(NA_HEADS):
        sel = lane_head == h
        qm = jnp.where(sel, q, jnp.zeros_like(q))
        s = lax.dot_general(qm, kcat, (((1,), (1,)), ((), ())), preferred_element_type=F32) + b_ref[0, h]
        m = jnp.max(s, axis=-1, keepdims=True)
        p = jnp.exp(s - m)
        l = jnp.sum(p, axis=-1, keepdims=True)
        oh = jnp.dot(p.astype(BF16), vcat, preferred_element_type=F32)
        acc = jnp.where(sel, oh / l, acc)
    o_ref[...] = acc.astype(o_ref.dtype)


def _neigh_attn_pallas(q, k, v, table, B, T):
    n, nw = q.shape
    tq = NA_QROWS * GRID_W
    J = T // tq
    assert T % tq == 0 and J >= 2 and T // GRID_W >= NA_WIN_ROWS
    blk = (tq, nw)
    cur = pl.BlockSpec(blk, lambda b, j: (b * J + j, 0))
    prev = pl.BlockSpec(blk, lambda b, j: (b * J + jnp.maximum(j - 1, 0), 0))
    nxt = pl.BlockSpec(blk, lambda b, j: (b * J + jnp.minimum(j + 1, J - 1), 0))
    tab = pl.BlockSpec((1,) + table.shape[1:],
                       lambda b, j: (jnp.where(j == 0, 0, jnp.where(j == J - 1, 2, 1)), 0, 0, 0))
    return pl.pallas_call(
        _na_kernel,
        grid=(B, J),
        in_specs=[cur, prev, cur, nxt, prev, cur, nxt, tab],
        out_specs=cur,
        out_shape=jax.ShapeDtypeStruct((n, nw), BF16),
        compiler_params=_cparams("parallel", "arbitrary"),
        name="neigh_attn",
    )(q, k, k, k, v, v, v, table)


HALO = 16
CONV_ROWS = 64


def _local_conv_kernel(u_ref, up_ref, un_ref, qk_ref, qkp_ref, qkn_ref, cw_ref, cb_ref, cg_ref, mw_ref, mb_ref,
                       yc_ref, q_ref, k_ref, wu_ref, wqk_ref, *, k_scale):
    t = pl.program_id(1)
    nt = pl.num_programs(1)
    tm = u_ref.shape[0]
    keep_p = (t > 0).astype(F32)
    keep_n = (t < nt - 1).astype(F32)
    wu_ref[0:HALO, :] = up_ref[...] * keep_p
    wu_ref[HALO:HALO + tm, :] = u_ref[...]
    wu_ref[HALO + tm:, :] = un_ref[...] * keep_n
    wqk_ref[0:HALO, :] = qkp_ref[...] * keep_p
    wqk_ref[HALO:HALO + tm, :] = qk_ref[...]
    wqk_ref[HALO + tm:, :] = qkn_ref[...] * keep_n
    kc = cw_ref.shape[0]
    km = mw_ref.shape[0]
    mw2 = q_ref.shape[-1]
    for r0 in range(0, tm, CONV_ROWS):
        acc = jnp.zeros((CONV_ROWS, u_ref.shape[-1]), F32) + cb_ref[...]
        for k in range(kc):
            s = r0 + HALO - kc // 2 + k
            acc = acc + wu_ref[s:s + CONV_ROWS, :] * cw_ref[k:k + 1, :]
        y = acc * lax.rsqrt(jnp.mean(acc * acc, axis=-1, keepdims=True) + EPS) * cg_ref[...]
        yc_ref[r0:r0 + CONV_ROWS, :] = (y * jax.nn.sigmoid(y)).astype(yc_ref.dtype)
        acc = jnp.zeros((CONV_ROWS, qk_ref.shape[-1]), F32) + mb_ref[...]
        for k in range(km):
            s = r0 + HALO - km // 2 + k
            acc = acc + wqk_ref[s:s + CONV_ROWS, :] * mw_ref[k:k + 1, :]
        y = acc * jax.nn.sigmoid(acc)
        q_ref[r0:r0 + CONV_ROWS, :] = y[:, :mw2].astype(q_ref.dtype)
        k_ref[r0:r0 + CONV_ROWS, :] = (y[:, mw2:] * k_scale).astype(k_ref.dtype)


def _local_conv(u, qkpre, cw, cb, cg, mw, mb, B, T):
    n, cwid = u.shape
    qw = qkpre.shape[-1]
    tm = min(TOKEN_TILE, T)
    nt = T // tm
    hb = tm // HALO
    nhb = n // HALO
    assert T % tm == 0 and tm % CONV_ROWS == 0 and CONV_K // 2 <= HALO

    def cur(w):
        return pl.BlockSpec((tm, w), lambda b, t: (b * nt + t, 0))

    def prev(w):
        return pl.BlockSpec((HALO, w), lambda b, t: (jnp.maximum((b * nt + t) * hb - 1, 0), 0))

    def nxt(w):
        return pl.BlockSpec((HALO, w), lambda b, t: (jnp.minimum((b * nt + t + 1) * hb, nhb - 1), 0))

    hd = (qw // 2) // ML_HEADS
    return pl.pallas_call(
        functools.partial(_local_conv_kernel, k_scale=hd ** -0.5),
        grid=(B, nt),
        in_specs=[cur(cwid), prev(cwid), nxt(cwid), cur(qw), prev(qw), nxt(qw),
                  _full(cw.shape), _full(cb.shape), _full(cg.shape), _full(mw.shape), _full(mb.shape)],
        out_specs=[cur(cwid), cur(qw // 2), cur(qw // 2)],
        out_shape=[jax.ShapeDtypeStruct((n, cwid), BF16), jax.ShapeDtypeStruct((n, qw // 2), BF16),
                   jax.ShapeDtypeStruct((n, qw // 2), BF16)],
        scratch_shapes=[pltpu.VMEM((tm + 2 * HALO, cwid), F32), pltpu.VMEM((tm + 2 * HALO, qw), F32)],
        compiler_params=_cparams("parallel", "parallel"),
        name="local_conv",
    )(u, u, u, qkpre, qkpre, qkpre, cw, cb, cg, mw, mb)


def _mlstm_kernel(q_ref, k_ref, v_ref, g_ref, h_ref, c_ref, n_ref, m_ref, *, reverse):
    L = q_ref.shape[0]
    hd = q_ref.shape[-1] // ML_HEADS

    @pl.when(pl.program_id(1) == 0)
    def _():
        c_ref[...] = jnp.zeros_like(c_ref)
        n_ref[...] = jnp.zeros_like(n_ref)
        m_ref[...] = jnp.zeros_like(m_ref)

    ci = 2 * ML_HEADS if reverse else 0
    cf = ci + ML_HEADS
    g = g_ref[...]
    lf = jnp.minimum(g, 0.0) - jnp.log1p(jnp.exp(-jnp.abs(g)))
    row = lax.broadcasted_iota(jnp.int32, g.shape, 0)
    b = lf
    s = 1
    while s < L:
        if reverse:
            b = b + jnp.where(row < L - s, pltpu.roll(b, L - s, axis=0), 0.0)
        else:
            b = b + jnp.where(row >= s, pltpu.roll(b, s, axis=0), 0.0)
        s *= 2
    total = b[0:1, :] if reverse else b[L - 1:L, :]
    bt = jnp.transpose(b)
    gt = jnp.transpose(g)
    jj = lax.broadcasted_iota(jnp.int32, (L, L), 0)
    ss = lax.broadcasted_iota(jnp.int32, (L, L), 1)
    visible = (ss >= jj) if reverse else (ss <= jj)
    outs = []
    for h in range(ML_HEADS):
        sl = slice(h * hd, (h + 1) * hd)
        qh = q_ref[:, sl]
        kh = k_ref[:, sl]
        vh = v_ref[:, sl]
        b_col = b[:, cf + h:cf + h + 1]
        i_col = g[:, ci + h:ci + h + 1]
        b_row = bt[cf + h:cf + h + 1, :]
        i_row = gt[ci + h:ci + h + 1, :]
        tot = total[:, cf + h:cf + h + 1]
        m_prev = m_ref[h][:, 0:1]
        c_prev = c_ref[h]
        n_prev = n_ref[h]
        d = jnp.where(visible, b_col - b_row + i_row, NEG_BIG)
        inter = b_col + m_prev
        m_j = jnp.maximum(inter, jnp.max(d, axis=-1, keepdims=True))
        pm = jnp.exp(d - m_j)
        e_inter = jnp.exp(inter - m_j)
        qk = lax.dot_general(qh, kh, (((1,), (1,)), ((), ())), preferred_element_type=F32) * pm
        num = (jnp.dot(qk.astype(BF16), vh, preferred_element_type=F32)
               + e_inter * jnp.dot(qh, c_prev.astype(BF16), preferred_element_type=F32))
        den = (jnp.sum(qk, axis=-1, keepdims=True)
               + e_inter * jnp.sum(qh.astype(F32) * n_prev, axis=-1, keepdims=True))
        outs.append(num / jnp.maximum(jnp.abs(den), jnp.exp(-m_j)))
        w = tot - b_col + i_col
        m_loc = jnp.max(w, axis=0, keepdims=True)
        ka = kh.astype(F32) * jnp.exp(w - m_loc)
        s_c = lax.dot_general(ka.astype(BF16), vh, (((0,), (0,)), ((), ())), preferred_element_type=F32)
        s_n = jnp.sum(ka, axis=0, keepdims=True)
        m_new = jnp.maximum(tot + m_prev, m_loc)
        dec = jnp.exp(tot + m_prev - m_new)
        inc = jnp.exp(m_loc - m_new)
        c_ref[h] = dec * c_prev + inc * s_c
        n_ref[h] = dec * n_prev + inc * s_n
        m_ref[h] = jnp.broadcast_to(m_new, m_ref.shape[1:])
    h_ref[...] = jnp.concatenate(outs, axis=-1)


def _mlstm(q, k, v, gates, B, T, reverse):
    n, mw = q.shape
    L = ML_CHUNK
    nc = T // L
    hd = mw // ML_HEADS
    assert T % L == 0

    def blk(w):
        if reverse:
            return pl.BlockSpec((L, w), lambda b, c: (b * nc + nc - 1 - c, 0))
        return pl.BlockSpec((L, w), lambda b, c: (b * nc + c, 0))

    return pl.pallas_call(
        functools.partial(_mlstm_kernel, reverse=reverse),
        grid=(B, nc),
        in_specs=[blk(mw), blk(mw), blk(mw), blk(LANES)],
        out_specs=blk(mw),
        out_shape=jax.ShapeDtypeStruct((n, mw), F32),
        scratch_shapes=[pltpu.VMEM((ML_HEADS, hd, hd), F32), pltpu.VMEM((ML_HEADS, 1, hd), F32),
                        pltpu.VMEM((ML_HEADS, 1, LANES), F32)],
        compiler_params=_cparams("parallel", "arbitrary"),
        name="mlstm_bwd" if reverse else "mlstm_fwd",
    )(q, k, v, gates)


def _mixers(B, T, u, qkpre, v, naq, nak, nav, gates, p):
    yc, q, k = _local_conv(u, qkpre, p['conv_dw_w'], p['conv_dw_b'], p['conv_norm_g'],
                           p['ml_qk_conv_w'], p['ml_qk_conv_b'], B, T)
    hf = _mlstm(q, k, v, gates, B, T, False)
    hb = _mlstm(q, k, v, gates, B, T, True)
    y_na = _neigh_attn_pallas(naq, nak, nav, p['na_table'], B, T)
    return yc, hf, hb, y_na


def _route_xla(logits):
    n = logits.shape[0]
    gp = jax.nn.softmax(logits[:, :N_GROUPS], axis=-1)
    pg, gsel = lax.top_k(gp, 1)
    el = logits[:, N_GROUPS:N_GROUPS + N_EXPERTS].reshape(n, N_GROUPS, EXPERTS_PER_GROUP)
    el = jnp.take_along_axis(el, gsel[:, :, None], axis=1)[:, 0]
    pe, esel = lax.top_k(jax.nn.softmax(el, axis=-1), TOP_K)
    gate = pg * pe / jnp.sum(pe, axis=-1, keepdims=True)
    eid = gsel * EXPERTS_PER_GROUP + esel
    oh = jax.nn.one_hot(eid, N_EXPERTS, dtype=jnp.int32).sum(axis=1)
    prefix = jnp.cumsum(oh, axis=0) - oh
    rank = jnp.take_along_axis(prefix, eid, axis=1)
    counts = oh.sum(axis=0)
    return eid, gate, rank, counts


def _moe(x1, xn, logits, wg, wu, wd):
    n, d = x1.shape
    bm = EXPERT_BLOCK
    eid, gate, rank, counts = _route_xla(logits)
    psz = (counts + bm - 1) // bm * bm
    pend = jnp.cumsum(psz)
    pstart = pend - psz
    dest = pstart[eid] + rank
    n_blocks = -(-(n * TOP_K) // bm) + N_EXPERTS
    blk_e = jnp.minimum(jnp.searchsorted(pend, jnp.arange(n_blocks) * bm, side='right'), N_EXPERTS - 1).astype(jnp.int32)
    n_used = (pend[-1] // bm).astype(jnp.int32).reshape(1)
    buf = jnp.zeros((n_blocks * bm, d), F32).at[dest.reshape(-1)].set(jnp.repeat(xn, TOP_K, axis=0))
    y = _experts(buf, blk_e, n_used, wg, wu, wd)
    return x1 + (y[dest[:, 0]] * gate[:, 0:1] + y[dest[:, 1]] * gate[:, 1:2])


def _prep_layer(l, P):
    d = P['w_in'].shape[1]
    cw = P['conv_dw_w'].shape[-1]
    mw = P['ml_norm_g'].shape[-1]
    nw = NA_HEADS * P['na_q_norm_g'].shape[-1]
    off_g = 2 * cw + 4 * mw
    ng = 4 * ML_HEADS
    w = P['w_in'][l]
    w_perm = jnp.concatenate([w[:, :off_g], w[:, off_g + ng:], w[:, off_g:off_g + ng],
                              jnp.zeros((d, LANES - ng), F32)], axis=1).astype(BF16)
    nhd = nw // NA_HEADS
    head_id = jnp.arange(nw) // nhd
    bd = jnp.where(head_id[:, None] == head_id[None, :], 1.0 / nhd, 0.0).astype(BF16)
    wr = jnp.concatenate([P['w_router_group'][l], P['w_router_expert'][l],
                          jnp.zeros((d, LANES - N_GROUPS - N_EXPERTS), F32)], axis=1)
    br = jnp.concatenate([P['b_router_group'][l], P['b_router_expert'][l],
                          jnp.zeros((LANES - N_GROUPS - N_EXPERTS,), F32)])[None, :]
    return dict(
        dims=(cw, mw, nw),
        norm_mix_g=P['norm_mix_g'][l][None, :], w_in=w_perm,
        gate_b=jnp.pad(P['ml_gate_b'][l].reshape(-1), (0, LANES - ng))[None, :],
        nq=jnp.tile(P['na_q_norm_g'][l], NA_HEADS)[None, :], nk=jnp.tile(P['na_k_norm_g'][l], NA_HEADS)[None, :],
        bd=bd,
        conv_dw_w=P['conv_dw_w'][l], conv_dw_b=P['conv_dw_b'][l][None, :], conv_norm_g=P['conv_norm_g'][l][None, :],
        ml_qk_conv_w=P['ml_qk_conv_w'][l], ml_qk_conv_b=P['ml_qk_conv_b'][l][None, :],
        ml_norm_g=P['ml_norm_g'][l][None, :], na_table=_na_bias_table(P['na_rpb'][l]),
        w_out=P['w_out'][l].astype(BF16), norm_ffn_g=P['norm_ffn_g'][l][None, :], wr=wr, br=br,
        wg=P['w_exp_gate'][l].astype(BF16), wu=P['w_exp_up'][l].astype(BF16), wd=P['w_exp_down'][l].astype(BF16),
    )


def _layer(x, B, T, p):
    dims = p['dims']
    u, qkpre, v, o, naq, nak, nav, gates = _in_proj(x, p['norm_mix_g'], p['w_in'], p['gate_b'], p['nq'], p['nk'],
                                                    p['bd'], dims)
    yc, hf, hb, yna = _mixers(B, T, u, qkpre, v, naq, nak, nav, gates, p)
    x1, xn, logits = _out_proj(yc, hf, hb, o, yna, x, p['ml_norm_g'], p['w_out'], p['norm_ffn_g'], p['wr'], p['br'],
                               dims)
    return _moe(x1, xn, logits, p['wg'], p['wu'], p['wd'])


def kernel(x_prompt, x_sample, norm_mix_g, w_in, conv_dw_w, conv_dw_b, conv_norm_g, ml_qk_conv_w, ml_qk_conv_b,
           ml_gate_b, ml_norm_g, na_q_norm_g, na_k_norm_g, na_rpb, w_out, norm_ffn_g, w_router_group,
           b_router_group, w_router_expert, b_router_expert, w_exp_gate, w_exp_up, w_exp_down):
    P = dict(norm_mix_g=norm_mix_g, w_in=w_in, conv_dw_w=conv_dw_w, conv_dw_b=conv_dw_b, conv_norm_g=conv_norm_g,
             ml_qk_conv_w=ml_qk_conv_w, ml_qk_conv_b=ml_qk_conv_b, ml_gate_b=ml_gate_b, ml_norm_g=ml_norm_g,
             na_q_norm_g=na_q_norm_g, na_k_norm_g=na_k_norm_g, na_rpb=na_rpb, w_out=w_out, norm_ffn_g=norm_ffn_g,
             w_router_group=w_router_group, b_router_group=b_router_group, w_router_expert=w_router_expert,
             b_router_expert=b_router_expert, w_exp_gate=w_exp_gate, w_exp_up=w_exp_up, w_exp_down=w_exp_down)
    depth = w_in.shape[0]
    d = x_prompt.shape[-1]
    groups = [(x_prompt.shape[0], x_prompt.shape[1], x_prompt.reshape(-1, d)),
              (x_sample.shape[0], x_sample.shape[1], x_sample.reshape(-1, d))]
    for l in range(depth):
        p = _prep_layer(l, P)
        groups = [(B, T, _layer(x, B, T, p)) for B, T, x in groups]
    return tuple(x.reshape(B, T, d) for B, T, x in groups)
```

```python
import functools

import jax
import jax.numpy as jnp
from jax import lax
from jax.experimental import pallas as pl
from jax.experimental.pallas import tpu as pltpu

F32 = jnp.float32
BF16 = jnp.bfloat16

ML_HEADS = 4
NA_HEADS = 4
CONV_K = 31
ML_CHUNK = 128
GRID_W = 64
NA_WIN_ROWS = 8
NA_WIN_COLS = 16
N_GROUPS = 8
EXPERTS_PER_GROUP = 8
N_EXPERTS = N_GROUPS * EXPERTS_PER_GROUP
TOP_K = 2
EPS = 1e-6

LANES = 128
VMEM_LIMIT = 48 * 1024 * 1024

TOKEN_TILE = 512
EXPERT_BLOCK = 256


def _cparams(*sem):
    return pltpu.CompilerParams(dimension_semantics=sem, vmem_limit_bytes=VMEM_LIMIT)


def _full(shape):
    return pl.BlockSpec(shape, lambda *_: (0,) * len(shape))


def _rows(tm, width):
    return pl.BlockSpec((tm, width), lambda i: (i, 0))


def _group_mean(zz, bd):
    hi = zz.astype(BF16)
    lo = (zz - hi.astype(F32)).astype(BF16)
    return (jnp.dot(hi, bd, preferred_element_type=F32) + jnp.dot(lo, bd, preferred_element_type=F32))


def _in_proj_kernel(x_ref, g_ref, w_ref, gb_ref, nq_ref, nk_ref, bd_ref,
                    u_ref, qk_ref, v_ref, o_ref, naq_ref, nak_ref, nav_ref, gt_ref, *, dims):
    cw, mw, nw = dims
    x = x_ref[...]
    ms = jnp.mean(x * x, axis=-1, keepdims=True)
    xn = (x * lax.rsqrt(ms + EPS) * g_ref[...]).astype(BF16)

    def proj(lo, width):
        return jnp.dot(xn, w_ref[:, lo:lo + width], preferred_element_type=F32)

    zc = proj(0, 2 * cw)
    u_ref[...] = zc[:, :cw] * jax.nn.sigmoid(zc[:, cw:])
    off = 2 * cw
    qk_ref[...] = proj(off, 2 * mw)
    v_ref[...] = proj(off + 2 * mw, mw).astype(BF16)
    o_ref[...] = proj(off + 3 * mw, mw)
    off += 4 * mw
    bd = bd_ref[...]
    zq = proj(off, nw)
    zk = proj(off + nw, nw)
    hd = nw // NA_HEADS
    naq_ref[...] = (zq * lax.rsqrt(_group_mean(zq * zq, bd) + EPS) * nq_ref[...] * (hd ** -0.5)).astype(BF16)
    nak_ref[...] = (zk * lax.rsqrt(_group_mean(zk * zk, bd) + EPS) * nk_ref[...]).astype(BF16)
    nav_ref[...] = proj(off + 2 * nw, nw).astype(BF16)
    off += 3 * nw
    gt_ref[...] = proj(off, LANES) + gb_ref[...]


def _in_proj(x, g, w, gate_b, nq, nk, bd, dims):
    n, d = x.shape
    cw, mw, nw = dims
    tm = min(TOKEN_TILE, n)
    outs = [(cw, F32), (2 * mw, F32), (mw, BF16), (mw, F32), (nw, BF16), (nw, BF16), (nw, BF16), (LANES, F32)]
    return pl.pallas_call(
        functools.partial(_in_proj_kernel, dims=dims),
        grid=(n // tm,),
        in_specs=[_rows(tm, d), _full(g.shape), _full(w.shape), _full(gate_b.shape),
                  _full(nq.shape), _full(nk.shape), _full(bd.shape)],
        out_specs=[_rows(tm, wd) for wd, _ in outs],
        out_shape=[jax.ShapeDtypeStruct((n, wd), dt) for wd, dt in outs],
        compiler_params=_cparams("parallel"),
        name="in_proj",
    )(x, g, w, gate_b, nq, nk, bd)


def _out_proj_kernel(yc_ref, hf_ref, hb_ref, o_ref, yna_ref, x_ref, mg_ref, w_ref, fg_ref, wr_ref, br_ref,
                     x1_ref, xn_ref, lg_ref, *, dims):
    cw, mw, nw = dims
    hd = mw // ML_HEADS
    hm = hf_ref[...] + hb_ref[...]
    parts = []
    for h in range(ML_HEADS):
        seg = hm[:, h * hd:(h + 1) * hd]
        parts.append(seg * lax.rsqrt(jnp.mean(seg * seg, axis=-1, keepdims=True) + EPS))
    yml = jnp.concatenate(parts, axis=-1) * mg_ref[...] * jax.nn.sigmoid(o_ref[...])
    y = jnp.dot(yc_ref[...].astype(BF16), w_ref[0:cw, :], preferred_element_type=F32)
    y += jnp.dot(yml.astype(BF16), w_ref[cw:cw + mw, :], preferred_element_type=F32)
    y += jnp.dot(yna_ref[...].astype(BF16), w_ref[cw + mw:cw + mw + nw, :], preferred_element_type=F32)
    x1 = x_ref[...] + y
    x1_ref[...] = x1
    xn = x1 * lax.rsqrt(jnp.mean(x1 * x1, axis=-1, keepdims=True) + EPS) * fg_ref[...]
    xn_ref[...] = xn
    lg_ref[...] = jnp.dot(xn, wr_ref[...], preferred_element_type=F32, precision=lax.Precision.HIGHEST) + br_ref[...]


def _out_proj(yc, hf, hb, o, yna, x, mg, w, fg, wr, br, dims):
    n, d = x.shape
    cw, mw, nw = dims
    tm = min(TOKEN_TILE, n)
    return pl.pallas_call(
        functools.partial(_out_proj_kernel, dims=dims),
        grid=(n // tm,),
        in_specs=[_rows(tm, cw), _rows(tm, mw), _rows(tm, mw), _rows(tm, mw), _rows(tm, nw), _rows(tm, d),
                  _full(mg.shape), _full(w.shape), _full(fg.shape), _full(wr.shape), _full(br.shape)],
        out_specs=[_rows(tm, d), _rows(tm, d), _rows(tm, LANES)],
        out_shape=[jax.ShapeDtypeStruct((n, d), F32), jax.ShapeDtypeStruct((n, d), F32),
                   jax.ShapeDtypeStruct((n, LANES), F32)],
        compiler_params=_cparams("parallel"),
        name="out_proj",
    )(yc, hf, hb, o, yna, x, mg, w, fg, wr, br)


def _expert_kernel(be_ref, nb_ref, x_ref, wg_ref, wu_ref, wd_ref, y_ref):
    i = pl.program_id(0)

    @pl.when(i < nb_ref[0])
    def _():
        xb = x_ref[...].astype(BF16)
        g = jnp.dot(xb, wg_ref[0], preferred_element_type=F32)
        u = jnp.dot(xb, wu_ref[0], preferred_element_type=F32)
        h = (g * jax.nn.sigmoid(g) * u).astype(BF16)
        y_ref[...] = jnp.dot(h, wd_ref[0], preferred_element_type=F32)

    @pl.when(i >= nb_ref[0])
    def _():
        y_ref[...] = jnp.zeros_like(y_ref)


def _experts(buf, blk_e, n_used, wg, wu, wd):
    p, d = buf.shape
    de = wg.shape[-1]
    bm = EXPERT_BLOCK
    grid_spec = pltpu.PrefetchScalarGridSpec(
        num_scalar_prefetch=2,
        grid=(p // bm,),
        in_specs=[pl.BlockSpec((bm, d), lambda i, be, nb: (i, 0)),
                  pl.BlockSpec((1, d, de), lambda i, be, nb: (be[i], 0, 0)),
                  pl.BlockSpec((1, d, de), lambda i, be, nb: (be[i], 0, 0)),
                  pl.BlockSpec((1, de, d), lambda i, be, nb: (be[i], 0, 0))],
        out_specs=pl.BlockSpec((bm, d), lambda i, be, nb: (i, 0)),
    )
    return pl.pallas_call(
        _expert_kernel,
        grid_spec=grid_spec,
        out_shape=jax.ShapeDtypeStruct((p, d), F32),
        compiler_params=_cparams("arbitrary"),
        name="experts",
    )(blk_e, n_used, buf, wg, wu, wd)


NA_QROWS = 4
NEG_BIG = -1e30


def _na_bias_table(rpb):
    a = jnp.arange(NA_QROWS)[:, None]
    i = jnp.arange(3 * NA_QROWS)[None, :]
    half = NA_WIN_ROWS // 2
    a0 = 0 * a
    row_ok = jnp.stack([(i >= a0 + NA_QROWS) & (i < a0 + NA_QROWS + NA_WIN_ROWS),
                        (i >= a + NA_QROWS - half) & (i < a + NA_QROWS - half + NA_WIN_ROWS),
                        (i >= a0 + 2 * NA_QROWS - NA_WIN_ROWS) & (i < a0 + 2 * NA_QROWS)])
    ro = i - NA_QROWS - a + (NA_WIN_ROWS - 1)
    c = jnp.arange(GRID_W)
    cs = jnp.clip(c - NA_WIN_COLS // 2, 0, GRID_W - NA_WIN_COLS)
    col_ok = (c[None, :] >= cs[:, None]) & (c[None, :] < cs[:, None] + NA_WIN_COLS)
    co = jnp.clip(c[None, :] - c[:, None] + (NA_WIN_COLS - 1), 0, 2 * NA_WIN_COLS - 2)
    bias = rpb.astype(F32)[:, jnp.clip(ro, 0, 2 * NA_WIN_ROWS - 2)[:, None, :, None], co[None, :, None, :]]
    ok = row_ok[:, None, :, None, :, None] & col_ok[None, None, None, :, None, :]
    tab = jnp.where(ok, bias[None], NEG_BIG)
    h = rpb.shape[0]
    return tab.reshape(3, h, NA_QROWS * GRID_W, 3 * NA_QROWS * GRID_W)


def _na_kernel(q_ref, kp_ref, kc_ref, kn_ref, vp_ref, vc_ref, vn_ref, b_ref, o_ref):
    q = q_ref[...]
    kcat = jnp.concatenate([kp_ref[...], kc_ref[...], kn_ref[...]], axis=0)
    vcat = jnp.concatenate([vp_ref[...], vc_ref[...], vn_ref[...]], axis=0)
    nw = q.shape[-1]
    hd = nw // NA_HEADS
    lane_head = lax.broadcasted_iota(jnp.int32, q.shape, 1) // hd
    acc = jnp.zeros(q.shape, F32)
    for h in range(NA_HEADS):
        sel = lane_head == h
        qm = jnp.where(sel, q, jnp.zeros_like(q))
        s = lax.dot_general(qm, kcat, (((1,), (1,)), ((), ())), preferred_element_type=F32) + b_ref[0, h]
        m = jnp.max(s, axis=-1, keepdims=True)
        p = jnp.exp(s - m)
        l = jnp.sum(p, axis=-1, keepdims=True)
        oh = jnp.dot(p.astype(BF16), vcat, preferred_element_type=F32)
        acc = jnp.where(sel, oh / l, acc)
    o_ref[...] = acc.astype(o_ref.dtype)


def _neigh_attn_pallas(q, k, v, table, B, T):
    n, nw = q.shape
    tq = NA_QROWS * GRID_W
    J = T // tq
    assert T % tq == 0 and J >= 2 and T // GRID_W >= NA_WIN_ROWS
    blk = (tq, nw)
    cur = pl.BlockSpec(blk, lambda b, j: (b * J + j, 0))
    prev = pl.BlockSpec(blk, lambda b, j: (b * J + jnp.maximum(j - 1, 0), 0))
    nxt = pl.BlockSpec(blk, lambda b, j: (b * J + jnp.minimum(j + 1, J - 1), 0))
    tab = pl.BlockSpec((1,) + table.shape[1:],
                       lambda b, j: (jnp.where(j == 0, 0, jnp.where(j == J - 1, 2, 1)), 0, 0, 0))
    return pl.pallas_call(
        _na_kernel,
        grid=(B, J),
        in_specs=[cur, prev, cur, nxt, prev, cur, nxt, tab],
        out_specs=cur,
        out_shape=jax.ShapeDtypeStruct((n, nw), BF16),
        compiler_params=_cparams("parallel", "arbitrary"),
        name="neigh_attn",
    )(q, k, k, k, v, v, v, table)


HALO = 16
CONV_ROWS = 64


def _local_conv_kernel(u_ref, up_ref, un_ref, qk_ref, qkp_ref, qkn_ref, cw_ref, cb_ref, cg_ref, mw_ref, mb_ref,
                       yc_ref, q_ref, k_ref, wu_ref, wqk_ref, *, k_scale):
    t = pl.program_id(1)
    nt = pl.num_programs(1)
    tm = u_ref.shape[0]
    keep_p = (t > 0).astype(F32)
    keep_n = (t < nt - 1).astype(F32)
    wu_ref[0:HALO, :] = up_ref[...] * keep_p
    wu_ref[HALO:HALO + tm, :] = u_ref[...]
    wu_ref[HALO + tm:, :] = un_ref[...] * keep_n
    wqk_ref[0:HALO, :] = qkp_ref[...] * keep_p
    wqk_ref[HALO:HALO + tm, :] = qk_ref[...]
    wqk_ref[HALO + tm:, :] = qkn_ref[...] * keep_n
    kc = cw_ref.shape[0]
    km = mw_ref.shape[0]
    mw2 = q_ref.shape[-1]
    for r0 in range(0, tm, CONV_ROWS):
        acc = jnp.zeros((CONV_ROWS, u_ref.shape[-1]), F32) + cb_ref[...]
        for k in range(kc):
            s = r0 + HALO - kc // 2 + k
            acc = acc + wu_ref[s:s + CONV_ROWS, :] * cw_ref[k:k + 1, :]
        y = acc * lax.rsqrt(jnp.mean(acc * acc, axis=-1, keepdims=True) + EPS) * cg_ref[...]
        yc_ref[r0:r0 + CONV_ROWS, :] = (y * jax.nn.sigmoid(y)).astype(yc_ref.dtype)
        acc = jnp.zeros((CONV_ROWS, qk_ref.shape[-1]), F32) + mb_ref[...]
        for k in range(km):
            s = r0 + HALO - km // 2 + k
            acc = acc + wqk_ref[s:s + CONV_ROWS, :] * mw_ref[k:k + 1, :]
        y = acc * jax.nn.sigmoid(acc)
        q_ref[r0:r0 + CONV_ROWS, :] = y[:, :mw2].astype(q_ref.dtype)
        k_ref[r0:r0 + CONV_ROWS, :] = (y[:, mw2:] * k_scale).astype(k_ref.dtype)


def _local_conv(u, qkpre, cw, cb, cg, mw, mb, B, T):
    n, cwid = u.shape
    qw = qkpre.shape[-1]
    tm = min(TOKEN_TILE, T)
    nt = T // tm
    hb = tm // HALO
    nhb = n // HALO
    assert T % tm == 0 and tm % CONV_ROWS == 0 and CONV_K // 2 <= HALO

    def cur(w):
        return pl.BlockSpec((tm, w), lambda b, t: (b * nt + t, 0))

    def prev(w):
        return pl.BlockSpec((HALO, w), lambda b, t: (jnp.maximum((b * nt + t) * hb - 1, 0), 0))

    def nxt(w):
        return pl.BlockSpec((HALO, w), lambda b, t: (jnp.minimum((b * nt + t + 1) * hb, nhb - 1), 0))

    hd = (qw // 2) // ML_HEADS
    return pl.pallas_call(
        functools.partial(_local_conv_kernel, k_scale=hd ** -0.5),
        grid=(B, nt),
        in_specs=[cur(cwid), prev(cwid), nxt(cwid), cur(qw), prev(qw), nxt(qw),
                  _full(cw.shape), _full(cb.shape), _full(cg.shape), _full(mw.shape), _full(mb.shape)],
        out_specs=[cur(cwid), cur(qw // 2), cur(qw // 2)],
        out_shape=[jax.ShapeDtypeStruct((n, cwid), BF16), jax.ShapeDtypeStruct((n, qw // 2), BF16),
                   jax.ShapeDtypeStruct((n, qw // 2), BF16)],
        scratch_shapes=[pltpu.VMEM((tm + 2 * HALO, cwid), F32), pltpu.VMEM((tm + 2 * HALO, qw), F32)],
        compiler_params=_cparams("parallel", "parallel"),
        name="local_conv",
    )(u, u, u, qkpre, qkpre, qkpre, cw, cb, cg, mw, mb)


def _mlstm_kernel(q_ref, k_ref, v_ref, g_ref, h_ref, c_ref, n_ref, m_ref, *, reverse):
    L = q_ref.shape[0]
    hd = q_ref.shape[-1] // ML_HEADS

    @pl.when(pl.program_id(1) == 0)
    def _():
        c_ref[...] = jnp.zeros_like(c_ref)
        n_ref[...] = jnp.zeros_like(n_ref)
        m_ref[...] = jnp.zeros_like(m_ref)

    ci = 2 * ML_HEADS if reverse else 0
    cf = ci + ML_HEADS
    g = g_ref[...]
    lf = jnp.minimum(g, 0.0) - jnp.log1p(jnp.exp(-jnp.abs(g)))
    row = lax.broadcasted_iota(jnp.int32, g.shape, 0)
    b = lf
    s = 1
    while s < L:
        if reverse:
            b = b + jnp.where(row < L - s, pltpu.roll(b, L - s, axis=0), 0.0)
        else:
            b = b + jnp.where(row >= s, pltpu.roll(b, s, axis=0), 0.0)
        s *= 2
    total = b[0:1, :] if reverse else b[L - 1:L, :]
    bt = jnp.transpose(b)
    gt = jnp.transpose(g)
    jj = lax.broadcasted_iota(jnp.int32, (L, L), 0)
    ss = lax.broadcasted_iota(jnp.int32, (L, L), 1)
    visible = (ss >= jj) if reverse else (ss <= jj)
    outs = []
    for h in range(ML_HEADS):
        sl = slice(h * hd, (h + 1) * hd)
        qh = q_ref[:, sl]
        kh = k_ref[:, sl]
        vh = v_ref[:, sl]
        b_col = b[:, cf + h:cf + h + 1]
        i_col = g[:, ci + h:ci + h + 1]
        b_row = bt[cf + h:cf + h + 1, :]
        i_row = gt[ci + h:ci + h + 1, :]
        tot = total[:, cf + h:cf + h + 1]
        m_prev = m_ref[h][:, 0:1]
        c_prev = c_ref[h]
        n_prev = n_ref[h]
        d = jnp.where(visible, b_col - b_row + i_row, NEG_BIG)
        inter = b_col + m_prev
        m_j = jnp.maximum(inter, jnp.max(d, axis=-1, keepdims=True))
        pm = jnp.exp(d - m_j)
        e_inter = jnp.exp(inter - m_j)
        qk = lax.dot_general(qh, kh, (((1,), (1,)), ((), ())), preferred_element_type=F32) * pm
        num = (jnp.dot(qk.astype(BF16), vh, preferred_element_type=F32)
               + e_inter * jnp.dot(qh, c_prev.astype(BF16), preferred_element_type=F32))
        den = (jnp.sum(qk, axis=-1, keepdims=True)
               + e_inter * jnp.sum(qh.astype(F32) * n_prev, axis=-1, keepdims=True))
        outs.append(num / jnp.maximum(jnp.abs(den), jnp.exp(-m_j)))
        w = tot - b_col + i_col
        m_loc = jnp.max(w, axis=0, keepdims=True)
        ka = kh.astype(F32) * jnp.exp(w - m_loc)
        s_c = lax.dot_general(ka.astype(BF16), vh, (((0,), (0,)), ((), ())), preferred_element_type=F32)
        s_n = jnp.sum(ka, axis=0, keepdims=True)
        m_new = jnp.maximum(tot + m_prev, m_loc)
        dec = jnp.exp(tot + m_prev - m_new)
        inc = jnp.exp(m_loc - m_new)
        c_ref[h] = dec * c_prev + inc * s_c
        n_ref[h] = dec * n_prev + inc * s_n
        m_ref[h] = jnp.broadcast_to(m_new, m_ref.shape[1:])
    h_ref[...] = jnp.concatenate(outs, axis=-1)


def _mlstm(q, k, v, gates, B, T, reverse):
    n, mw = q.shape
    L = ML_CHUNK
    nc = T // L
    hd = mw // ML_HEADS
    assert T % L == 0

    def blk(w):
        if reverse:
            return pl.BlockSpec((L, w), lambda b, c: (b * nc + nc - 1 - c, 0))
        return pl.BlockSpec((L, w), lambda b, c: (b * nc + c, 0))

    return pl.pallas_call(
        functools.partial(_mlstm_kernel, reverse=reverse),
        grid=(B, nc),
        in_specs=[blk(mw), blk(mw), blk(mw), blk(LANES)],
        out_specs=blk(mw),
        out_shape=jax.ShapeDtypeStruct((n, mw), F32),
        scratch_shapes=[pltpu.VMEM((ML_HEADS, hd, hd), F32), pltpu.VMEM((ML_HEADS, 1, hd), F32),
                        pltpu.VMEM((ML_HEADS, 1, LANES), F32)],
        compiler_params=_cparams("parallel", "arbitrary"),
        name="mlstm_bwd" if reverse else "mlstm_fwd",
    )(q, k, v, gates)


def _mixers(B, T, u, qkpre, v, naq, nak, nav, gates, p):
    yc, q, k = _local_conv(u, qkpre, p['conv_dw_w'], p['conv_dw_b'], p['conv_norm_g'],
                           p['ml_qk_conv_w'], p['ml_qk_conv_b'], B, T)
    hf = _mlstm(q, k, v, gates, B, T, False)
    hb = _mlstm(q, k, v, gates, B, T, True)
    y_na = _neigh_attn_pallas(naq, nak, nav, p['na_table'], B, T)
    return yc, hf, hb, y_na


def _first_lane(mask, lane):
    return jnp.min(jnp.where(mask, lane, LANES), axis=-1, keepdims=True)


def _route_kernel(lg_ref, ri_ref, rf_ref, cnt_ref, carry_ref):
    @pl.when(pl.program_id(0) == 0)
    def _():
        carry_ref[...] = jnp.zeros_like(carry_ref)

    lg = lg_ref[...]
    tm = lg.shape[0]
    lane = lax.broadcasted_iota(jnp.int32, lg.shape, 1)
    gl = jnp.where(lane < N_GROUPS, lg, NEG_BIG)
    gmax = jnp.max(gl, axis=-1, keepdims=True)
    pg = 1.0 / jnp.sum(jnp.exp(gl - gmax), axis=-1, keepdims=True)
    gsel = _first_lane(gl == gmax, lane)
    lo = N_GROUPS + gsel * EXPERTS_PER_GROUP
    in_grp = (lane >= lo) & (lane < lo + EXPERTS_PER_GROUP)
    el = jnp.where(in_grp, lg, NEG_BIG)
    emax = jnp.max(el, axis=-1, keepdims=True)
    esum = jnp.sum(jnp.exp(el - emax), axis=-1, keepdims=True)
    idx1 = _first_lane(in_grp & (el == emax), lane)
    el2 = jnp.where(lane == idx1, NEG_BIG, el)
    emax2 = jnp.max(el2, axis=-1, keepdims=True)
    idx2 = _first_lane(in_grp & (el2 == emax2) & (lane != idx1), lane)
    p1 = 1.0 / esum
    p2 = jnp.exp(emax2 - emax) / esum
    g1 = pg * p1 / (p1 + p2)
    g2 = pg * p2 / (p1 + p2)
    hit1 = lane == idx1
    hit2 = lane == idx2
    onehot = jnp.where(hit1 | hit2, 1.0, 0.0)
    r = lax.broadcasted_iota(jnp.int32, (tm, tm), 0)
    c = lax.broadcasted_iota(jnp.int32, (tm, tm), 1)
    tri = jnp.where(r > c, 1.0, 0.0).astype(BF16)
    prefix = jnp.dot(tri, onehot.astype(BF16), preferred_element_type=F32) + carry_ref[...]
    rank1 = jnp.sum(jnp.where(hit1, prefix, 0.0), axis=-1, keepdims=True).astype(jnp.int32)
    rank2 = jnp.sum(jnp.where(hit2, prefix, 0.0), axis=-1, keepdims=True).astype(jnp.int32)
    carry = carry_ref[...] + jnp.sum(onehot, axis=0, keepdims=True)
    carry_ref[...] = carry
    cnt_ref[...] = carry
    ri_ref[...] = jnp.where(lane == 0, idx1 - N_GROUPS, jnp.where(lane == 1, idx2 - N_GROUPS,
                            jnp.where(lane == 2, rank1, jnp.where(lane == 3, rank2, 0))))
    rf_ref[...] = jnp.where(lane == 0, g1, jnp.where(lane == 1, g2, 0.0))


def _route(logits):
    n = logits.shape[0]
    tm = min(TOKEN_TILE, n)
    return pl.pallas_call(
        _route_kernel,
        grid=(n // tm,),
        in_specs=[_rows(tm, LANES)],
        out_specs=[_rows(tm, LANES), _rows(tm, LANES), _full((1, LANES))],
        out_shape=[jax.ShapeDtypeStruct((n, LANES), jnp.int32), jax.ShapeDtypeStruct((n, LANES), F32),
                   jax.ShapeDtypeStruct((1, LANES), F32)],
        scratch_shapes=[pltpu.VMEM((1, LANES), F32)],
        compiler_params=_cparams("arbitrary"),
        name="route",
    )(logits)


ROW_UNROLL = 8


def _row_copy(src, i, dst, j, sem):
    return pltpu.make_async_copy(src.at[pl.ds(i, 1), :], dst.at[pl.ds(j, 1), :], sem)


def _dispatch_kernel(dest_hbm, x_ref, buf_in, buf_hbm, dest_smem, sem_idx, sem_rows):
    del buf_in
    i = pl.program_id(0)
    tm = x_ref.shape[0]
    idx_copy = pltpu.make_async_copy(dest_hbm.at[i], dest_smem, sem_idx)
    idx_copy.start()
    idx_copy.wait()

    def body(o, carry):
        for u in range(ROW_UNROLL):
            r = o * ROW_UNROLL + u
            for k in range(TOP_K):
                _row_copy(x_ref, r, buf_hbm, dest_smem[TOP_K * r + k], sem_rows).start()
        return carry

    lax.fori_loop(0, tm // ROW_UNROLL, body, 0)
    for k in range(TOP_K):
        pltpu.make_async_copy(x_ref, buf_hbm.at[pl.ds(0, tm), :], sem_rows).wait()


def _dispatch(xn, dest, n_rows):
    n, d = xn.shape
    tm = min(TOKEN_TILE, n)
    buf = jnp.zeros((n_rows, d), xn.dtype)
    return pl.pallas_call(
        _dispatch_kernel,
        grid=(n // tm,),
        in_specs=[pl.BlockSpec(memory_space=pl.ANY), _rows(tm, d), pl.BlockSpec(memory_space=pl.ANY)],
        out_specs=pl.BlockSpec(memory_space=pl.ANY),
        out_shape=jax.ShapeDtypeStruct((n_rows, d), xn.dtype),
        scratch_shapes=[pltpu.SMEM((TOP_K * tm,), jnp.int32), pltpu.SemaphoreType.DMA, pltpu.SemaphoreType.DMA],
        input_output_aliases={2: 0},
        compiler_params=pltpu.CompilerParams(dimension_semantics=("arbitrary",), vmem_limit_bytes=VMEM_LIMIT,
                                             has_side_effects=True),
        name="dispatch",
    )(dest.reshape(n // tm, TOP_K * tm), xn, buf)


def _combine_kernel(dest_hbm, y_hbm, x_ref, rf_ref, o_ref, dest_smem, y0_ref, y1_ref, sem_idx, sem_rows):
    i = pl.program_id(0)
    tm = x_ref.shape[0]
    idx_copy = pltpu.make_async_copy(dest_hbm.at[i], dest_smem, sem_idx)
    idx_copy.start()
    idx_copy.wait()
    ybufs = (y0_ref, y1_ref)

    def body(o, carry):
        for u in range(ROW_UNROLL):
            r = o * ROW_UNROLL + u
            for k in range(TOP_K):
                _row_copy(y_hbm, dest_smem[TOP_K * r + k], ybufs[k], r, sem_rows).start()
        return carry

    lax.fori_loop(0, tm // ROW_UNROLL, body, 0)
    for k in range(TOP_K):
        pltpu.make_async_copy(y_hbm.at[pl.ds(0, tm), :], ybufs[k], sem_rows).wait()
    rf = rf_ref[...]
    o_ref[...] = x_ref[...] + (y0_ref[...] * rf[:, 0:1] + y1_ref[...] * rf[:, 1:2])


def _combine(x1, y, dest, rf):
    n, d = x1.shape
    tm = min(TOKEN_TILE, n)
    return pl.pallas_call(
        _combine_kernel,
        grid=(n // tm,),
        in_specs=[pl.BlockSpec(memory_space=pl.ANY), pl.BlockSpec(memory_space=pl.ANY), _rows(tm, d),
                  _rows(tm, LANES)],
        out_specs=_rows(tm, d),
        out_shape=jax.ShapeDtypeStruct((n, d), x1.dtype),
        scratch_shapes=[pltpu.SMEM((TOP_K * tm,), jnp.int32), pltpu.VMEM((tm, d), y.dtype),
                        pltpu.VMEM((tm, d), y.dtype), pltpu.SemaphoreType.DMA, pltpu.SemaphoreType.DMA],
        compiler_params=_cparams("arbitrary"),
        name="combine",
    )(dest.reshape(n // tm, TOP_K * tm), y, x1, rf)


def _moe(x1, xn, logits, wg, wu, wd):
    n, d = x1.shape
    bm = EXPERT_BLOCK
    ri, rf, cnt = _route(logits)
    counts = cnt[0, N_GROUPS:N_GROUPS + N_EXPERTS].astype(jnp.int32)
    psz = (counts + bm - 1) // bm * bm
    pend = jnp.cumsum(psz)
    pstart = pend - psz
    dest = pstart[ri[:, :TOP_K]] + ri[:, TOP_K:2 * TOP_K]
    n_blocks = -(-(n * TOP_K) // bm) + N_EXPERTS
    blk_start = jnp.arange(n_blocks, dtype=jnp.int32) * bm
    blk_e = jnp.minimum(jnp.sum(pend[None, :] <= blk_start[:, None], axis=1), N_EXPERTS - 1).astype(jnp.int32)
    n_used = (pend[-1] // bm).astype(jnp.int32).reshape(1)
    buf = _dispatch(xn, dest, n_blocks * bm)
    y = _experts(buf, blk_e, n_used, wg, wu, wd)
    return _combine(x1, y, dest, rf)


def _prep_layer(l, P):
    d = P['w_in'].shape[1]
    cw = P['conv_dw_w'].shape[-1]
    mw = P['ml_norm_g'].shape[-1]
    nw = NA_HEADS * P['na_q_norm_g'].shape[-1]
    off_g = 2 * cw + 4 * mw
    ng = 4 * ML_HEADS
    w = P['w_in'][l]
    w_perm = jnp.concatenate([w[:, :off_g], w[:, off_g + ng:], w[:, off_g:off_g + ng],
                              jnp.zeros((d, LANES - ng), F32)], axis=1).astype(BF16)
    nhd = nw // NA_HEADS
    head_id = jnp.arange(nw) // nhd
    bd = jnp.where(head_id[:, None] == head_id[None, :], 1.0 / nhd, 0.0).astype(BF16)
    wr = jnp.concatenate([P['w_router_group'][l], P['w_router_expert'][l],
                          jnp.zeros((d, LANES - N_GROUPS - N_EXPERTS), F32)], axis=1)
    br = jnp.concatenate([P['b_router_group'][l], P['b_router_expert'][l],
                          jnp.zeros((LANES - N_GROUPS - N_EXPERTS,), F32)])[None, :]
    return dict(
        dims=(cw, mw, nw),
        norm_mix_g=P['norm_mix_g'][l][None, :], w_in=w_perm,
        gate_b=jnp.pad(P['ml_gate_b'][l].reshape(-1), (0, LANES - ng))[None, :],
        nq=jnp.tile(P['na_q_norm_g'][l], NA_HEADS)[None, :], nk=jnp.tile(P['na_k_norm_g'][l], NA_HEADS)[None, :],
        bd=bd,
        conv_dw_w=P['conv_dw_w'][l], conv_dw_b=P['conv_dw_b'][l][None, :], conv_norm_g=P['conv_norm_g'][l][None, :],
        ml_qk_conv_w=P['ml_qk_conv_w'][l], ml_qk_conv_b=P['ml_qk_conv_b'][l][None, :],
        ml_norm_g=P['ml_norm_g'][l][None, :], na_table=_na_bias_table(P['na_rpb'][l]),
        w_out=P['w_out'][l].astype(BF16), norm_ffn_g=P['norm_ffn_g'][l][None, :], wr=wr, br=br,
        wg=P['w_exp_gate'][l].astype(BF16), wu=P['w_exp_up'][l].astype(BF16), wd=P['w_exp_down'][l].astype(BF16),
    )


def _layer(x, B, T, p):
    dims = p['dims']
    u, qkpre, v, o, naq, nak, nav, gates = _in_proj(x, p['norm_mix_g'], p['w_in'], p['gate_b'], p['nq'], p['nk'],
                                                    p['bd'], dims)
    yc, hf, hb, yna = _mixers(B, T, u, qkpre, v, naq, nak, nav, gates, p)
    x1, xn, logits = _out_proj(yc, hf, hb, o, yna, x, p['ml_norm_g'], p['w_out'], p['norm_ffn_g'], p['wr'], p['br'],
                               dims)
    return _moe(x1, xn, logits, p['wg'], p['wu'], p['wd'])


def kernel(x_prompt, x_sample, norm_mix_g, w_in, conv_dw_w, conv_dw_b, conv_norm_g, ml_qk_conv_w, ml_qk_conv_b,
           ml_gate_b, ml_norm_g, na_q_norm_g, na_k_norm_g, na_rpb, w_out, norm_ffn_g, w_router_group,
           b_router_group, w_router_expert, b_router_expert, w_exp_gate, w_exp_up, w_exp_down):
    P = dict(norm_mix_g=norm_mix_g, w_in=w_in, conv_dw_w=conv_dw_w, conv_dw_b=conv_dw_b, conv_norm_g=conv_norm_g,
             ml_qk_conv_w=ml_qk_conv_w, ml_qk_conv_b=ml_qk_conv_b, ml_gate_b=ml_gate_b, ml_norm_g=ml_norm_g,
             na_q_norm_g=na_q_norm_g, na_k_norm_g=na_k_norm_g, na_rpb=na_rpb, w_out=w_out, norm_ffn_g=norm_ffn_g,
             w_router_group=w_router_group, b_router_group=b_router_group, w_router_expert=w_router_expert,
             b_router_expert=b_router_expert, w_exp_gate=w_exp_gate, w_exp_up=w_exp_up, w_exp_down=w_exp_down)
    depth = w_in.shape[0]
    d = x_prompt.shape[-1]
    groups = [(x_prompt.shape[0], x_prompt.shape[1], x_prompt.reshape(-1, d)),
              (x_sample.shape[0], x_sample.shape[1], x_sample.reshape(-1, d))]
    for l in range(depth):
        p = _prep_layer(l, P)
        groups = [(B, T, _layer(x, B, T, p)) for B, T, x in groups]
    return tuple(x.reshape(B, T, d) for B, T, x in groups)
```

```python
import functools

import jax
import jax.numpy as jnp
from jax import lax
from jax.experimental import pallas as pl
from jax.experimental.pallas import tpu as pltpu

F32 = jnp.float32
BF16 = jnp.bfloat16

ML_HEADS = 4
NA_HEADS = 4
CONV_K = 31
ML_CHUNK = 128
GRID_W = 64
NA_WIN_ROWS = 8
NA_WIN_COLS = 16
N_GROUPS = 8
EXPERTS_PER_GROUP = 8
N_EXPERTS = N_GROUPS * EXPERTS_PER_GROUP
TOP_K = 2
EPS = 1e-6

LANES = 128
SUBLANES = 8
VMEM_LIMIT = 48 * 1024 * 1024

TOKEN_TILE = 512
EXPERT_BLOCK = 256


def _cparams(*sem):
    return pltpu.CompilerParams(dimension_semantics=sem, vmem_limit_bytes=VMEM_LIMIT)


def _full(shape):
    return pl.BlockSpec(shape, lambda *_: (0,) * len(shape))


def _rows(tm, width):
    return pl.BlockSpec((tm, width), lambda i: (i, 0))


def _group_mean(zz, bd):
    hi = zz.astype(BF16)
    lo = (zz - hi.astype(F32)).astype(BF16)
    return (jnp.dot(hi, bd, preferred_element_type=F32) + jnp.dot(lo, bd, preferred_element_type=F32))


def _in_proj_kernel(x_ref, g_ref, w_ref, gb_ref, nq_ref, nk_ref, bd_ref,
                    u_ref, qk_ref, v_ref, o_ref, naq_ref, nak_ref, nav_ref, gt_ref, *, dims):
    cw, mw, nw = dims
    x = x_ref[...]
    ms = jnp.mean(x * x, axis=-1, keepdims=True)
    xn = (x * lax.rsqrt(ms + EPS) * g_ref[...]).astype(BF16)

    def proj(lo, width):
        return jnp.dot(xn, w_ref[:, lo:lo + width], preferred_element_type=F32)

    zc = proj(0, 2 * cw)
    u_ref[...] = zc[:, :cw] * jax.nn.sigmoid(zc[:, cw:])
    off = 2 * cw
    qk_ref[...] = proj(off, 2 * mw)
    v_ref[...] = proj(off + 2 * mw, mw).astype(BF16)
    o_ref[...] = proj(off + 3 * mw, mw)
    off += 4 * mw
    bd = bd_ref[...]
    zq = proj(off, nw)
    zk = proj(off + nw, nw)
    hd = nw // NA_HEADS
    naq_ref[...] = (zq * lax.rsqrt(_group_mean(zq * zq, bd) + EPS) * nq_ref[...] * (hd ** -0.5)).astype(BF16)
    nak_ref[...] = (zk * lax.rsqrt(_group_mean(zk * zk, bd) + EPS) * nk_ref[...]).astype(BF16)
    nav_ref[...] = proj(off + 2 * nw, nw).astype(BF16)
    off += 3 * nw
    gt_ref[...] = proj(off, LANES) + gb_ref[...]


def _in_proj(x, g, w, gate_b, nq, nk, bd, dims):
    n, d = x.shape
    cw, mw, nw = dims
    tm = min(TOKEN_TILE, n)
    outs = [(cw, F32), (2 * mw, F32), (mw, BF16), (mw, F32), (nw, BF16), (nw, BF16), (nw, BF16), (LANES, F32)]
    return pl.pallas_call(
        functools.partial(_in_proj_kernel, dims=dims),
        grid=(n // tm,),
        in_specs=[_rows(tm, d), _full(g.shape), _full(w.shape), _full(gate_b.shape),
                  _full(nq.shape), _full(nk.shape), _full(bd.shape)],
        out_specs=[_rows(tm, wd) for wd, _ in outs],
        out_shape=[jax.ShapeDtypeStruct((n, wd), dt) for wd, dt in outs],
        compiler_params=_cparams("parallel"),
        name="in_proj",
    )(x, g, w, gate_b, nq, nk, bd)


OUT_SUB = 128


def _out_proj_kernel(yc_ref, hf_ref, hb_ref, o_ref, yna_ref, x_ref, mg_ref, w_ref, fg_ref, wrh_ref, wrl_ref, br_ref,
                     x1_ref, xn_ref, lg_ref, *, dims):
    cw, mw, nw = dims
    hd = mw // ML_HEADS
    for r0 in range(0, x_ref.shape[0], OUT_SUB):
        rs = slice(r0, r0 + OUT_SUB)
        hm = hf_ref[rs, :] + hb_ref[rs, :]
        parts = []
        for h in range(ML_HEADS):
            seg = hm[:, h * hd:(h + 1) * hd]
            parts.append(seg * lax.rsqrt(jnp.mean(seg * seg, axis=-1, keepdims=True) + EPS))
        yml = jnp.concatenate(parts, axis=-1) * mg_ref[...] * jax.nn.sigmoid(o_ref[rs, :])
        y = jnp.dot(yc_ref[rs, :], w_ref[0:cw, :], preferred_element_type=F32)
        y += jnp.dot(yml.astype(BF16), w_ref[cw:cw + mw, :], preferred_element_type=F32)
        y += jnp.dot(yna_ref[rs, :], w_ref[cw + mw:cw + mw + nw, :], preferred_element_type=F32)
        x1 = x_ref[rs, :] + y
        x1_ref[rs, :] = x1
        xn = x1 * lax.rsqrt(jnp.mean(x1 * x1, axis=-1, keepdims=True) + EPS) * fg_ref[...]
        xn_ref[rs, :] = xn
        xh = xn.astype(BF16)
        xl = (xn - xh.astype(F32)).astype(BF16)
        lg = jnp.dot(xh, wrh_ref[...], preferred_element_type=F32)
        lg += jnp.dot(xl, wrh_ref[...], preferred_element_type=F32)
        lg += jnp.dot(xh, wrl_ref[...], preferred_element_type=F32)
        lg_ref[rs, :] = lg + br_ref[...]


def _out_proj(yc, hf, hb, o, yna, x, mg, w, fg, wrh, wrl, br, dims):
    n, d = x.shape
    cw, mw, nw = dims
    tm = min(TOKEN_TILE, n)
    assert tm % OUT_SUB == 0
    return pl.pallas_call(
        functools.partial(_out_proj_kernel, dims=dims),
        grid=(n // tm,),
        in_specs=[_rows(tm, cw), _rows(tm, mw), _rows(tm, mw), _rows(tm, mw), _rows(tm, nw), _rows(tm, d),
                  _full(mg.shape), _full(w.shape), _full(fg.shape), _full(wrh.shape), _full(wrl.shape),
                  _full(br.shape)],
        out_specs=[_rows(tm, d), _rows(tm, d), _rows(tm, LANES)],
        out_shape=[jax.ShapeDtypeStruct((n, d), F32), jax.ShapeDtypeStruct((n, d), F32),
                   jax.ShapeDtypeStruct((n, LANES), F32)],
        compiler_params=_cparams("parallel"),
        name="out_proj",
    )(yc, hf, hb, o, yna, x, mg, w, fg, wrh, wrl, br)


def _expert_kernel(be_ref, nb_ref, x_ref, wg_ref, wu_ref, wd_ref, y_ref):
    i = pl.program_id(0)

    @pl.when(i < nb_ref[0])
    def _():
        xb = x_ref[...].astype(BF16)
        g = jnp.dot(xb, wg_ref[0], preferred_element_type=F32)
        u = jnp.dot(xb, wu_ref[0], preferred_element_type=F32)
        h = (g * jax.nn.sigmoid(g) * u).astype(BF16)
        y_ref[...] = jnp.dot(h, wd_ref[0], preferred_element_type=F32)

    @pl.when(i >= nb_ref[0])
    def _():
        y_ref[...] = jnp.zeros_like(y_ref)


def _experts(buf, blk_e, n_used, wg, wu, wd):
    p, d = buf.shape
    de = wg.shape[-1]
    bm = EXPERT_BLOCK
    grid_spec = pltpu.PrefetchScalarGridSpec(
        num_scalar_prefetch=2,
        grid=(p // bm,),
        in_specs=[pl.BlockSpec((bm, d), lambda i, be, nb: (jnp.minimum(i, nb[0] - 1), 0)),
                  pl.BlockSpec((1, d, de), lambda i, be, nb: (be[i], 0, 0)),
                  pl.BlockSpec((1, d, de), lambda i, be, nb: (be[i], 0, 0)),
                  pl.BlockSpec((1, de, d), lambda i, be, nb: (be[i], 0, 0))],
        out_specs=pl.BlockSpec((bm, d), lambda i, be, nb: (i, 0)),
    )
    return pl.pallas_call(
        _expert_kernel,
        grid_spec=grid_spec,
        out_shape=jax.ShapeDtypeStruct((p, d), F32),
        compiler_params=_cparams("arbitrary"),
        name="experts",
    )(blk_e, n_used, buf, wg, wu, wd)


NA_QROWS = 4
NEG_BIG = -1e30


def _na_bias_table(rpb):
    a = jnp.arange(NA_QROWS)[:, None]
    i = jnp.arange(3 * NA_QROWS)[None, :]
    half = NA_WIN_ROWS // 2
    a0 = 0 * a
    row_ok = jnp.stack([(i >= a0 + NA_QROWS) & (i < a0 + NA_QROWS + NA_WIN_ROWS),
                        (i >= a + NA_QROWS - half) & (i < a + NA_QROWS - half + NA_WIN_ROWS),
                        (i >= a0 + 2 * NA_QROWS - NA_WIN_ROWS) & (i < a0 + 2 * NA_QROWS)])
    ro = i - NA_QROWS - a + (NA_WIN_ROWS - 1)
    c = jnp.arange(GRID_W)
    cs = jnp.clip(c - NA_WIN_COLS // 2, 0, GRID_W - NA_WIN_COLS)
    col_ok = (c[None, :] >= cs[:, None]) & (c[None, :] < cs[:, None] + NA_WIN_COLS)
    co = jnp.clip(c[None, :] - c[:, None] + (NA_WIN_COLS - 1), 0, 2 * NA_WIN_COLS - 2)
    bias = rpb.astype(F32)[:, jnp.clip(ro, 0, 2 * NA_WIN_ROWS - 2)[:, None, :, None], co[None, :, None, :]]
    ok = row_ok[:, None, :, None, :, None] & col_ok[None, None, None, :, None, :]
    tab = jnp.where(ok, bias[None], NEG_BIG)
    h = rpb.shape[0]
    return tab.reshape(3, h, NA_QROWS * GRID_W, 3 * NA_QROWS * GRID_W)


def _na_kernel(q_ref, kp_ref, kc_ref, kn_ref, vp_ref, vc_ref, vn_ref, b_ref, o_ref):
    q = q_ref[...]
    kcat = jnp.concatenate([kp_ref[...], kc_ref[...], kn_ref[...]], axis=0)
    vcat = jnp.concatenate([vp_ref[...], vc_ref[...], vn_ref[...]], axis=0)
    nw = q.shape[-1]
    hd = nw // NA_HEADS
    lane_head = lax.broadcasted_iota(jnp.int32, q.shape, 1) // hd
    acc = jnp.zeros(q.shape, F32)
    for h in range(NA_HEADS):
        sel = lane_head == h
        qm = jnp.where(sel, q, jnp.zeros_like(q))
        s = lax.dot_general(qm, kcat, (((1,), (1,)), ((), ())), preferred_element_type=F32) + b_ref[0, h]
        m = jnp.max(s, axis=-1, keepdims=True)
        p = jnp.exp(s - m)
        l = jnp.sum(p, axis=-1, keepdims=True)
        oh = jnp.dot(p.astype(BF16), vcat, preferred_element_type=F32)
        acc = jnp.where(sel, oh / l, acc)
    o_ref[...] = acc.astype(o_ref.dtype)


def _neigh_attn_pallas(q, k, v, table, B, T):
    n, nw = q.shape
    tq = NA_QROWS * GRID_W
    J = T // tq
    assert T % tq == 0 and J >= 2 and T // GRID_W >= NA_WIN_ROWS
    blk = (tq, nw)
    cur = pl.BlockSpec(blk, lambda b, j: (b * J + j, 0))
    prev = pl.BlockSpec(blk, lambda b, j: (b * J + jnp.maximum(j - 1, 0), 0))
    nxt = pl.BlockSpec(blk, lambda b, j: (b * J + jnp.minimum(j + 1, J - 1), 0))
    tab = pl.BlockSpec((1,) + table.shape[1:],
                       lambda b, j: (jnp.where(j == 0, 0, jnp.where(j == J - 1, 2, 1)), 0, 0, 0))
    return pl.pallas_call(
        _na_kernel,
        grid=(B, J),
        in_specs=[cur, prev, cur, nxt, prev, cur, nxt, tab],
        out_specs=cur,
        out_shape=jax.ShapeDtypeStruct((n, nw), BF16),
        compiler_params=_cparams("parallel", "arbitrary"),
        name="neigh_attn",
    )(q, k, k, k, v, v, v, table)


HALO = 16
CONV_ROWS = 64


def _local_conv_kernel(u_ref, up_ref, un_ref, qk_ref, qkp_ref, qkn_ref, cw_ref, cb_ref, cg_ref, mw_ref, mb_ref,
                       yc_ref, q_ref, k_ref, wu_ref, wqk_ref, sh_ref, *, k_scale):
    t = pl.program_id(1)
    nt = pl.num_programs(1)
    tm = u_ref.shape[0]
    keep_p = (t > 0).astype(F32)
    keep_n = (t < nt - 1).astype(F32)
    wu_ref[0:HALO, :] = up_ref[...] * keep_p
    wu_ref[HALO:HALO + tm, :] = u_ref[...]
    wu_ref[HALO + tm:, :] = un_ref[...] * keep_n
    wqk_ref[0:HALO, :] = qkp_ref[...] * keep_p
    wqk_ref[HALO:HALO + tm, :] = qk_ref[...]
    wqk_ref[HALO + tm:, :] = qkn_ref[...] * keep_n
    kc = cw_ref.shape[0]
    km = mw_ref.shape[0]
    mw2 = q_ref.shape[-1]
    nsh = sh_ref.shape[1]
    for ph in range(1, SUBLANES):
        sh_ref[ph - 1] = wu_ref[ph:ph + nsh, :]
    for r0 in range(0, tm, CONV_ROWS):
        acc = jnp.zeros((CONV_ROWS, u_ref.shape[-1]), F32) + cb_ref[...]
        for k in range(kc):
            s = r0 + HALO - kc // 2 + k
            ph = s % SUBLANES
            tap = wu_ref[s:s + CONV_ROWS, :] if ph == 0 else sh_ref[ph - 1, s - ph:s - ph + CONV_ROWS, :]
            acc = acc + tap * cw_ref[k:k + 1, :]
        y = acc * lax.rsqrt(jnp.mean(acc * acc, axis=-1, keepdims=True) + EPS) * cg_ref[...]
        yc_ref[r0:r0 + CONV_ROWS, :] = (y * jax.nn.sigmoid(y)).astype(yc_ref.dtype)
        acc = jnp.zeros((CONV_ROWS, qk_ref.shape[-1]), F32) + mb_ref[...]
        for k in range(km):
            s = r0 + HALO - km // 2 + k
            acc = acc + wqk_ref[s:s + CONV_ROWS, :] * mw_ref[k:k + 1, :]
        y = acc * jax.nn.sigmoid(acc)
        q_ref[r0:r0 + CONV_ROWS, :] = y[:, :mw2].astype(q_ref.dtype)
        k_ref[r0:r0 + CONV_ROWS, :] = (y[:, mw2:] * k_scale).astype(k_ref.dtype)


def _local_conv(u, qkpre, cw, cb, cg, mw, mb, B, T):
    n, cwid = u.shape
    qw = qkpre.shape[-1]
    tm = min(TOKEN_TILE, T)
    nt = T // tm
    hb = tm // HALO
    nhb = n // HALO
    assert T % tm == 0 and tm % CONV_ROWS == 0 and CONV_K // 2 <= HALO

    def cur(w):
        return pl.BlockSpec((tm, w), lambda b, t: (b * nt + t, 0))

    def prev(w):
        return pl.BlockSpec((HALO, w), lambda b, t: (jnp.maximum((b * nt + t) * hb - 1, 0), 0))

    def nxt(w):
        return pl.BlockSpec((HALO, w), lambda b, t: (jnp.minimum((b * nt + t + 1) * hb, nhb - 1), 0))

    hd = (qw // 2) // ML_HEADS
    return pl.pallas_call(
        functools.partial(_local_conv_kernel, k_scale=hd ** -0.5),
        grid=(B, nt),
        in_specs=[cur(cwid), prev(cwid), nxt(cwid), cur(qw), prev(qw), nxt(qw),
                  _full(cw.shape), _full(cb.shape), _full(cg.shape), _full(mw.shape), _full(mb.shape)],
        out_specs=[cur(cwid), cur(qw // 2), cur(qw // 2)],
        out_shape=[jax.ShapeDtypeStruct((n, cwid), BF16), jax.ShapeDtypeStruct((n, qw // 2), BF16),
                   jax.ShapeDtypeStruct((n, qw // 2), BF16)],
        scratch_shapes=[pltpu.VMEM((tm + 2 * HALO, cwid), F32), pltpu.VMEM((tm + 2 * HALO, qw), F32),
                        pltpu.VMEM((SUBLANES - 1, tm + 2 * HALO - SUBLANES, cwid), F32)],
        compiler_params=_cparams("parallel", "parallel"),
        name="local_conv",
    )(u, u, u, qkpre, qkpre, qkpre, cw, cb, cg, mw, mb)


def _scan_rows(x, reverse, combine, fill):
    L = x.shape[0]
    row = lax.broadcasted_iota(jnp.int32, x.shape, 0)
    s = 1
    while s < L:
        if reverse:
            x = combine(x, jnp.where(row < L - s, pltpu.roll(x, L - s, axis=0), fill))
        else:
            x = combine(x, jnp.where(row >= s, pltpu.roll(x, s, axis=0), fill))
        s *= 2
    return x


def _mlstm_direction(q_ref, k_ref, v_ref, g_ref, h_ref, c_ref, m_ref, reverse):
    L = q_ref.shape[0]
    hd = q_ref.shape[-1] // ML_HEADS
    ci = 2 * ML_HEADS if reverse else 0
    cf = ci + ML_HEADS
    g = g_ref[...]
    gi = pltpu.roll(g, ML_HEADS, axis=1)
    lf = jnp.minimum(g, 0.0) - jnp.log1p(jnp.exp(-jnp.abs(g)))
    b = _scan_rows(lf, reverse, jnp.add, 0.0)
    r = gi - b
    cm = _scan_rows(r, reverse, jnp.maximum, NEG_BIG)
    total = b[0:1, :] if reverse else b[L - 1:L, :]
    m_prev = m_ref[...]
    inter = b + m_prev
    m_j = jnp.maximum(inter, b + cm)
    a_t = b - m_j
    e_t = jnp.exp(inter - m_j)
    z_t = jnp.exp(-m_j)
    w = total - b + gi
    m_loc = jnp.max(w, axis=0, keepdims=True)
    wa_t = jnp.exp(w - m_loc)
    m_new = jnp.maximum(total + m_prev, m_loc)
    dec = jnp.exp(total + m_prev - m_new)
    inc = jnp.exp(m_loc - m_new)
    m_ref[...] = m_new
    rt = jnp.transpose(r)
    jj = lax.broadcasted_iota(jnp.int32, (L, L), 0)
    ss = lax.broadcasted_iota(jnp.int32, (L, L), 1)
    visible = (ss >= jj) if reverse else (ss <= jj)
    ones = jnp.ones((L, hd), BF16)
    for h in range(ML_HEADS):
        sl = slice(h * hd, (h + 1) * hd)
        col = slice(cf + h, cf + h + 1)
        qh = q_ref[:, sl]
        kh = k_ref[:, sl]
        v_ext = jnp.concatenate([v_ref[:, sl], ones], axis=-1)
        c_prev = c_ref[h]
        pm = jnp.exp(jnp.where(visible, a_t[:, col] + rt[col, :], NEG_BIG))
        qk = lax.dot_general(qh, kh, (((1,), (1,)), ((), ())), preferred_element_type=F32) * pm
        tot = (jnp.dot(qk.astype(BF16), v_ext, preferred_element_type=F32)
               + e_t[:, col] * jnp.dot(qh, c_prev.astype(BF16), preferred_element_type=F32))
        h_ref[:, sl] = tot[:, :hd] / jnp.maximum(jnp.abs(tot[:, hd:]), z_t[:, col])
        ka = (kh.astype(F32) * wa_t[:, col]).astype(BF16)
        s_ext = lax.dot_general(ka, v_ext, (((0,), (0,)), ((), ())), preferred_element_type=F32)
        c_ref[h] = dec[:, col] * c_prev + inc[:, col] * s_ext


def _mlstm_kernel(qf_ref, kf_ref, vf_ref, gf_ref, qb_ref, kb_ref, vb_ref, gb_ref, hf_ref, hb_ref,
                  cf_ref, cb_ref, mf_ref, mb_ref):
    @pl.when(pl.program_id(1) == 0)
    def _():
        cf_ref[...] = jnp.zeros_like(cf_ref)
        cb_ref[...] = jnp.zeros_like(cb_ref)
        mf_ref[...] = jnp.zeros_like(mf_ref)
        mb_ref[...] = jnp.zeros_like(mb_ref)

    _mlstm_direction(qf_ref, kf_ref, vf_ref, gf_ref, hf_ref, cf_ref, mf_ref, False)
    _mlstm_direction(qb_ref, kb_ref, vb_ref, gb_ref, hb_ref, cb_ref, mb_ref, True)


def _mlstm(q, k, v, gates, B, T):
    n, mw = q.shape
    L = ML_CHUNK
    nc = T // L
    hd = mw // ML_HEADS
    assert T % L == 0 and hd == LANES

    def fwd(w):
        return pl.BlockSpec((L, w), lambda b, c: (b * nc + c, 0))

    def bwd(w):
        return pl.BlockSpec((L, w), lambda b, c: (b * nc + nc - 1 - c, 0))

    return pl.pallas_call(
        _mlstm_kernel,
        grid=(B, nc),
        in_specs=[fwd(mw), fwd(mw), fwd(mw), fwd(LANES), bwd(mw), bwd(mw), bwd(mw), bwd(LANES)],
        out_specs=[fwd(mw), bwd(mw)],
        out_shape=[jax.ShapeDtypeStruct((n, mw), F32), jax.ShapeDtypeStruct((n, mw), F32)],
        scratch_shapes=[pltpu.VMEM((ML_HEADS, hd, 2 * hd), F32), pltpu.VMEM((ML_HEADS, hd, 2 * hd), F32),
                        pltpu.VMEM((1, LANES), F32), pltpu.VMEM((1, LANES), F32)],
        compiler_params=_cparams("parallel", "arbitrary"),
        name="mlstm",
    )(q, k, v, gates, q, k, v, gates)


def _mixers(B, T, u, qkpre, v, naq, nak, nav, gates, p):
    yc, q, k = _local_conv(u, qkpre, p['conv_dw_w'], p['conv_dw_b'], p['conv_norm_g'],
                           p['ml_qk_conv_w'], p['ml_qk_conv_b'], B, T)
    hf, hb = _mlstm(q, k, v, gates, B, T)
    y_na = _neigh_attn_pallas(naq, nak, nav, p['na_table'], B, T)
    return yc, hf, hb, y_na


def _first_lane(mask, lane):
    return jnp.min(jnp.where(mask, lane, LANES), axis=-1, keepdims=True)


def _route_kernel(lg_ref, ri_ref, rf_ref, cnt_ref, carry_ref):
    @pl.when(pl.program_id(0) == 0)
    def _():
        carry_ref[...] = jnp.zeros_like(carry_ref)

    lg = lg_ref[...]
    tm = lg.shape[0]
    lane = lax.broadcasted_iota(jnp.int32, lg.shape, 1)
    gl = jnp.where(lane < N_GROUPS, lg, NEG_BIG)
    gmax = jnp.max(gl, axis=-1, keepdims=True)
    pg = 1.0 / jnp.sum(jnp.exp(gl - gmax), axis=-1, keepdims=True)
    gsel = _first_lane(gl == gmax, lane)
    lo = N_GROUPS + gsel * EXPERTS_PER_GROUP
    in_grp = (lane >= lo) & (lane < lo + EXPERTS_PER_GROUP)
    el = jnp.where(in_grp, lg, NEG_BIG)
    emax = jnp.max(el, axis=-1, keepdims=True)
    esum = jnp.sum(jnp.exp(el - emax), axis=-1, keepdims=True)
    idx1 = _first_lane(in_grp & (el == emax), lane)
    el2 = jnp.where(lane == idx1, NEG_BIG, el)
    emax2 = jnp.max(el2, axis=-1, keepdims=True)
    idx2 = _first_lane(in_grp & (el2 == emax2) & (lane != idx1), lane)
    p1 = 1.0 / esum
    p2 = jnp.exp(emax2 - emax) / esum
    g1 = pg * p1 / (p1 + p2)
    g2 = pg * p2 / (p1 + p2)
    hit1 = lane == idx1
    hit2 = lane == idx2
    onehot = jnp.where(hit1 | hit2, 1.0, 0.0)
    r = lax.broadcasted_iota(jnp.int32, (tm, tm), 0)
    c = lax.broadcasted_iota(jnp.int32, (tm, tm), 1)
    tri = jnp.where(r > c, 1.0, 0.0).astype(BF16)
    prefix = jnp.dot(tri, onehot.astype(BF16), preferred_element_type=F32) + carry_ref[...]
    rank1 = jnp.sum(jnp.where(hit1, prefix, 0.0), axis=-1, keepdims=True).astype(jnp.int32)
    rank2 = jnp.sum(jnp.where(hit2, prefix, 0.0), axis=-1, keepdims=True).astype(jnp.int32)
    carry = carry_ref[...] + jnp.sum(onehot, axis=0, keepdims=True)
    carry_ref[...] = carry
    cnt_ref[...] = carry
    ri_ref[...] = jnp.where(lane == 0, idx1 - N_GROUPS, jnp.where(lane == 1, idx2 - N_GROUPS,
                            jnp.where(lane == 2, rank1, jnp.where(lane == 3, rank2, 0))))
    rf_ref[...] = jnp.where(lane == 0, g1, jnp.where(lane == 1, g2, 0.0))


def _dest_kernel(ri_ref, ps_ref, d_ref):
    ri = ri_ref[...]
    lane = lax.broadcasted_iota(jnp.int32, ri.shape, 1)
    ps = ps_ref[...]
    out = jnp.zeros(ri.shape, jnp.int32)
    for k in range(TOP_K):
        start = jnp.sum(jnp.where(lane == ri[:, k:k + 1], ps, 0.0), axis=-1, keepdims=True).astype(jnp.int32)
        out = jnp.where(lane == k, start + ri[:, TOP_K + k:TOP_K + k + 1], out)
    d_ref[...] = out


def _dest(ri, pstart_row):
    n = ri.shape[0]
    tm = min(TOKEN_TILE, n)
    return pl.pallas_call(
        _dest_kernel,
        grid=(n // tm,),
        in_specs=[_rows(tm, LANES), _full((1, LANES))],
        out_specs=_rows(tm, LANES),
        out_shape=jax.ShapeDtypeStruct((n, LANES), jnp.int32),
        compiler_params=_cparams("parallel"),
        name="dest",
    )(ri, pstart_row)


def _route(logits):
    n = logits.shape[0]
    tm = min(TOKEN_TILE, n)
    return pl.pallas_call(
        _route_kernel,
        grid=(n // tm,),
        in_specs=[_rows(tm, LANES)],
        out_specs=[_rows(tm, LANES), _rows(tm, LANES), _full((1, LANES))],
        out_shape=[jax.ShapeDtypeStruct((n, LANES), jnp.int32), jax.ShapeDtypeStruct((n, LANES), F32),
                   jax.ShapeDtypeStruct((1, LANES), F32)],
        scratch_shapes=[pltpu.VMEM((1, LANES), F32)],
        compiler_params=_cparams("arbitrary"),
        name="route",
    )(logits)


ROW_UNROLL = 8


def _row_copy(src, i, dst, j, sem):
    return pltpu.make_async_copy(src.at[pl.ds(i, 1), :], dst.at[pl.ds(j, 1), :], sem)


def _dispatch_kernel(dest_hbm, pend_ref, x_ref, buf_hbm, dest_smem, zero_ref, sem_idx, sem_rows):
    i = pl.program_id(0)
    tm = x_ref.shape[0]
    bm = zero_ref.shape[0]

    @pl.when(i == 0)
    def _():
        zero_ref[...] = jnp.zeros_like(zero_ref)

        def last_block(e):
            start = pl.multiple_of(jnp.maximum(pend_ref[e] - bm, 0), bm)
            return pltpu.make_async_copy(zero_ref, buf_hbm.at[pl.ds(start, bm), :], sem_rows)

        for e in range(N_EXPERTS):
            last_block(e).start()
        for e in range(N_EXPERTS):
            last_block(e).wait()

        def unused_block(blk, carry):
            start = pl.multiple_of(blk * bm, bm)
            copy = pltpu.make_async_copy(zero_ref, buf_hbm.at[pl.ds(start, bm), :], sem_rows)
            copy.start()
            copy.wait()
            return carry

        lax.fori_loop(pend_ref[N_EXPERTS - 1] // bm, buf_hbm.shape[0] // bm, unused_block, 0)

    idx_copy = pltpu.make_async_copy(dest_hbm.at[i], dest_smem, sem_idx)
    idx_copy.start()
    idx_copy.wait()

    def body(o, carry):
        for u in range(ROW_UNROLL):
            r = o * ROW_UNROLL + u
            for k in range(TOP_K):
                _row_copy(x_ref, r, buf_hbm, dest_smem[TOP_K * r + k], sem_rows).start()
        return carry

    lax.fori_loop(0, tm // ROW_UNROLL, body, 0)
    for k in range(TOP_K):
        pltpu.make_async_copy(x_ref, buf_hbm.at[pl.ds(0, tm), :], sem_rows).wait()


def _dispatch(xn, dest, pend, n_rows):
    n, d = xn.shape
    tm = min(TOKEN_TILE, n)
    return pl.pallas_call(
        _dispatch_kernel,
        grid=(n // tm,),
        in_specs=[pl.BlockSpec(memory_space=pl.ANY), pl.BlockSpec(memory_space=pltpu.SMEM), _rows(tm, d)],
        out_specs=pl.BlockSpec(memory_space=pl.ANY),
        out_shape=jax.ShapeDtypeStruct((n_rows, d), xn.dtype),
        scratch_shapes=[pltpu.SMEM((TOP_K * tm,), jnp.int32), pltpu.VMEM((EXPERT_BLOCK, d), xn.dtype),
                        pltpu.SemaphoreType.DMA, pltpu.SemaphoreType.DMA],
        compiler_params=_cparams("arbitrary"),
        name="dispatch",
    )(dest.reshape(n // tm, TOP_K * tm), pend, xn)


def _combine_kernel(dest_hbm, y_hbm, x_ref, rf_ref, o_ref, dest_smem, y0_ref, y1_ref, sem_idx, sem_rows):
    i = pl.program_id(0)
    tm = x_ref.shape[0]
    idx_copy = pltpu.make_async_copy(dest_hbm.at[i], dest_smem, sem_idx)
    idx_copy.start()
    idx_copy.wait()
    ybufs = (y0_ref, y1_ref)

    def body(o, carry):
        for u in range(ROW_UNROLL):
            r = o * ROW_UNROLL + u
            for k in range(TOP_K):
                _row_copy(y_hbm, dest_smem[TOP_K * r + k], ybufs[k], r, sem_rows).start()
        return carry

    lax.fori_loop(0, tm // ROW_UNROLL, body, 0)
    for k in range(TOP_K):
        pltpu.make_async_copy(y_hbm.at[pl.ds(0, tm), :], ybufs[k], sem_rows).wait()
    rf = rf_ref[...]
    o_ref[...] = x_ref[...] + (y0_ref[...] * rf[:, 0:1] + y1_ref[...] * rf[:, 1:2])


def _combine(x1, y, dest, rf):
    n, d = x1.shape
    tm = min(TOKEN_TILE, n)
    return pl.pallas_call(
        _combine_kernel,
        grid=(n // tm,),
        in_specs=[pl.BlockSpec(memory_space=pl.ANY), pl.BlockSpec(memory_space=pl.ANY), _rows(tm, d),
                  _rows(tm, LANES)],
        out_specs=_rows(tm, d),
        out_shape=jax.ShapeDtypeStruct((n, d), x1.dtype),
        scratch_shapes=[pltpu.SMEM((TOP_K * tm,), jnp.int32), pltpu.VMEM((tm, d), y.dtype),
                        pltpu.VMEM((tm, d), y.dtype), pltpu.SemaphoreType.DMA, pltpu.SemaphoreType.DMA],
        compiler_params=_cparams("arbitrary"),
        name="combine",
    )(dest.reshape(n // tm, TOP_K * tm), y, x1, rf)


def _moe(x1, xn, logits, wg, wu, wd):
    n, d = x1.shape
    bm = EXPERT_BLOCK
    ri, rf, cnt = _route(logits)
    counts = cnt[0, N_GROUPS:N_GROUPS + N_EXPERTS].astype(jnp.int32)
    psz = (counts + bm - 1) // bm * bm
    pend = jnp.cumsum(psz)
    pstart = pend - psz
    pstart_row = jnp.pad(pstart.astype(F32), (0, LANES - N_EXPERTS))[None, :]
    dest = _dest(ri, pstart_row)[:, :TOP_K]
    n_blocks = -(-(n * TOP_K) // bm) + N_EXPERTS
    blk_start = jnp.arange(n_blocks, dtype=jnp.int32) * bm
    blk_e = jnp.minimum(jnp.sum(pend[None, :] <= blk_start[:, None], axis=1), N_EXPERTS - 1).astype(jnp.int32)
    n_used = (pend[-1] // bm).astype(jnp.int32).reshape(1)
    buf = _dispatch(xn, dest, pend.astype(jnp.int32), n_blocks * bm)
    y = _experts(buf, blk_e, n_used, wg, wu, wd)
    return _combine(x1, y, dest, rf)


def _prep_layer(l, P):
    d = P['w_in'].shape[1]
    cw = P['conv_dw_w'].shape[-1]
    mw = P['ml_norm_g'].shape[-1]
    nw = NA_HEADS * P['na_q_norm_g'].shape[-1]
    off_g = 2 * cw + 4 * mw
    ng = 4 * ML_HEADS
    w = P['w_in'][l]
    w_perm = jnp.concatenate([w[:, :off_g], w[:, off_g + ng:], w[:, off_g:off_g + ng],
                              jnp.zeros((d, LANES - ng), F32)], axis=1).astype(BF16)
    nhd = nw // NA_HEADS
    head_id = jnp.arange(nw) // nhd
    bd = jnp.where(head_id[:, None] == head_id[None, :], 1.0 / nhd, 0.0).astype(BF16)
    wr = jnp.concatenate([P['w_router_group'][l], P['w_router_expert'][l],
                          jnp.zeros((d, LANES - N_GROUPS - N_EXPERTS), F32)], axis=1)
    br = jnp.concatenate([P['b_router_group'][l], P['b_router_expert'][l],
                          jnp.zeros((LANES - N_GROUPS - N_EXPERTS,), F32)])[None, :]
    return dict(
        dims=(cw, mw, nw),
        norm_mix_g=P['norm_mix_g'][l][None, :], w_in=w_perm,
        gate_b=jnp.pad(P['ml_gate_b'][l].reshape(-1), (0, LANES - ng))[None, :],
        nq=jnp.tile(P['na_q_norm_g'][l], NA_HEADS)[None, :], nk=jnp.tile(P['na_k_norm_g'][l], NA_HEADS)[None, :],
        bd=bd,
        conv_dw_w=P['conv_dw_w'][l], conv_dw_b=P['conv_dw_b'][l][None, :], conv_norm_g=P['conv_norm_g'][l][None, :],
        ml_qk_conv_w=P['ml_qk_conv_w'][l], ml_qk_conv_b=P['ml_qk_conv_b'][l][None, :],
        ml_norm_g=P['ml_norm_g'][l][None, :], na_table=_na_bias_table(P['na_rpb'][l]),
        w_out=P['w_out'][l].astype(BF16), norm_ffn_g=P['norm_ffn_g'][l][None, :],
        wrh=wr.astype(BF16), wrl=(wr - wr.astype(BF16).astype(F32)).astype(BF16), br=br,
        wg=P['w_exp_gate'][l].astype(BF16), wu=P['w_exp_up'][l].astype(BF16), wd=P['w_exp_down'][l].astype(BF16),
    )


def _layer(x, B, T, p):
    dims = p['dims']
    u, qkpre, v, o, naq, nak, nav, gates = _in_proj(x, p['norm_mix_g'], p['w_in'], p['gate_b'], p['nq'], p['nk'],
                                                    p['bd'], dims)
    yc, hf, hb, yna = _mixers(B, T, u, qkpre, v, naq, nak, nav, gates, p)
    x1, xn, logits = _out_proj(yc, hf, hb, o, yna, x, p['ml_norm_g'], p['w_out'], p['norm_ffn_g'], p['wrh'], p['wrl'],
                               p['br'], dims)
    return _moe(x1, xn, logits, p['wg'], p['wu'], p['wd'])


def kernel(x_prompt, x_sample, norm_mix_g, w_in, conv_dw_w, conv_dw_b, conv_norm_g, ml_qk_conv_w, ml_qk_conv_b,
           ml_gate_b, ml_norm_g, na_q_norm_g, na_k_norm_g, na_rpb, w_out, norm_ffn_g, w_router_group,
           b_router_group, w_router_expert, b_router_expert, w_exp_gate, w_exp_up, w_exp_down):
    P = dict(norm_mix_g=norm_mix_g, w_in=w_in, conv_dw_w=conv_dw_w, conv_dw_b=conv_dw_b, conv_norm_g=conv_norm_g,
             ml_qk_conv_w=ml_qk_conv_w, ml_qk_conv_b=ml_qk_conv_b, ml_gate_b=ml_gate_b, ml_norm_g=ml_norm_g,
             na_q_norm_g=na_q_norm_g, na_k_norm_g=na_k_norm_g, na_rpb=na_rpb, w_out=w_out, norm_ffn_g=norm_ffn_g,
             w_router_group=w_router_group, b_router_group=b_router_group, w_router_expert=w_router_expert,
             b_router_expert=b_router_expert, w_exp_gate=w_exp_gate, w_exp_up=w_exp_up, w_exp_down=w_exp_down)
    depth = w_in.shape[0]
    d = x_prompt.shape[-1]
    groups = [(x_prompt.shape[0], x_prompt.shape[1], x_prompt.reshape(-1, d)),
              (x_sample.shape[0], x_sample.shape[1], x_sample.reshape(-1, d))]
    for l in range(depth):
        p = _prep_layer(l, P)
        groups = [(B, T, _layer(x, B, T, p)) for B, T, x in groups]
    return tuple(x.reshape(B, T, d) for B, T, x in groups)
```

```python
import functools

import numpy as np

import jax
import jax.numpy as jnp
from jax import lax
from jax.experimental import pallas as pl
from jax.experimental.pallas import tpu as pltpu

F32 = jnp.float32
BF16 = jnp.bfloat16

ML_HEADS = 4
NA_HEADS = 4
CONV_K = 31
ML_CHUNK = 128
GRID_W = 64
NA_WIN_ROWS = 8
NA_WIN_COLS = 16
N_GROUPS = 8
EXPERTS_PER_GROUP = 8
N_EXPERTS = N_GROUPS * EXPERTS_PER_GROUP
TOP_K = 2
EPS = 1e-6

LANES = 128
SUBLANES = 8
VMEM_LIMIT = 48 * 1024 * 1024

TOKEN_TILE = 512
EXPERT_BLOCK = 256


def _cparams(*sem):
    return pltpu.CompilerParams(dimension_semantics=sem, vmem_limit_bytes=VMEM_LIMIT)


def _full(shape):
    return pl.BlockSpec(shape, lambda *_: (0,) * len(shape))


def _rows(tm, width):
    return pl.BlockSpec((tm, width), lambda i: (i, 0))


def _group_mean(zz, bd):
    hi = zz.astype(BF16)
    lo = (zz - hi.astype(F32)).astype(BF16)
    return (jnp.dot(hi, bd, preferred_element_type=F32) + jnp.dot(lo, bd, preferred_element_type=F32))


def _in_proj_kernel(x_ref, g_ref, w_ref, gb_ref, nq_ref, nk_ref, bd_ref,
                    u_ref, qk_ref, v_ref, o_ref, naq_ref, nak_ref, nav_ref, gt_ref, *, dims):
    cw, mw, nw = dims
    x = x_ref[...]
    ms = jnp.mean(x * x, axis=-1, keepdims=True)
    xn = (x * lax.rsqrt(ms + EPS) * g_ref[...]).astype(BF16)

    def proj(lo, width):
        return jnp.dot(xn, w_ref[:, lo:lo + width], preferred_element_type=F32)

    zc = proj(0, 2 * cw)
    u_ref[...] = zc[:, :cw] * jax.nn.sigmoid(zc[:, cw:])
    off = 2 * cw
    qk_ref[...] = proj(off, 2 * mw)
    v_ref[...] = proj(off + 2 * mw, mw).astype(BF16)
    o_ref[...] = proj(off + 3 * mw, mw)
    off += 4 * mw
    bd = bd_ref[...]
    zq = proj(off, nw)
    zk = proj(off + nw, nw)
    hd = nw // NA_HEADS
    naq_ref[...] = (zq * lax.rsqrt(_group_mean(zq * zq, bd) + EPS) * nq_ref[...] * (hd ** -0.5)).astype(BF16)
    nak_ref[...] = (zk * lax.rsqrt(_group_mean(zk * zk, bd) + EPS) * nk_ref[...]).astype(BF16)
    nav_ref[...] = proj(off + 2 * nw, nw).astype(BF16)
    off += 3 * nw
    gt_ref[...] = proj(off, LANES) + gb_ref[...]


def _in_proj(x, g, w, gate_b, nq, nk, bd, dims):
    n, d = x.shape
    cw, mw, nw = dims
    tm = min(TOKEN_TILE, n)
    outs = [(cw, F32), (2 * mw, F32), (mw, BF16), (mw, F32), (nw, BF16), (nw, BF16), (nw, BF16), (LANES, F32)]
    return pl.pallas_call(
        functools.partial(_in_proj_kernel, dims=dims),
        grid=(n // tm,),
        in_specs=[_rows(tm, d), _full(g.shape), _full(w.shape), _full(gate_b.shape),
                  _full(nq.shape), _full(nk.shape), _full(bd.shape)],
        out_specs=[_rows(tm, wd) for wd, _ in outs],
        out_shape=[jax.ShapeDtypeStruct((n, wd), dt) for wd, dt in outs],
        compiler_params=_cparams("parallel"),
        name="in_proj",
    )(x, g, w, gate_b, nq, nk, bd)


OUT_SUB = 128


def _out_proj_kernel(yc_ref, hf_ref, hb_ref, o_ref, yna_ref, x_ref, mg_ref, w_ref, fg_ref, wrh_ref, wrl_ref, br_ref,
                     x1_ref, xn_ref, lg_ref, *, dims):
    cw, mw, nw = dims
    hd = mw // ML_HEADS
    for r0 in range(0, x_ref.shape[0], OUT_SUB):
        rs = slice(r0, r0 + OUT_SUB)
        hm = hf_ref[rs, :] + hb_ref[rs, :]
        parts = []
        for h in range(ML_HEADS):
            seg = hm[:, h * hd:(h + 1) * hd]
            parts.append(seg * lax.rsqrt(jnp.mean(seg * seg, axis=-1, keepdims=True) + EPS))
        yml = jnp.concatenate(parts, axis=-1) * mg_ref[...] * jax.nn.sigmoid(o_ref[rs, :])
        y = jnp.dot(yc_ref[rs, :], w_ref[0:cw, :], preferred_element_type=F32)
        y += jnp.dot(yml.astype(BF16), w_ref[cw:cw + mw, :], preferred_element_type=F32)
        y += jnp.dot(yna_ref[rs, :], w_ref[cw + mw:cw + mw + nw, :], preferred_element_type=F32)
        x1 = x_ref[rs, :] + y
        x1_ref[rs, :] = x1
        xn = x1 * lax.rsqrt(jnp.mean(x1 * x1, axis=-1, keepdims=True) + EPS) * fg_ref[...]
        xn_ref[rs, :] = xn
        xh = xn.astype(BF16)
        xl = (xn - xh.astype(F32)).astype(BF16)
        lg = jnp.dot(xh, wrh_ref[...], preferred_element_type=F32)
        lg += jnp.dot(xl, wrh_ref[...], preferred_element_type=F32)
        lg += jnp.dot(xh, wrl_ref[...], preferred_element_type=F32)
        lg_ref[rs, :] = lg + br_ref[...]


def _out_proj(yc, hf, hb, o, yna, x, mg, w, fg, wrh, wrl, br, dims):
    n, d = x.shape
    cw, mw, nw = dims
    tm = min(TOKEN_TILE, n)
    assert tm % OUT_SUB == 0
    return pl.pallas_call(
        functools.partial(_out_proj_kernel, dims=dims),
        grid=(n // tm,),
        in_specs=[_rows(tm, cw), _rows(tm, mw), _rows(tm, mw), _rows(tm, mw), _rows(tm, nw), _rows(tm, d),
                  _full(mg.shape), _full(w.shape), _full(fg.shape), _full(wrh.shape), _full(wrl.shape),
                  _full(br.shape)],
        out_specs=[_rows(tm, d), _rows(tm, d), _rows(tm, LANES)],
        out_shape=[jax.ShapeDtypeStruct((n, d), F32), jax.ShapeDtypeStruct((n, d), F32),
                   jax.ShapeDtypeStruct((n, LANES), F32)],
        compiler_params=_cparams("parallel"),
        name="out_proj",
    )(yc, hf, hb, o, yna, x, mg, w, fg, wrh, wrl, br)


def _expert_kernel(be_ref, nb_ref, x_ref, wg_ref, wu_ref, wd_ref, y_ref, wgb_ref, wub_ref, wdb_ref):
    i = pl.program_id(0)

    @pl.when((i == 0) | (be_ref[i] != be_ref[jnp.maximum(i - 1, 0)]))
    def _():
        wgb_ref[...] = wg_ref[0].astype(BF16)
        wub_ref[...] = wu_ref[0].astype(BF16)
        wdb_ref[...] = wd_ref[0].astype(BF16)

    @pl.when(i < nb_ref[0])
    def _():
        xb = x_ref[...].astype(BF16)
        g = jnp.dot(xb, wgb_ref[...], preferred_element_type=F32)
        u = jnp.dot(xb, wub_ref[...], preferred_element_type=F32)
        h = (g * jax.nn.sigmoid(g) * u).astype(BF16)
        y_ref[...] = jnp.dot(h, wdb_ref[...], preferred_element_type=F32)

    @pl.when(i >= nb_ref[0])
    def _():
        y_ref[...] = jnp.zeros_like(y_ref)


def _experts(buf, blk_e, n_used, wg, wu, wd, layer):
    p, d = buf.shape
    de = wg.shape[-1]
    bm = EXPERT_BLOCK
    grid_spec = pltpu.PrefetchScalarGridSpec(
        num_scalar_prefetch=2,
        grid=(p // bm,),
        in_specs=[pl.BlockSpec((bm, d), lambda i, be, nb: (jnp.minimum(i, nb[0] - 1), 0)),
                  pl.BlockSpec((None, 1, d, de), lambda i, be, nb: (layer, be[i], 0, 0)),
                  pl.BlockSpec((None, 1, d, de), lambda i, be, nb: (layer, be[i], 0, 0)),
                  pl.BlockSpec((None, 1, de, d), lambda i, be, nb: (layer, be[i], 0, 0))],
        out_specs=pl.BlockSpec((bm, d), lambda i, be, nb: (i, 0)),
        scratch_shapes=[pltpu.VMEM((d, de), BF16), pltpu.VMEM((d, de), BF16), pltpu.VMEM((de, d), BF16)],
    )
    return pl.pallas_call(
        _expert_kernel,
        grid_spec=grid_spec,
        out_shape=jax.ShapeDtypeStruct((p, d), F32),
        compiler_params=_cparams("arbitrary"),
        name="experts",
    )(blk_e, n_used, buf, wg, wu, wd)


NA_QROWS = 4
NEG_BIG = -1e30


def _bias_expand_kernel(rpb_ref, selr_ref, selc_ref, o_ref):
    rows = jnp.dot(selr_ref[...], rpb_ref[0], preferred_element_type=F32, precision=lax.Precision.HIGHEST)
    o_ref[0] = jnp.dot(rows, selc_ref[...], preferred_element_type=F32, precision=lax.Precision.HIGHEST)


def _na_bias_table(rpb):
    nh = rpb.shape[0]
    a = np.arange(NA_QROWS)[:, None]
    i = np.arange(3 * NA_QROWS)[None, :]
    half = NA_WIN_ROWS // 2
    a0 = 0 * a
    row_ok = np.stack([(i >= a0 + NA_QROWS) & (i < a0 + NA_QROWS + NA_WIN_ROWS),
                       (i >= a + NA_QROWS - half) & (i < a + NA_QROWS - half + NA_WIN_ROWS),
                       (i >= a0 + 2 * NA_QROWS - NA_WIN_ROWS) & (i < a0 + 2 * NA_QROWS)])
    ro = np.clip(i - NA_QROWS - a + (NA_WIN_ROWS - 1), 0, 2 * NA_WIN_ROWS - 2)
    c = np.arange(GRID_W)
    cs = np.clip(c - NA_WIN_COLS // 2, 0, GRID_W - NA_WIN_COLS)
    col_ok = (c[None, :] >= cs[:, None]) & (c[None, :] < cs[:, None] + NA_WIN_COLS)
    co = np.clip(c[None, :] - c[:, None] + (NA_WIN_COLS - 1), 0, 2 * NA_WIN_COLS - 2)
    nr, nc = 2 * SUBLANES, LANES
    sel_r = (ro.reshape(-1)[:, None] == np.arange(nr)[None, :]).astype(np.float32)
    sel_c = (np.arange(nc)[:, None] == co.reshape(-1)[None, :]).astype(np.float32)
    rpb_p = jnp.pad(rpb.astype(F32), ((0, 0), (0, nr - rpb.shape[1]), (0, nc - rpb.shape[2])))
    nx, ny = sel_r.shape[0], sel_c.shape[1]
    bias = pl.pallas_call(
        _bias_expand_kernel,
        grid=(nh,),
        in_specs=[pl.BlockSpec((1, nr, nc), lambda h: (h, 0, 0)), _full(sel_r.shape), _full(sel_c.shape)],
        out_specs=pl.BlockSpec((1, nx, ny), lambda h: (h, 0, 0)),
        out_shape=jax.ShapeDtypeStruct((nh, nx, ny), F32),
        compiler_params=_cparams("parallel"),
        name="na_bias_expand",
    )(rpb_p, jnp.asarray(sel_r), jnp.asarray(sel_c))
    bias = bias.reshape(nh, NA_QROWS, 3 * NA_QROWS, GRID_W, GRID_W).transpose(0, 1, 3, 2, 4)
    ok = row_ok[:, None, :, None, :, None] & col_ok[None, None, None, :, None, :]
    tab = jnp.where(jnp.asarray(ok), bias[None], NEG_BIG)
    return tab.reshape(3, nh, NA_QROWS * GRID_W, 3 * NA_QROWS * GRID_W)


def _na_kernel(q_ref, kp_ref, kc_ref, kn_ref, vp_ref, vc_ref, vn_ref, b_ref, o_ref):
    q = q_ref[...]
    kcat = jnp.concatenate([kp_ref[...], kc_ref[...], kn_ref[...]], axis=0)
    vcat = jnp.concatenate([vp_ref[...], vc_ref[...], vn_ref[...]], axis=0)
    nw = q.shape[-1]
    hd = nw // NA_HEADS
    lane_head = lax.broadcasted_iota(jnp.int32, q.shape, 1) // hd
    acc = jnp.zeros(q.shape, F32)
    for h in range(NA_HEADS):
        sel = lane_head == h
        qm = jnp.where(sel, q, jnp.zeros_like(q))
        s = lax.dot_general(qm, kcat, (((1,), (1,)), ((), ())), preferred_element_type=F32) + b_ref[0, h]
        m = jnp.max(s, axis=-1, keepdims=True)
        p = jnp.exp(s - m)
        l = jnp.sum(p, axis=-1, keepdims=True)
        oh = jnp.dot(p.astype(BF16), vcat, preferred_element_type=F32)
        acc = jnp.where(sel, oh / l, acc)
    o_ref[...] = acc.astype(o_ref.dtype)


def _neigh_attn_pallas(q, k, v, table, B, T):
    n, nw = q.shape
    tq = NA_QROWS * GRID_W
    J = T // tq
    assert T % tq == 0 and J >= 2 and T // GRID_W >= NA_WIN_ROWS
    blk = (tq, nw)
    cur = pl.BlockSpec(blk, lambda b, j: (b * J + j, 0))
    prev = pl.BlockSpec(blk, lambda b, j: (b * J + jnp.maximum(j - 1, 0), 0))
    nxt = pl.BlockSpec(blk, lambda b, j: (b * J + jnp.minimum(j + 1, J - 1), 0))
    tab = pl.BlockSpec((1,) + table.shape[1:],
                       lambda b, j: (jnp.where(j == 0, 0, jnp.where(j == J - 1, 2, 1)), 0, 0, 0))
    return pl.pallas_call(
        _na_kernel,
        grid=(B, J),
        in_specs=[cur, prev, cur, nxt, prev, cur, nxt, tab],
        out_specs=cur,
        out_shape=jax.ShapeDtypeStruct((n, nw), BF16),
        compiler_params=_cparams("parallel", "arbitrary"),
        name="neigh_attn",
    )(q, k, k, k, v, v, v, table)


HALO = 16
CONV_ROWS = 64


def _local_conv_kernel(u_ref, up_ref, un_ref, qk_ref, qkp_ref, qkn_ref, cw_ref, cb_ref, cg_ref, mw_ref, mb_ref,
                       yc_ref, q_ref, k_ref, wu_ref, wqk_ref, sh_ref, *, k_scale):
    t = pl.program_id(1)
    nt = pl.num_programs(1)
    tm = u_ref.shape[0]
    keep_p = (t > 0).astype(F32)
    keep_n = (t < nt - 1).astype(F32)
    wu_ref[0:HALO, :] = up_ref[...] * keep_p
    wu_ref[HALO:HALO + tm, :] = u_ref[...]
    wu_ref[HALO + tm:, :] = un_ref[...] * keep_n
    wqk_ref[0:HALO, :] = qkp_ref[...] * keep_p
    wqk_ref[HALO:HALO + tm, :] = qk_ref[...]
    wqk_ref[HALO + tm:, :] = qkn_ref[...] * keep_n
    kc = cw_ref.shape[0]
    km = mw_ref.shape[0]
    mw2 = q_ref.shape[-1]
    nsh = sh_ref.shape[1]
    for ph in range(1, SUBLANES):
        sh_ref[ph - 1] = wu_ref[ph:ph + nsh, :]
    for r0 in range(0, tm, CONV_ROWS):
        acc = jnp.zeros((CONV_ROWS, u_ref.shape[-1]), F32) + cb_ref[...]
        for k in range(kc):
            s = r0 + HALO - kc // 2 + k
            ph = s % SUBLANES
            tap = wu_ref[s:s + CONV_ROWS, :] if ph == 0 else sh_ref[ph - 1, s - ph:s - ph + CONV_ROWS, :]
            acc = acc + tap * cw_ref[k:k + 1, :]
        y = acc * lax.rsqrt(jnp.mean(acc * acc, axis=-1, keepdims=True) + EPS) * cg_ref[...]
        yc_ref[r0:r0 + CONV_ROWS, :] = (y * jax.nn.sigmoid(y)).astype(yc_ref.dtype)
        acc = jnp.zeros((CONV_ROWS, qk_ref.shape[-1]), F32) + mb_ref[...]
        for k in range(km):
            s = r0 + HALO - km // 2 + k
            acc = acc + wqk_ref[s:s + CONV_ROWS, :] * mw_ref[k:k + 1, :]
        y = acc * jax.nn.sigmoid(acc)
        q_ref[r0:r0 + CONV_ROWS, :] = y[:, :mw2].astype(q_ref.dtype)
        k_ref[r0:r0 + CONV_ROWS, :] = (y[:, mw2:] * k_scale).astype(k_ref.dtype)


def _local_conv(u, qkpre, cw, cb, cg, mw, mb, B, T):
    n, cwid = u.shape
    qw = qkpre.shape[-1]
    tm = min(TOKEN_TILE, T)
    nt = T // tm
    hb = tm // HALO
    nhb = n // HALO
    assert T % tm == 0 and tm % CONV_ROWS == 0 and CONV_K // 2 <= HALO

    def cur(w):
        return pl.BlockSpec((tm, w), lambda b, t: (b * nt + t, 0))

    def prev(w):
        return pl.BlockSpec((HALO, w), lambda b, t: (jnp.maximum((b * nt + t) * hb - 1, 0), 0))

    def nxt(w):
        return pl.BlockSpec((HALO, w), lambda b, t: (jnp.minimum((b * nt + t + 1) * hb, nhb - 1), 0))

    hd = (qw // 2) // ML_HEADS
    return pl.pallas_call(
        functools.partial(_local_conv_kernel, k_scale=hd ** -0.5),
        grid=(B, nt),
        in_specs=[cur(cwid), prev(cwid), nxt(cwid), cur(qw), prev(qw), nxt(qw),
                  _full(cw.shape), _full(cb.shape), _full(cg.shape), _full(mw.shape), _full(mb.shape)],
        out_specs=[cur(cwid), cur(qw // 2), cur(qw // 2)],
        out_shape=[jax.ShapeDtypeStruct((n, cwid), BF16), jax.ShapeDtypeStruct((n, qw // 2), BF16),
                   jax.ShapeDtypeStruct((n, qw // 2), BF16)],
        scratch_shapes=[pltpu.VMEM((tm + 2 * HALO, cwid), F32), pltpu.VMEM((tm + 2 * HALO, qw), F32),
                        pltpu.VMEM((SUBLANES - 1, tm + 2 * HALO - SUBLANES, cwid), F32)],
        compiler_params=_cparams("parallel", "parallel"),
        name="local_conv",
    )(u, u, u, qkpre, qkpre, qkpre, cw, cb, cg, mw, mb)


def _scan_rows(x, reverse, combine, fill):
    L = x.shape[0]
    row = lax.broadcasted_iota(jnp.int32, x.shape, 0)
    s = 1
    while s < L:
        if reverse:
            x = combine(x, jnp.where(row < L - s, pltpu.roll(x, L - s, axis=0), fill))
        else:
            x = combine(x, jnp.where(row >= s, pltpu.roll(x, s, axis=0), fill))
        s *= 2
    return x


def _mlstm_direction(q_ref, k_ref, v_ref, g_ref, h_ref, c_ref, m_ref, reverse):
    L = q_ref.shape[0]
    hd = q_ref.shape[-1] // ML_HEADS
    ci = 2 * ML_HEADS if reverse else 0
    cf = ci + ML_HEADS
    g = g_ref[...]
    gi = pltpu.roll(g, ML_HEADS, axis=1)
    lf = jnp.minimum(g, 0.0) - jnp.log1p(jnp.exp(-jnp.abs(g)))
    b = _scan_rows(lf, reverse, jnp.add, 0.0)
    r = gi - b
    cm = _scan_rows(r, reverse, jnp.maximum, NEG_BIG)
    total = b[0:1, :] if reverse else b[L - 1:L, :]
    m_prev = m_ref[...]
    inter = b + m_prev
    m_j = jnp.maximum(inter, b + cm)
    a_t = b - m_j
    e_t = jnp.exp(inter - m_j)
    z_t = jnp.exp(-m_j)
    w = total - b + gi
    m_loc = jnp.max(w, axis=0, keepdims=True)
    wa_t = jnp.exp(w - m_loc)
    m_new = jnp.maximum(total + m_prev, m_loc)
    dec = jnp.exp(total + m_prev - m_new)
    inc = jnp.exp(m_loc - m_new)
    m_ref[...] = m_new
    rt = jnp.transpose(r)
    jj = lax.broadcasted_iota(jnp.int32, (L, L), 0)
    ss = lax.broadcasted_iota(jnp.int32, (L, L), 1)
    visible = (ss >= jj) if reverse else (ss <= jj)
    ones = jnp.ones((L, hd), BF16)
    for h in range(ML_HEADS):
        sl = slice(h * hd, (h + 1) * hd)
        col = slice(cf + h, cf + h + 1)
        qh = q_ref[:, sl]
        kh = k_ref[:, sl]
        v_ext = jnp.concatenate([v_ref[:, sl], ones], axis=-1)
        c_prev = c_ref[h]
        pm = jnp.exp(jnp.where(visible, a_t[:, col] + rt[col, :], NEG_BIG))
        qk = lax.dot_general(qh, kh, (((1,), (1,)), ((), ())), preferred_element_type=F32) * pm
        tot = (jnp.dot(qk.astype(BF16), v_ext, preferred_element_type=F32)
               + e_t[:, col] * jnp.dot(qh, c_prev.astype(BF16), preferred_element_type=F32))
        h_ref[:, sl] = tot[:, :hd] / jnp.maximum(jnp.abs(tot[:, hd:]), z_t[:, col])
        ka = (kh.astype(F32) * wa_t[:, col]).astype(BF16)
        s_ext = lax.dot_general(ka, v_ext, (((0,), (0,)), ((), ())), preferred_element_type=F32)
        c_ref[h] = dec[:, col] * c_prev + inc[:, col] * s_ext


def _mlstm_kernel(qf_ref, kf_ref, vf_ref, gf_ref, qb_ref, kb_ref, vb_ref, gb_ref, hf_ref, hb_ref,
                  cf_ref, cb_ref, mf_ref, mb_ref):
    @pl.when(pl.program_id(1) == 0)
    def _():
        cf_ref[...] = jnp.zeros_like(cf_ref)
        cb_ref[...] = jnp.zeros_like(cb_ref)
        mf_ref[...] = jnp.zeros_like(mf_ref)
        mb_ref[...] = jnp.zeros_like(mb_ref)

    _mlstm_direction(qf_ref, kf_ref, vf_ref, gf_ref, hf_ref, cf_ref, mf_ref, False)
    _mlstm_direction(qb_ref, kb_ref, vb_ref, gb_ref, hb_ref, cb_ref, mb_ref, True)


def _mlstm(q, k, v, gates, B, T):
    n, mw = q.shape
    L = ML_CHUNK
    nc = T // L
    hd = mw // ML_HEADS
    assert T % L == 0 and hd == LANES

    def fwd(w):
        return pl.BlockSpec((L, w), lambda b, c: (b * nc + c, 0))

    def bwd(w):
        return pl.BlockSpec((L, w), lambda b, c: (b * nc + nc - 1 - c, 0))

    return pl.pallas_call(
        _mlstm_kernel,
        grid=(B, nc),
        in_specs=[fwd(mw), fwd(mw), fwd(mw), fwd(LANES), bwd(mw), bwd(mw), bwd(mw), bwd(LANES)],
        out_specs=[fwd(mw), bwd(mw)],
        out_shape=[jax.ShapeDtypeStruct((n, mw), F32), jax.ShapeDtypeStruct((n, mw), F32)],
        scratch_shapes=[pltpu.VMEM((ML_HEADS, hd, 2 * hd), F32), pltpu.VMEM((ML_HEADS, hd, 2 * hd), F32),
                        pltpu.VMEM((1, LANES), F32), pltpu.VMEM((1, LANES), F32)],
        compiler_params=_cparams("parallel", "arbitrary"),
        name="mlstm",
    )(q, k, v, gates, q, k, v, gates)


def _mixers(B, T, u, qkpre, v, naq, nak, nav, gates, p):
    yc, q, k = _local_conv(u, qkpre, p['conv_dw_w'], p['conv_dw_b'], p['conv_norm_g'],
                           p['ml_qk_conv_w'], p['ml_qk_conv_b'], B, T)
    hf, hb = _mlstm(q, k, v, gates, B, T)
    y_na = _neigh_attn_pallas(naq, nak, nav, p['na_table'], B, T)
    return yc, hf, hb, y_na


def _first_lane(mask, lane):
    return jnp.min(jnp.where(mask, lane, LANES), axis=-1, keepdims=True)


def _route_kernel(lg_ref, ri_ref, rf_ref, cnt_ref, carry_ref):
    @pl.when(pl.program_id(0) == 0)
    def _():
        carry_ref[...] = jnp.zeros_like(carry_ref)

    lg = lg_ref[...]
    tm = lg.shape[0]
    lane = lax.broadcasted_iota(jnp.int32, lg.shape, 1)
    gl = jnp.where(lane < N_GROUPS, lg, NEG_BIG)
    gmax = jnp.max(gl, axis=-1, keepdims=True)
    pg = 1.0 / jnp.sum(jnp.exp(gl - gmax), axis=-1, keepdims=True)
    gsel = _first_lane(gl == gmax, lane)
    lo = N_GROUPS + gsel * EXPERTS_PER_GROUP
    in_grp = (lane >= lo) & (lane < lo + EXPERTS_PER_GROUP)
    el = jnp.where(in_grp, lg, NEG_BIG)
    emax = jnp.max(el, axis=-1, keepdims=True)
    esum = jnp.sum(jnp.exp(el - emax), axis=-1, keepdims=True)
    idx1 = _first_lane(in_grp & (el == emax), lane)
    el2 = jnp.where(lane == idx1, NEG_BIG, el)
    emax2 = jnp.max(el2, axis=-1, keepdims=True)
    idx2 = _first_lane(in_grp & (el2 == emax2) & (lane != idx1), lane)
    p1 = 1.0 / esum
    p2 = jnp.exp(emax2 - emax) / esum
    g1 = pg * p1 / (p1 + p2)
    g2 = pg * p2 / (p1 + p2)
    hit1 = lane == idx1
    hit2 = lane == idx2
    onehot = jnp.where(hit1 | hit2, 1.0, 0.0)
    r = lax.broadcasted_iota(jnp.int32, (tm, tm), 0)
    c = lax.broadcasted_iota(jnp.int32, (tm, tm), 1)
    tri = jnp.where(r > c, 1.0, 0.0).astype(BF16)
    prefix = jnp.dot(tri, onehot.astype(BF16), preferred_element_type=F32) + carry_ref[...]
    rank1 = jnp.sum(jnp.where(hit1, prefix, 0.0), axis=-1, keepdims=True).astype(jnp.int32)
    rank2 = jnp.sum(jnp.where(hit2, prefix, 0.0), axis=-1, keepdims=True).astype(jnp.int32)
    carry = carry_ref[...] + jnp.sum(onehot, axis=0, keepdims=True)
    carry_ref[...] = carry
    cnt_ref[...] = carry
    ri_ref[...] = jnp.where(lane == 0, idx1 - N_GROUPS, jnp.where(lane == 1, idx2 - N_GROUPS,
                            jnp.where(lane == 2, rank1, jnp.where(lane == 3, rank2, 0))))
    rf_ref[...] = jnp.where(lane == 0, g1, jnp.where(lane == 1, g2, 0.0))


def _dest_kernel(ri_ref, ps_ref, d_ref):
    ri = ri_ref[...]
    lane = lax.broadcasted_iota(jnp.int32, ri.shape, 1)
    ps = ps_ref[...]
    out = jnp.zeros(ri.shape, jnp.int32)
    for k in range(TOP_K):
        start = jnp.sum(jnp.where(lane == ri[:, k:k + 1], ps, 0.0), axis=-1, keepdims=True).astype(jnp.int32)
        out = jnp.where(lane == k, start + ri[:, TOP_K + k:TOP_K + k + 1], out)
    d_ref[...] = out


def _dest(ri, pstart_row):
    n = ri.shape[0]
    tm = min(TOKEN_TILE, n)
    return pl.pallas_call(
        _dest_kernel,
        grid=(n // tm,),
        in_specs=[_rows(tm, LANES), _full((1, LANES))],
        out_specs=_rows(tm, LANES),
        out_shape=jax.ShapeDtypeStruct((n, LANES), jnp.int32),
        compiler_params=_cparams("parallel"),
        name="dest",
    )(ri, pstart_row)


def _route(logits):
    n = logits.shape[0]
    tm = min(TOKEN_TILE, n)
    return pl.pallas_call(
        _route_kernel,
        grid=(n // tm,),
        in_specs=[_rows(tm, LANES)],
        out_specs=[_rows(tm, LANES), _rows(tm, LANES), _full((1, LANES))],
        out_shape=[jax.ShapeDtypeStruct((n, LANES), jnp.int32), jax.ShapeDtypeStruct((n, LANES), F32),
                   jax.ShapeDtypeStruct((1, LANES), F32)],
        scratch_shapes=[pltpu.VMEM((1, LANES), F32)],
        compiler_params=_cparams("arbitrary"),
        name="route",
    )(logits)


ROW_UNROLL = 8


def _row_copy(src, i, dst, j, sem):
    return pltpu.make_async_copy(src.at[pl.ds(i, 1), :], dst.at[pl.ds(j, 1), :], sem)


def _dispatch_kernel(dest_hbm, pend_ref, x_ref, buf_hbm, dest_smem, zero_ref, sem_idx, sem_rows):
    i = pl.program_id(0)
    tm = x_ref.shape[0]
    bm = zero_ref.shape[0]

    @pl.when(i == 0)
    def _():
        zero_ref[...] = jnp.zeros_like(zero_ref)

        def last_block(e):
            start = pl.multiple_of(jnp.maximum(pend_ref[e] - bm, 0), bm)
            return pltpu.make_async_copy(zero_ref, buf_hbm.at[pl.ds(start, bm), :], sem_rows)

        for e in range(N_EXPERTS):
            last_block(e).start()
        for e in range(N_EXPERTS):
            last_block(e).wait()

        def unused_block(blk, carry):
            start = pl.multiple_of(blk * bm, bm)
            copy = pltpu.make_async_copy(zero_ref, buf_hbm.at[pl.ds(start, bm), :], sem_rows)
            copy.start()
            copy.wait()
            return carry

        lax.fori_loop(pend_ref[N_EXPERTS - 1] // bm, buf_hbm.shape[0] // bm, unused_block, 0)

    idx_copy = pltpu.make_async_copy(dest_hbm.at[i], dest_smem, sem_idx)
    idx_copy.start()
    idx_copy.wait()

    def body(o, carry):
        for u in range(ROW_UNROLL):
            r = o * ROW_UNROLL + u
            for k in range(TOP_K):
                _row_copy(x_ref, r, buf_hbm, dest_smem[TOP_K * r + k], sem_rows).start()
        return carry

    lax.fori_loop(0, tm // ROW_UNROLL, body, 0)
    for k in range(TOP_K):
        pltpu.make_async_copy(x_ref, buf_hbm.at[pl.ds(0, tm), :], sem_rows).wait()


def _dispatch(xn, dest, pend, n_rows):
    n, d = xn.shape
    tm = min(TOKEN_TILE, n)
    return pl.pallas_call(
        _dispatch_kernel,
        grid=(n // tm,),
        in_specs=[pl.BlockSpec(memory_space=pl.ANY), pl.BlockSpec(memory_space=pltpu.SMEM), _rows(tm, d)],
        out_specs=pl.BlockSpec(memory_space=pl.ANY),
        out_shape=jax.ShapeDtypeStruct((n_rows, d), xn.dtype),
        scratch_shapes=[pltpu.SMEM((TOP_K * tm,), jnp.int32), pltpu.VMEM((EXPERT_BLOCK, d), xn.dtype),
                        pltpu.SemaphoreType.DMA, pltpu.SemaphoreType.DMA],
        compiler_params=_cparams("arbitrary"),
        name="dispatch",
    )(dest.reshape(n // tm, TOP_K * tm), pend, xn)


def _combine_kernel(dest_hbm, y_hbm, x_ref, rf_ref, o_ref, dest_smem, y0_ref, y1_ref, sem_idx, sem_rows):
    i = pl.program_id(0)
    tm = x_ref.shape[0]
    idx_copy = pltpu.make_async_copy(dest_hbm.at[i], dest_smem, sem_idx)
    idx_copy.start()
    idx_copy.wait()
    ybufs = (y0_ref, y1_ref)

    def body(o, carry):
        for u in range(ROW_UNROLL):
            r = o * ROW_UNROLL + u
            for k in range(TOP_K):
                _row_copy(y_hbm, dest_smem[TOP_K * r + k], ybufs[k], r, sem_rows).start()
        return carry

    lax.fori_loop(0, tm // ROW_UNROLL, body, 0)
    for k in range(TOP_K):
        pltpu.make_async_copy(y_hbm.at[pl.ds(0, tm), :], ybufs[k], sem_rows).wait()
    rf = rf_ref[...]
    o_ref[...] = x_ref[...] + (y0_ref[...] * rf[:, 0:1] + y1_ref[...] * rf[:, 1:2])


def _combine(x1, y, dest, rf):
    n, d = x1.shape
    tm = min(TOKEN_TILE, n)
    return pl.pallas_call(
        _combine_kernel,
        grid=(n // tm,),
        in_specs=[pl.BlockSpec(memory_space=pl.ANY), pl.BlockSpec(memory_space=pl.ANY), _rows(tm, d),
                  _rows(tm, LANES)],
        out_specs=_rows(tm, d),
        out_shape=jax.ShapeDtypeStruct((n, d), x1.dtype),
        scratch_shapes=[pltpu.SMEM((TOP_K * tm,), jnp.int32), pltpu.VMEM((tm, d), y.dtype),
                        pltpu.VMEM((tm, d), y.dtype), pltpu.SemaphoreType.DMA, pltpu.SemaphoreType.DMA],
        compiler_params=_cparams("arbitrary"),
        name="combine",
    )(dest.reshape(n // tm, TOP_K * tm), y, x1, rf)


def _moe(x1, xn, logits, wg, wu, wd, layer):
    n, d = x1.shape
    bm = EXPERT_BLOCK
    ri, rf, cnt = _route(logits)
    counts = cnt[0, N_GROUPS:N_GROUPS + N_EXPERTS].astype(jnp.int32)
    psz = (counts + bm - 1) // bm * bm
    pend = jnp.cumsum(psz)
    pstart = pend - psz
    pstart_row = jnp.pad(pstart.astype(F32), (0, LANES - N_EXPERTS))[None, :]
    dest = _dest(ri, pstart_row)[:, :TOP_K]
    n_blocks = -(-(n * TOP_K) // bm) + N_EXPERTS
    blk_start = jnp.arange(n_blocks, dtype=jnp.int32) * bm
    blk_e = jnp.minimum(jnp.sum(pend[None, :] <= blk_start[:, None], axis=1), N_EXPERTS - 1).astype(jnp.int32)
    n_used = (pend[-1] // bm).astype(jnp.int32).reshape(1)
    buf = _dispatch(xn, dest, pend.astype(jnp.int32), n_blocks * bm)
    y = _experts(buf, blk_e, n_used, wg, wu, wd, layer)
    return _combine(x1, y, dest, rf)


def _prep_layer(l, P):
    d = P['w_in'].shape[1]
    cw = P['conv_dw_w'].shape[-1]
    mw = P['ml_norm_g'].shape[-1]
    nw = NA_HEADS * P['na_q_norm_g'].shape[-1]
    off_g = 2 * cw + 4 * mw
    ng = 4 * ML_HEADS
    w = P['w_in'][l]
    w_perm = jnp.concatenate([w[:, :off_g], w[:, off_g + ng:], w[:, off_g:off_g + ng],
                              jnp.zeros((d, LANES - ng), F32)], axis=1).astype(BF16)
    nhd = nw // NA_HEADS
    head_id = jnp.arange(nw) // nhd
    bd = jnp.where(head_id[:, None] == head_id[None, :], 1.0 / nhd, 0.0).astype(BF16)
    wr = jnp.concatenate([P['w_router_group'][l], P['w_router_expert'][l],
                          jnp.zeros((d, LANES - N_GROUPS - N_EXPERTS), F32)], axis=1)
    br = jnp.concatenate([P['b_router_group'][l], P['b_router_expert'][l],
                          jnp.zeros((LANES - N_GROUPS - N_EXPERTS,), F32)])[None, :]
    return dict(
        dims=(cw, mw, nw),
        norm_mix_g=P['norm_mix_g'][l][None, :], w_in=w_perm,
        gate_b=jnp.pad(P['ml_gate_b'][l].reshape(-1), (0, LANES - ng))[None, :],
        nq=jnp.tile(P['na_q_norm_g'][l], NA_HEADS)[None, :], nk=jnp.tile(P['na_k_norm_g'][l], NA_HEADS)[None, :],
        bd=bd,
        conv_dw_w=P['conv_dw_w'][l], conv_dw_b=P['conv_dw_b'][l][None, :], conv_norm_g=P['conv_norm_g'][l][None, :],
        ml_qk_conv_w=P['ml_qk_conv_w'][l], ml_qk_conv_b=P['ml_qk_conv_b'][l][None, :],
        ml_norm_g=P['ml_norm_g'][l][None, :], na_table=_na_bias_table(P['na_rpb'][l]),
        w_out=P['w_out'][l].astype(BF16), norm_ffn_g=P['norm_ffn_g'][l][None, :],
        wrh=wr.astype(BF16), wrl=(wr - wr.astype(BF16).astype(F32)).astype(BF16), br=br,
        wg=P['w_exp_gate'], wu=P['w_exp_up'], wd=P['w_exp_down'], layer=l,
    )


def _layer(x, B, T, p):
    dims = p['dims']
    u, qkpre, v, o, naq, nak, nav, gates = _in_proj(x, p['norm_mix_g'], p['w_in'], p['gate_b'], p['nq'], p['nk'],
                                                    p['bd'], dims)
    yc, hf, hb, yna = _mixers(B, T, u, qkpre, v, naq, nak, nav, gates, p)
    x1, xn, logits = _out_proj(yc, hf, hb, o, yna, x, p['ml_norm_g'], p['w_out'], p['norm_ffn_g'], p['wrh'], p['wrl'],
                               p['br'], dims)
    return _moe(x1, xn, logits, p['wg'], p['wu'], p['wd'], p['layer'])


def kernel(x_prompt, x_sample, norm_mix_g, w_in, conv_dw_w, conv_dw_b, conv_norm_g, ml_qk_conv_w, ml_qk_conv_b,
           ml_gate_b, ml_norm_g, na_q_norm_g, na_k_norm_g, na_rpb, w_out, norm_ffn_g, w_router_group,
           b_router_group, w_router_expert, b_router_expert, w_exp_gate, w_exp_up, w_exp_down):
    P = dict(norm_mix_g=norm_mix_g, w_in=w_in, conv_dw_w=conv_dw_w, conv_dw_b=conv_dw_b, conv_norm_g=conv_norm_g,
             ml_qk_conv_w=ml_qk_conv_w, ml_qk_conv_b=ml_qk_conv_b, ml_gate_b=ml_gate_b, ml_norm_g=ml_norm_g,
             na_q_norm_g=na_q_norm_g, na_k_norm_g=na_k_norm_g, na_rpb=na_rpb, w_out=w_out, norm_ffn_g=norm_ffn_g,
             w_router_group=w_router_group, b_router_group=b_router_group, w_router_expert=w_router_expert,
             b_router_expert=b_router_expert, w_exp_gate=w_exp_gate, w_exp_up=w_exp_up, w_exp_down=w_exp_down)
    depth = w_in.shape[0]
    d = x_prompt.shape[-1]
    groups = [(x_prompt.shape[0], x_prompt.shape[1], x_prompt.reshape(-1, d)),
              (x_sample.shape[0], x_sample.shape[1], x_sample.reshape(-1, d))]
    for l in range(depth):
        p = _prep_layer(l, P)
        groups = [(B, T, _layer(x, B, T, p)) for B, T, x in groups]
    return tuple(x.reshape(B, T, d) for B, T, x in groups)
```

```python
import functools

import numpy as np

import jax
import jax.numpy as jnp
from jax import lax
from jax.experimental import pallas as pl
from jax.experimental.pallas import tpu as pltpu

F32 = jnp.float32
BF16 = jnp.bfloat16

ML_HEADS = 4
NA_HEADS = 4
CONV_K = 31
ML_CHUNK = 128
GRID_W = 64
NA_WIN_ROWS = 8
NA_WIN_COLS = 16
N_GROUPS = 8
EXPERTS_PER_GROUP = 8
N_EXPERTS = N_GROUPS * EXPERTS_PER_GROUP
TOP_K = 2
EPS = 1e-6

LANES = 128
SUBLANES = 8
VMEM_LIMIT = 48 * 1024 * 1024

TOKEN_TILE = 512
EXPERT_BLOCK = 256


def _cparams(*sem):
    return pltpu.CompilerParams(dimension_semantics=sem, vmem_limit_bytes=VMEM_LIMIT)


def _full(shape):
    return pl.BlockSpec(shape, lambda *_: (0,) * len(shape))


def _rows(tm, width):
    return pl.BlockSpec((tm, width), lambda i: (i, 0))


ROW_TILE = SUBLANES


def _load_row_tiles(ref, r0, rows):
    return jnp.concatenate([ref[pl.ds(r0 * ROW_TILE + s, rows, stride=ROW_TILE), :] for s in range(ROW_TILE)],
                           axis=-1)


def _store_row_tiles(ref, r0, val):
    for s in range(ROW_TILE):
        ref[pl.ds(r0 * ROW_TILE + s, val.shape[0], stride=ROW_TILE), :] = val[:, s * LANES:(s + 1) * LANES]


def _group_mean(zz, bd):
    hi = zz.astype(BF16)
    lo = (zz - hi.astype(F32)).astype(BF16)
    return (jnp.dot(hi, bd, preferred_element_type=F32) + jnp.dot(lo, bd, preferred_element_type=F32))


def _in_proj_kernel(x_ref, g_ref, w_ref, gb_ref, nq_ref, nk_ref, bd_ref,
                    u_ref, qk_ref, v_ref, o_ref, naq_ref, nak_ref, nav_ref, gt_ref, *, dims):
    cw, mw, nw = dims
    x = x_ref[...]
    ms = jnp.mean(x * x, axis=-1, keepdims=True)
    xn = (x * lax.rsqrt(ms + EPS) * g_ref[...]).astype(BF16)

    def proj(lo, width):
        return jnp.dot(xn, w_ref[:, lo:lo + width], preferred_element_type=F32)

    zc = proj(0, 2 * cw)
    u_ref[...] = zc[:, :cw] * jax.nn.sigmoid(zc[:, cw:])
    off = 2 * cw
    qk_ref[...] = proj(off, 2 * mw)
    v_ref[...] = proj(off + 2 * mw, mw).astype(BF16)
    o_ref[...] = proj(off + 3 * mw, mw)
    off += 4 * mw
    bd = bd_ref[...]
    zq = proj(off, nw)
    zk = proj(off + nw, nw)
    hd = nw // NA_HEADS
    naq_ref[...] = (zq * lax.rsqrt(_group_mean(zq * zq, bd) + EPS) * nq_ref[...] * (hd ** -0.5)).astype(BF16)
    nak_ref[...] = (zk * lax.rsqrt(_group_mean(zk * zk, bd) + EPS) * nk_ref[...]).astype(BF16)
    nav_ref[...] = proj(off + 2 * nw, nw).astype(BF16)
    off += 3 * nw
    gt_ref[...] = proj(off, LANES) + gb_ref[...]


def _in_proj(x, g, w, gate_b, nq, nk, bd, dims):
    n, d = x.shape
    cw, mw, nw = dims
    tm = min(TOKEN_TILE, n)
    outs = [(cw, F32), (2 * mw, F32), (mw, BF16), (mw, F32), (nw, BF16), (nw, BF16), (nw, BF16), (LANES, F32)]
    return pl.pallas_call(
        functools.partial(_in_proj_kernel, dims=dims),
        grid=(n // tm,),
        in_specs=[_rows(tm, d), _full(g.shape), _full(w.shape), _full(gate_b.shape),
                  _full(nq.shape), _full(nk.shape), _full(bd.shape)],
        out_specs=[_rows(tm, wd) for wd, _ in outs],
        out_shape=[jax.ShapeDtypeStruct((n, wd), dt) for wd, dt in outs],
        compiler_params=_cparams("parallel"),
        name="in_proj",
    )(x, g, w, gate_b, nq, nk, bd)


OUT_SUB = 128


def _out_proj_kernel(yc_ref, hf_ref, hb_ref, o_ref, yna_ref, x_ref, mg_ref, w_ref, fg_ref, wrh_ref, wrl_ref, br_ref,
                     x1_ref, xn_ref, lg_ref, *, dims):
    cw, mw, nw = dims
    hd = mw // ML_HEADS
    for r0 in range(0, x_ref.shape[0], OUT_SUB):
        rs = slice(r0, r0 + OUT_SUB)
        hm = hf_ref[rs, :] + hb_ref[rs, :]
        parts = []
        for h in range(ML_HEADS):
            seg = hm[:, h * hd:(h + 1) * hd]
            parts.append(seg * lax.rsqrt(jnp.mean(seg * seg, axis=-1, keepdims=True) + EPS))
        yml = jnp.concatenate(parts, axis=-1) * mg_ref[...] * jax.nn.sigmoid(o_ref[rs, :])
        y = jnp.dot(yc_ref[rs, :], w_ref[0:cw, :], preferred_element_type=F32)
        y += jnp.dot(yml.astype(BF16), w_ref[cw:cw + mw, :], preferred_element_type=F32)
        y += jnp.dot(yna_ref[rs, :], w_ref[cw + mw:cw + mw + nw, :], preferred_element_type=F32)
        x1 = x_ref[rs, :] + y
        x1_ref[rs, :] = x1
        xn = x1 * lax.rsqrt(jnp.mean(x1 * x1, axis=-1, keepdims=True) + EPS) * fg_ref[...]
        _store_row_tiles(xn_ref, r0, xn)
        xh = xn.astype(BF16)
        xl = (xn - xh.astype(F32)).astype(BF16)
        lg = jnp.dot(xh, wrh_ref[...], preferred_element_type=F32)
        lg += jnp.dot(xl, wrh_ref[...], preferred_element_type=F32)
        lg += jnp.dot(xh, wrl_ref[...], preferred_element_type=F32)
        lg_ref[rs, :] = lg + br_ref[...]


def _out_proj(yc, hf, hb, o, yna, x, mg, w, fg, wrh, wrl, br, dims):
    n, d = x.shape
    cw, mw, nw = dims
    tm = min(TOKEN_TILE, n)
    assert tm % OUT_SUB == 0
    return pl.pallas_call(
        functools.partial(_out_proj_kernel, dims=dims),
        grid=(n // tm,),
        in_specs=[_rows(tm, cw), _rows(tm, mw), _rows(tm, mw), _rows(tm, mw), _rows(tm, nw), _rows(tm, d),
                  _full(mg.shape), _full(w.shape), _full(fg.shape), _full(wrh.shape), _full(wrl.shape),
                  _full(br.shape)],
        out_specs=[_rows(tm, d), _rows(tm * ROW_TILE, LANES), _rows(tm, LANES)],
        out_shape=[jax.ShapeDtypeStruct((n, d), F32), jax.ShapeDtypeStruct((n * ROW_TILE, LANES), F32),
                   jax.ShapeDtypeStruct((n, LANES), F32)],
        compiler_params=_cparams("parallel"),
        name="out_proj",
    )(yc, hf, hb, o, yna, x, mg, w, fg, wrh, wrl, br)


def _expert_kernel(be_ref, nb_ref, x_ref, wg_ref, wu_ref, wd_ref, y_ref, wgb_ref, wub_ref, wdb_ref):
    i = pl.program_id(0)

    @pl.when((i == 0) | (be_ref[i] != be_ref[jnp.maximum(i - 1, 0)]))
    def _():
        wgb_ref[...] = wg_ref[0].astype(BF16)
        wub_ref[...] = wu_ref[0].astype(BF16)
        wdb_ref[...] = wd_ref[0].astype(BF16)

    @pl.when(i < nb_ref[0])
    def _():
        bm = x_ref.shape[0] // ROW_TILE
        xb = _load_row_tiles(x_ref, 0, bm).astype(BF16)
        g = jnp.dot(xb, wgb_ref[...], preferred_element_type=F32)
        u = jnp.dot(xb, wub_ref[...], preferred_element_type=F32)
        h = (g * jax.nn.sigmoid(g) * u).astype(BF16)
        _store_row_tiles(y_ref, 0, jnp.dot(h, wdb_ref[...], preferred_element_type=F32))

    @pl.when(i >= nb_ref[0])
    def _():
        y_ref[...] = jnp.zeros_like(y_ref)


def _experts(buf, blk_e, n_used, wg, wu, wd, layer):
    d, de = wg.shape[-2:]
    assert d == ROW_TILE * LANES
    bm = EXPERT_BLOCK
    tb = bm * ROW_TILE
    grid_spec = pltpu.PrefetchScalarGridSpec(
        num_scalar_prefetch=2,
        grid=(buf.shape[0] // tb,),
        in_specs=[pl.BlockSpec((tb, LANES), lambda i, be, nb: (jnp.minimum(i, nb[0] - 1), 0)),
                  pl.BlockSpec((None, 1, d, de), lambda i, be, nb: (layer, be[i], 0, 0)),
                  pl.BlockSpec((None, 1, d, de), lambda i, be, nb: (layer, be[i], 0, 0)),
                  pl.BlockSpec((None, 1, de, d), lambda i, be, nb: (layer, be[i], 0, 0))],
        out_specs=pl.BlockSpec((tb, LANES), lambda i, be, nb: (i, 0)),
        scratch_shapes=[pltpu.VMEM((d, de), BF16), pltpu.VMEM((d, de), BF16), pltpu.VMEM((de, d), BF16)],
    )
    return pl.pallas_call(
        _expert_kernel,
        grid_spec=grid_spec,
        out_shape=jax.ShapeDtypeStruct(buf.shape, F32),
        compiler_params=_cparams("arbitrary"),
        name="experts",
    )(blk_e, n_used, buf, wg, wu, wd)


NA_QROWS = 4
NEG_BIG = -1e30


def _bias_expand_kernel(rpb_ref, selr_ref, selc_ref, o_ref):
    rows = jnp.dot(selr_ref[...], rpb_ref[0], preferred_element_type=F32, precision=lax.Precision.HIGHEST)
    o_ref[0] = jnp.dot(rows, selc_ref[...], preferred_element_type=F32, precision=lax.Precision.HIGHEST)


def _na_bias_table(rpb):
    nh = rpb.shape[0]
    a = np.arange(NA_QROWS)[:, None]
    i = np.arange(3 * NA_QROWS)[None, :]
    half = NA_WIN_ROWS // 2
    a0 = 0 * a
    row_ok = np.stack([(i >= a0 + NA_QROWS) & (i < a0 + NA_QROWS + NA_WIN_ROWS),
                       (i >= a + NA_QROWS - half) & (i < a + NA_QROWS - half + NA_WIN_ROWS),
                       (i >= a0 + 2 * NA_QROWS - NA_WIN_ROWS) & (i < a0 + 2 * NA_QROWS)])
    ro = np.clip(i - NA_QROWS - a + (NA_WIN_ROWS - 1), 0, 2 * NA_WIN_ROWS - 2)
    c = np.arange(GRID_W)
    cs = np.clip(c - NA_WIN_COLS // 2, 0, GRID_W - NA_WIN_COLS)
    col_ok = (c[None, :] >= cs[:, None]) & (c[None, :] < cs[:, None] + NA_WIN_COLS)
    co = np.clip(c[None, :] - c[:, None] + (NA_WIN_COLS - 1), 0, 2 * NA_WIN_COLS - 2)
    nr, nc = 2 * SUBLANES, LANES
    sel_r = (ro.reshape(-1)[:, None] == np.arange(nr)[None, :]).astype(np.float32)
    sel_c = (np.arange(nc)[:, None] == co.reshape(-1)[None, :]).astype(np.float32)
    rpb_p = jnp.pad(rpb.astype(F32), ((0, 0), (0, nr - rpb.shape[1]), (0, nc - rpb.shape[2])))
    nx, ny = sel_r.shape[0], sel_c.shape[1]
    bias = pl.pallas_call(
        _bias_expand_kernel,
        grid=(nh,),
        in_specs=[pl.BlockSpec((1, nr, nc), lambda h: (h, 0, 0)), _full(sel_r.shape), _full(sel_c.shape)],
        out_specs=pl.BlockSpec((1, nx, ny), lambda h: (h, 0, 0)),
        out_shape=jax.ShapeDtypeStruct((nh, nx, ny), F32),
        compiler_params=_cparams("parallel"),
        name="na_bias_expand",
    )(rpb_p, jnp.asarray(sel_r), jnp.asarray(sel_c))
    bias = bias.reshape(nh, NA_QROWS, 3 * NA_QROWS, GRID_W, GRID_W).transpose(0, 1, 3, 2, 4)
    ok = row_ok[:, None, :, None, :, None] & col_ok[None, None, None, :, None, :]
    tab = jnp.where(jnp.asarray(ok), bias[None], NEG_BIG)
    return tab.reshape(3, nh, NA_QROWS * GRID_W, 3 * NA_QROWS * GRID_W)


def _na_kernel(q_ref, kp_ref, kc_ref, kn_ref, vp_ref, vc_ref, vn_ref, b_ref, o_ref):
    q = q_ref[...]
    kcat = jnp.concatenate([kp_ref[...], kc_ref[...], kn_ref[...]], axis=0)
    vcat = jnp.concatenate([vp_ref[...], vc_ref[...], vn_ref[...]], axis=0)
    nw = q.shape[-1]
    hd = nw // NA_HEADS
    lane_head = lax.broadcasted_iota(jnp.int32, q.shape, 1) // hd
    acc = jnp.zeros(q.shape, F32)
    for h in range(NA_HEADS):
        sel = lane_head == h
        qm = jnp.where(sel, q, jnp.zeros_like(q))
        s = lax.dot_general(qm, kcat, (((1,), (1,)), ((), ())), preferred_element_type=F32) + b_ref[0, h]
        m = jnp.max(s, axis=-1, keepdims=True)
        p = jnp.exp(s - m)
        l = jnp.sum(p, axis=-1, keepdims=True)
        oh = jnp.dot(p.astype(BF16), vcat, preferred_element_type=F32)
        acc = jnp.where(sel, oh / l, acc)
    o_ref[...] = acc.astype(o_ref.dtype)


def _neigh_attn_pallas(q, k, v, table, B, T):
    n, nw = q.shape
    tq = NA_QROWS * GRID_W
    J = T // tq
    assert T % tq == 0 and J >= 2 and T // GRID_W >= NA_WIN_ROWS
    blk = (tq, nw)
    cur = pl.BlockSpec(blk, lambda b, j: (b * J + j, 0))
    prev = pl.BlockSpec(blk, lambda b, j: (b * J + jnp.maximum(j - 1, 0), 0))
    nxt = pl.BlockSpec(blk, lambda b, j: (b * J + jnp.minimum(j + 1, J - 1), 0))
    tab = pl.BlockSpec((1,) + table.shape[1:],
                       lambda b, j: (jnp.where(j == 0, 0, jnp.where(j == J - 1, 2, 1)), 0, 0, 0))
    return pl.pallas_call(
        _na_kernel,
        grid=(B, J),
        in_specs=[cur, prev, cur, nxt, prev, cur, nxt, tab],
        out_specs=cur,
        out_shape=jax.ShapeDtypeStruct((n, nw), BF16),
        compiler_params=_cparams("parallel", "arbitrary"),
        name="neigh_attn",
    )(q, k, k, k, v, v, v, table)


HALO = 16
CONV_ROWS = 64


def _local_conv_kernel(u_ref, up_ref, un_ref, qk_ref, qkp_ref, qkn_ref, cw_ref, cb_ref, cg_ref, mw_ref, mb_ref,
                       yc_ref, q_ref, k_ref, wu_ref, wqk_ref, sh_ref, *, k_scale):
    t = pl.program_id(1)
    nt = pl.num_programs(1)
    tm = u_ref.shape[0]
    keep_p = (t > 0).astype(F32)
    keep_n = (t < nt - 1).astype(F32)
    wu_ref[0:HALO, :] = up_ref[...] * keep_p
    wu_ref[HALO:HALO + tm, :] = u_ref[...]
    wu_ref[HALO + tm:, :] = un_ref[...] * keep_n
    wqk_ref[0:HALO, :] = qkp_ref[...] * keep_p
    wqk_ref[HALO:HALO + tm, :] = qk_ref[...]
    wqk_ref[HALO + tm:, :] = qkn_ref[...] * keep_n
    kc = cw_ref.shape[0]
    km = mw_ref.shape[0]
    mw2 = q_ref.shape[-1]
    nsh = sh_ref.shape[1]
    for ph in range(1, SUBLANES):
        sh_ref[ph - 1] = wu_ref[ph:ph + nsh, :]
    for r0 in range(0, tm, CONV_ROWS):
        acc = jnp.zeros((CONV_ROWS, u_ref.shape[-1]), F32) + cb_ref[...]
        for k in range(kc):
            s = r0 + HALO - kc // 2 + k
            ph = s % SUBLANES
            tap = wu_ref[s:s + CONV_ROWS, :] if ph == 0 else sh_ref[ph - 1, s - ph:s - ph + CONV_ROWS, :]
            acc = acc + tap * cw_ref[k:k + 1, :]
        y = acc * lax.rsqrt(jnp.mean(acc * acc, axis=-1, keepdims=True) + EPS) * cg_ref[...]
        yc_ref[r0:r0 + CONV_ROWS, :] = (y * jax.nn.sigmoid(y)).astype(yc_ref.dtype)
        acc = jnp.zeros((CONV_ROWS, qk_ref.shape[-1]), F32) + mb_ref[...]
        for k in range(km):
            s = r0 + HALO - km // 2 + k
            acc = acc + wqk_ref[s:s + CONV_ROWS, :] * mw_ref[k:k + 1, :]
        y = acc * jax.nn.sigmoid(acc)
        q_ref[r0:r0 + CONV_ROWS, :] = y[:, :mw2].astype(q_ref.dtype)
        k_ref[r0:r0 + CONV_ROWS, :] = (y[:, mw2:] * k_scale).astype(k_ref.dtype)


def _local_conv(u, qkpre, cw, cb, cg, mw, mb, B, T):
    n, cwid = u.shape
    qw = qkpre.shape[-1]
    tm = min(TOKEN_TILE, T)
    nt = T // tm
    hb = tm // HALO
    nhb = n // HALO
    assert T % tm == 0 and tm % CONV_ROWS == 0 and CONV_K // 2 <= HALO

    def cur(w):
        return pl.BlockSpec((tm, w), lambda b, t: (b * nt + t, 0))

    def prev(w):
        return pl.BlockSpec((HALO, w), lambda b, t: (jnp.maximum((b * nt + t) * hb - 1, 0), 0))

    def nxt(w):
        return pl.BlockSpec((HALO, w), lambda b, t: (jnp.minimum((b * nt + t + 1) * hb, nhb - 1), 0))

    hd = (qw // 2) // ML_HEADS
    return pl.pallas_call(
        functools.partial(_local_conv_kernel, k_scale=hd ** -0.5),
        grid=(B, nt),
        in_specs=[cur(cwid), prev(cwid), nxt(cwid), cur(qw), prev(qw), nxt(qw),
                  _full(cw.shape), _full(cb.shape), _full(cg.shape), _full(mw.shape), _full(mb.shape)],
        out_specs=[cur(cwid), cur(qw // 2), cur(qw // 2)],
        out_shape=[jax.ShapeDtypeStruct((n, cwid), BF16), jax.ShapeDtypeStruct((n, qw // 2), BF16),
                   jax.ShapeDtypeStruct((n, qw // 2), BF16)],
        scratch_shapes=[pltpu.VMEM((tm + 2 * HALO, cwid), F32), pltpu.VMEM((tm + 2 * HALO, qw), F32),
                        pltpu.VMEM((SUBLANES - 1, tm + 2 * HALO - SUBLANES, cwid), F32)],
        compiler_params=_cparams("parallel", "parallel"),
        name="local_conv",
    )(u, u, u, qkpre, qkpre, qkpre, cw, cb, cg, mw, mb)


def _scan_rows(x, reverse, combine, fill):
    L = x.shape[0]
    row = lax.broadcasted_iota(jnp.int32, x.shape, 0)
    s = 1
    while s < L:
        if reverse:
            x = combine(x, jnp.where(row < L - s, pltpu.roll(x, L - s, axis=0), fill))
        else:
            x = combine(x, jnp.where(row >= s, pltpu.roll(x, s, axis=0), fill))
        s *= 2
    return x


def _mlstm_direction(q_ref, k_ref, v_ref, g_ref, h_ref, c_ref, m_ref, reverse):
    L = q_ref.shape[0]
    hd = q_ref.shape[-1] // ML_HEADS
    ci = 2 * ML_HEADS if reverse else 0
    cf = ci + ML_HEADS
    g = g_ref[...]
    gi = pltpu.roll(g, ML_HEADS, axis=1)
    lf = jnp.minimum(g, 0.0) - jnp.log1p(jnp.exp(-jnp.abs(g)))
    b = _scan_rows(lf, reverse, jnp.add, 0.0)
    r = gi - b
    cm = _scan_rows(r, reverse, jnp.maximum, NEG_BIG)
    total = b[0:1, :] if reverse else b[L - 1:L, :]
    m_prev = m_ref[...]
    inter = b + m_prev
    m_j = jnp.maximum(inter, b + cm)
    a_t = b - m_j
    e_t = jnp.exp(inter - m_j)
    z_t = jnp.exp(-m_j)
    w = total - b + gi
    m_loc = jnp.max(w, axis=0, keepdims=True)
    wa_t = jnp.exp(w - m_loc)
    m_new = jnp.maximum(total + m_prev, m_loc)
    dec = jnp.exp(total + m_prev - m_new)
    inc = jnp.exp(m_loc - m_new)
    m_ref[...] = m_new
    rt = jnp.transpose(r)
    jj = lax.broadcasted_iota(jnp.int32, (L, L), 0)
    ss = lax.broadcasted_iota(jnp.int32, (L, L), 1)
    visible = (ss >= jj) if reverse else (ss <= jj)
    ones = jnp.ones((L, hd), BF16)
    for h in range(ML_HEADS):
        sl = slice(h * hd, (h + 1) * hd)
        col = slice(cf + h, cf + h + 1)
        qh = q_ref[:, sl]
        kh = k_ref[:, sl]
        v_ext = jnp.concatenate([v_ref[:, sl], ones], axis=-1)
        c_prev = c_ref[h]
        pm = jnp.exp(jnp.where(visible, a_t[:, col] + rt[col, :], NEG_BIG))
        qk = lax.dot_general(qh, kh, (((1,), (1,)), ((), ())), preferred_element_type=F32) * pm
        tot = (jnp.dot(qk.astype(BF16), v_ext, preferred_element_type=F32)
               + e_t[:, col] * jnp.dot(qh, c_prev.astype(BF16), preferred_element_type=F32))
        h_ref[:, sl] = tot[:, :hd] / jnp.maximum(jnp.abs(tot[:, hd:]), z_t[:, col])
        ka = (kh.astype(F32) * wa_t[:, col]).astype(BF16)
        s_ext = lax.dot_general(ka, v_ext, (((0,), (0,)), ((), ())), preferred_element_type=F32)
        c_ref[h] = dec[:, col] * c_prev + inc[:, col] * s_ext


def _mlstm_kernel(qf_ref, kf_ref, vf_ref, gf_ref, qb_ref, kb_ref, vb_ref, gb_ref, hf_ref, hb_ref,
                  cf_ref, cb_ref, mf_ref, mb_ref):
    @pl.when(pl.program_id(1) == 0)
    def _():
        cf_ref[...] = jnp.zeros_like(cf_ref)
        cb_ref[...] = jnp.zeros_like(cb_ref)
        mf_ref[...] = jnp.zeros_like(mf_ref)
        mb_ref[...] = jnp.zeros_like(mb_ref)

    _mlstm_direction(qf_ref, kf_ref, vf_ref, gf_ref, hf_ref, cf_ref, mf_ref, False)
    _mlstm_direction(qb_ref, kb_ref, vb_ref, gb_ref, hb_ref, cb_ref, mb_ref, True)


def _mlstm(q, k, v, gates, B, T):
    n, mw = q.shape
    L = ML_CHUNK
    nc = T // L
    hd = mw // ML_HEADS
    assert T % L == 0 and hd == LANES

    def fwd(w):
        return pl.BlockSpec((L, w), lambda b, c: (b * nc + c, 0))

    def bwd(w):
        return pl.BlockSpec((L, w), lambda b, c: (b * nc + nc - 1 - c, 0))

    return pl.pallas_call(
        _mlstm_kernel,
        grid=(B, nc),
        in_specs=[fwd(mw), fwd(mw), fwd(mw), fwd(LANES), bwd(mw), bwd(mw), bwd(mw), bwd(LANES)],
        out_specs=[fwd(mw), bwd(mw)],
        out_shape=[jax.ShapeDtypeStruct((n, mw), F32), jax.ShapeDtypeStruct((n, mw), F32)],
        scratch_shapes=[pltpu.VMEM((ML_HEADS, hd, 2 * hd), F32), pltpu.VMEM((ML_HEADS, hd, 2 * hd), F32),
                        pltpu.VMEM((1, LANES), F32), pltpu.VMEM((1, LANES), F32)],
        compiler_params=_cparams("parallel", "arbitrary"),
        name="mlstm",
    )(q, k, v, gates, q, k, v, gates)


def _mixers(B, T, u, qkpre, v, naq, nak, nav, gates, p):
    yc, q, k = _local_conv(u, qkpre, p['conv_dw_w'], p['conv_dw_b'], p['conv_norm_g'],
                           p['ml_qk_conv_w'], p['ml_qk_conv_b'], B, T)
    hf, hb = _mlstm(q, k, v, gates, B, T)
    y_na = _neigh_attn_pallas(naq, nak, nav, p['na_table'], B, T)
    return yc, hf, hb, y_na


def _first_lane(mask, lane):
    return jnp.min(jnp.where(mask, lane, LANES), axis=-1, keepdims=True)


def _route_kernel(lg_ref, ri_ref, rf_ref, cnt_ref, carry_ref):
    @pl.when(pl.program_id(0) == 0)
    def _():
        carry_ref[...] = jnp.zeros_like(carry_ref)

    lg = lg_ref[...]
    tm = lg.shape[0]
    lane = lax.broadcasted_iota(jnp.int32, lg.shape, 1)
    gl = jnp.where(lane < N_GROUPS, lg, NEG_BIG)
    gmax = jnp.max(gl, axis=-1, keepdims=True)
    pg = 1.0 / jnp.sum(jnp.exp(gl - gmax), axis=-1, keepdims=True)
    gsel = _first_lane(gl == gmax, lane)
    lo = N_GROUPS + gsel * EXPERTS_PER_GROUP
    in_grp = (lane >= lo) & (lane < lo + EXPERTS_PER_GROUP)
    el = jnp.where(in_grp, lg, NEG_BIG)
    emax = jnp.max(el, axis=-1, keepdims=True)
    esum = jnp.sum(jnp.exp(el - emax), axis=-1, keepdims=True)
    idx1 = _first_lane(in_grp & (el == emax), lane)
    el2 = jnp.where(lane == idx1, NEG_BIG, el)
    emax2 = jnp.max(el2, axis=-1, keepdims=True)
    idx2 = _first_lane(in_grp & (el2 == emax2) & (lane != idx1), lane)
    p1 = 1.0 / esum
    p2 = jnp.exp(emax2 - emax) / esum
    g1 = pg * p1 / (p1 + p2)
    g2 = pg * p2 / (p1 + p2)
    hit1 = lane == idx1
    hit2 = lane == idx2
    onehot = jnp.where(hit1 | hit2, 1.0, 0.0)
    r = lax.broadcasted_iota(jnp.int32, (tm, tm), 0)
    c = lax.broadcasted_iota(jnp.int32, (tm, tm), 1)
    tri = jnp.where(r > c, 1.0, 0.0).astype(BF16)
    prefix = jnp.dot(tri, onehot.astype(BF16), preferred_element_type=F32) + carry_ref[...]
    rank1 = jnp.sum(jnp.where(hit1, prefix, 0.0), axis=-1, keepdims=True).astype(jnp.int32)
    rank2 = jnp.sum(jnp.where(hit2, prefix, 0.0), axis=-1, keepdims=True).astype(jnp.int32)
    carry = carry_ref[...] + jnp.sum(onehot, axis=0, keepdims=True)
    carry_ref[...] = carry
    cnt_ref[...] = carry
    ri_ref[...] = jnp.where(lane == 0, idx1 - N_GROUPS, jnp.where(lane == 1, idx2 - N_GROUPS,
                            jnp.where(lane == 2, rank1, jnp.where(lane == 3, rank2, 0))))
    rf_ref[...] = jnp.where(lane == 0, g1, jnp.where(lane == 1, g2, 0.0))


def _dest_kernel(ri_ref, ps_ref, d_ref):
    ri = ri_ref[...]
    lane = lax.broadcasted_iota(jnp.int32, ri.shape, 1)
    ps = ps_ref[...]
    out = jnp.zeros(ri.shape, jnp.int32)
    for k in range(TOP_K):
        start = jnp.sum(jnp.where(lane == ri[:, k:k + 1], ps, 0.0), axis=-1, keepdims=True).astype(jnp.int32)
        out = jnp.where(lane == k, start + ri[:, TOP_K + k:TOP_K + k + 1], out)
    d_ref[...] = out


def _dest(ri, pstart_row):
    n = ri.shape[0]
    tm = min(4 * TOKEN_TILE, n)
    assert n % tm == 0
    return pl.pallas_call(
        _dest_kernel,
        grid=(n // tm,),
        in_specs=[_rows(tm, LANES), _full((1, LANES))],
        out_specs=_rows(tm, LANES),
        out_shape=jax.ShapeDtypeStruct((n, LANES), jnp.int32),
        compiler_params=_cparams("parallel"),
        name="dest",
    )(ri, pstart_row)


def _route(logits):
    n = logits.shape[0]
    tm = min(TOKEN_TILE, n)
    return pl.pallas_call(
        _route_kernel,
        grid=(n // tm,),
        in_specs=[_rows(tm, LANES)],
        out_specs=[_rows(tm, LANES), _rows(tm, LANES), _full((1, LANES))],
        out_shape=[jax.ShapeDtypeStruct((n, LANES), jnp.int32), jax.ShapeDtypeStruct((n, LANES), F32),
                   jax.ShapeDtypeStruct((1, LANES), F32)],
        scratch_shapes=[pltpu.VMEM((1, LANES), F32)],
        compiler_params=_cparams("arbitrary"),
        name="route",
    )(logits)


ROW_UNROLL = 8


def _rows_of(ref, first, count):
    return ref.at[pl.ds(pl.multiple_of(first * ROW_TILE, ROW_TILE), count * ROW_TILE), :]


def _row_copy(src, i, dst, j, sem):
    return pltpu.make_async_copy(_rows_of(src, i, 1), _rows_of(dst, j, 1), sem)


def _dispatch_kernel(dest_hbm, pend_ref, x_ref, buf_hbm, dest_smem, zero_ref, sem_idx, sem_rows):
    i = pl.program_id(0)
    tm = x_ref.shape[0] // ROW_TILE
    bm = zero_ref.shape[0] // ROW_TILE

    @pl.when(i == 0)
    def _():
        zero_ref[...] = jnp.zeros_like(zero_ref)

        def last_block(e):
            start = pl.multiple_of(jnp.maximum(pend_ref[e] - bm, 0), bm)
            return pltpu.make_async_copy(zero_ref, _rows_of(buf_hbm, start, bm), sem_rows)

        for e in range(N_EXPERTS):
            last_block(e).start()
        for e in range(N_EXPERTS):
            last_block(e).wait()

        def unused_block(blk, carry):
            copy = pltpu.make_async_copy(zero_ref, _rows_of(buf_hbm, pl.multiple_of(blk * bm, bm), bm), sem_rows)
            copy.start()
            copy.wait()
            return carry

        lax.fori_loop(pend_ref[N_EXPERTS - 1] // bm, buf_hbm.shape[0] // (bm * ROW_TILE), unused_block, 0)

    idx_copy = pltpu.make_async_copy(dest_hbm.at[i], dest_smem, sem_idx)
    idx_copy.start()
    idx_copy.wait()

    def body(o, carry):
        for u in range(ROW_UNROLL):
            r = o * ROW_UNROLL + u
            for k in range(TOP_K):
                _row_copy(x_ref, r, buf_hbm, dest_smem[TOP_K * r + k], sem_rows).start()
        return carry

    lax.fori_loop(0, tm // ROW_UNROLL, body, 0)
    for k in range(TOP_K):
        pltpu.make_async_copy(x_ref, _rows_of(buf_hbm, 0, tm), sem_rows).wait()


def _dispatch(xn, dest, pend, n_rows):
    n = xn.shape[0] // ROW_TILE
    tm = min(TOKEN_TILE, n)
    return pl.pallas_call(
        _dispatch_kernel,
        grid=(n // tm,),
        in_specs=[pl.BlockSpec(memory_space=pl.ANY), pl.BlockSpec(memory_space=pltpu.SMEM),
                  _rows(tm * ROW_TILE, LANES)],
        out_specs=pl.BlockSpec(memory_space=pl.ANY),
        out_shape=jax.ShapeDtypeStruct((n_rows * ROW_TILE, LANES), xn.dtype),
        scratch_shapes=[pltpu.SMEM((TOP_K * tm,), jnp.int32), pltpu.VMEM((EXPERT_BLOCK * ROW_TILE, LANES), xn.dtype),
                        pltpu.SemaphoreType.DMA, pltpu.SemaphoreType.DMA],
        compiler_params=_cparams("arbitrary"),
        name="dispatch",
    )(dest.reshape(n // tm, TOP_K * tm), pend, xn)


def _combine_kernel(dest_hbm, y_hbm, x_ref, rf_ref, o_ref, dest_smem, y0_ref, y1_ref, sem_idx, sem_rows):
    i = pl.program_id(0)
    tm = x_ref.shape[0]
    idx_copy = pltpu.make_async_copy(dest_hbm.at[i], dest_smem, sem_idx)
    idx_copy.start()
    idx_copy.wait()
    ybufs = (y0_ref, y1_ref)

    def body(o, carry):
        for u in range(ROW_UNROLL):
            r = o * ROW_UNROLL + u
            for k in range(TOP_K):
                _row_copy(y_hbm, dest_smem[TOP_K * r + k], ybufs[k], r, sem_rows).start()
        return carry

    lax.fori_loop(0, tm // ROW_UNROLL, body, 0)
    for k in range(TOP_K):
        pltpu.make_async_copy(_rows_of(y_hbm, 0, tm), ybufs[k], sem_rows).wait()
    rf = rf_ref[...]
    o_ref[...] = x_ref[...] + (_load_row_tiles(y0_ref, 0, tm) * rf[:, 0:1] + _load_row_tiles(y1_ref, 0, tm) * rf[:, 1:2])


def _combine(x1, y, dest, rf):
    n, d = x1.shape
    tm = min(TOKEN_TILE, n)
    return pl.pallas_call(
        _combine_kernel,
        grid=(n // tm,),
        in_specs=[pl.BlockSpec(memory_space=pl.ANY), pl.BlockSpec(memory_space=pl.ANY), _rows(tm, d),
                  _rows(tm, LANES)],
        out_specs=_rows(tm, d),
        out_shape=jax.ShapeDtypeStruct((n, d), x1.dtype),
        scratch_shapes=[pltpu.SMEM((TOP_K * tm,), jnp.int32), pltpu.VMEM((tm * ROW_TILE, LANES), y.dtype),
                        pltpu.VMEM((tm * ROW_TILE, LANES), y.dtype), pltpu.SemaphoreType.DMA,
                        pltpu.SemaphoreType.DMA],
        compiler_params=_cparams("arbitrary"),
        name="combine",
    )(dest.reshape(n // tm, TOP_K * tm), y, x1, rf)


def _moe(x1, xn, logits, wg, wu, wd, layer):
    n, d = x1.shape
    bm = EXPERT_BLOCK
    ri, rf, cnt = _route(logits)
    counts = cnt[0, N_GROUPS:N_GROUPS + N_EXPERTS].astype(jnp.int32)
    psz = (counts + bm - 1) // bm * bm
    pend = jnp.cumsum(psz)
    pstart = pend - psz
    pstart_row = jnp.pad(pstart.astype(F32), (0, LANES - N_EXPERTS))[None, :]
    dest = _dest(ri, pstart_row)[:, :TOP_K]
    n_blocks = -(-(n * TOP_K) // bm) + N_EXPERTS
    blk_start = jnp.arange(n_blocks, dtype=jnp.int32) * bm
    blk_e = jnp.minimum(jnp.sum(pend[None, :] <= blk_start[:, None], axis=1), N_EXPERTS - 1).astype(jnp.int32)
    n_used = (pend[-1] // bm).astype(jnp.int32).reshape(1)
    buf = _dispatch(xn, dest, pend.astype(jnp.int32), n_blocks * bm)
    y = _experts(buf, blk_e, n_used, wg, wu, wd, layer)
    return _combine(x1, y, dest, rf)


def _prep_layer(l, P):
    d = P['w_in'].shape[1]
    cw = P['conv_dw_w'].shape[-1]
    mw = P['ml_norm_g'].shape[-1]
    nw = NA_HEADS * P['na_q_norm_g'].shape[-1]
    off_g = 2 * cw + 4 * mw
    ng = 4 * ML_HEADS
    w = P['w_in'][l]
    w_perm = jnp.concatenate([w[:, :off_g], w[:, off_g + ng:], w[:, off_g:off_g + ng],
                              jnp.zeros((d, LANES - ng), F32)], axis=1).astype(BF16)
    nhd = nw // NA_HEADS
    head_id = jnp.arange(nw) // nhd
    bd = jnp.where(head_id[:, None] == head_id[None, :], 1.0 / nhd, 0.0).astype(BF16)
    wr = jnp.concatenate([P['w_router_group'][l], P['w_router_expert'][l],
                          jnp.zeros((d, LANES - N_GROUPS - N_EXPERTS), F32)], axis=1)
    br = jnp.concatenate([P['b_router_group'][l], P['b_router_expert'][l],
                          jnp.zeros((LANES - N_GROUPS - N_EXPERTS,), F32)])[None, :]
    return dict(
        dims=(cw, mw, nw),
        norm_mix_g=P['norm_mix_g'][l][None, :], w_in=w_perm,
        gate_b=jnp.pad(P['ml_gate_b'][l].reshape(-1), (0, LANES - ng))[None, :],
        nq=jnp.tile(P['na_q_norm_g'][l], NA_HEADS)[None, :], nk=jnp.tile(P['na_k_norm_g'][l], NA_HEADS)[None, :],
        bd=bd,
        conv_dw_w=P['conv_dw_w'][l], conv_dw_b=P['conv_dw_b'][l][None, :], conv_norm_g=P['conv_norm_g'][l][None, :],
        ml_qk_conv_w=P['ml_qk_conv_w'][l], ml_qk_conv_b=P['ml_qk_conv_b'][l][None, :],
        ml_norm_g=P['ml_norm_g'][l][None, :], na_table=_na_bias_table(P['na_rpb'][l]),
        w_out=P['w_out'][l].astype(BF16), norm_ffn_g=P['norm_ffn_g'][l][None, :],
        wrh=wr.astype(BF16), wrl=(wr - wr.astype(BF16).astype(F32)).astype(BF16), br=br,
        wg=P['w_exp_gate'], wu=P['w_exp_up'], wd=P['w_exp_down'], layer=l,
    )


def _layer(x, B, T, p):
    dims = p['dims']
    u, qkpre, v, o, naq, nak, nav, gates = _in_proj(x, p['norm_mix_g'], p['w_in'], p['gate_b'], p['nq'], p['nk'],
                                                    p['bd'], dims)
    yc, hf, hb, yna = _mixers(B, T, u, qkpre, v, naq, nak, nav, gates, p)
    x1, xn, logits = _out_proj(yc, hf, hb, o, yna, x, p['ml_norm_g'], p['w_out'], p['norm_ffn_g'], p['wrh'], p['wrl'],
                               p['br'], dims)
    return _moe(x1, xn, logits, p['wg'], p['wu'], p['wd'], p['layer'])


def kernel(x_prompt, x_sample, norm_mix_g, w_in, conv_dw_w, conv_dw_b, conv_norm_g, ml_qk_conv_w, ml_qk_conv_b,
           ml_gate_b, ml_norm_g, na_q_norm_g, na_k_norm_g, na_rpb, w_out, norm_ffn_g, w_router_group,
           b_router_group, w_router_expert, b_router_expert, w_exp_gate, w_exp_up, w_exp_down):
    P = dict(norm_mix_g=norm_mix_g, w_in=w_in, conv_dw_w=conv_dw_w, conv_dw_b=conv_dw_b, conv_norm_g=conv_norm_g,
             ml_qk_conv_w=ml_qk_conv_w, ml_qk_conv_b=ml_qk_conv_b, ml_gate_b=ml_gate_b, ml_norm_g=ml_norm_g,
             na_q_norm_g=na_q_norm_g, na_k_norm_g=na_k_norm_g, na_rpb=na_rpb, w_out=w_out, norm_ffn_g=norm_ffn_g,
             w_router_group=w_router_group, b_router_group=b_router_group, w_router_expert=w_router_expert,
             b_router_expert=b_router_expert, w_exp_gate=w_exp_gate, w_exp_up=w_exp_up, w_exp_down=w_exp_down)
    depth = w_in.shape[0]
    d = x_prompt.shape[-1]
    groups = [(x_prompt.shape[0], x_prompt.shape[1], x_prompt.reshape(-1, d)),
              (x_sample.shape[0], x_sample.shape[1], x_sample.reshape(-1, d))]
    for l in range(depth):
        p = _prep_layer(l, P)
        groups = [(B, T, _layer(x, B, T, p)) for B, T, x in groups]
    return tuple(x.reshape(B, T, d) for B, T, x in groups)
```

```python
import functools

import numpy as np

import jax
import jax.numpy as jnp
from jax import lax
from jax.experimental import pallas as pl
from jax.experimental.pallas import tpu as pltpu

F32 = jnp.float32
BF16 = jnp.bfloat16

ML_HEADS = 4
NA_HEADS = 4
CONV_K = 31
ML_CHUNK = 128
GRID_W = 64
NA_WIN_ROWS = 8
NA_WIN_COLS = 16
N_GROUPS = 8
EXPERTS_PER_GROUP = 8
N_EXPERTS = N_GROUPS * EXPERTS_PER_GROUP
TOP_K = 2
EPS = 1e-6

LANES = 128
SUBLANES = 8
VMEM_LIMIT = 48 * 1024 * 1024

TOKEN_TILE = 512
EXPERT_BLOCK = 256
MOE_TILE = 1024


def _cparams(*sem):
    return pltpu.CompilerParams(dimension_semantics=sem, vmem_limit_bytes=VMEM_LIMIT)


def _full(shape):
    return pl.BlockSpec(shape, lambda *_: (0,) * len(shape))


def _rows(tm, width):
    return pl.BlockSpec((tm, width), lambda i: (i, 0))


ROW_TILE = SUBLANES


def _load_row_tiles(ref, r0, rows):
    return jnp.concatenate([ref[pl.ds(r0 * ROW_TILE + s, rows, stride=ROW_TILE), :] for s in range(ROW_TILE)],
                           axis=-1)


def _store_row_tiles(ref, r0, val):
    for s in range(ROW_TILE):
        ref[pl.ds(r0 * ROW_TILE + s, val.shape[0], stride=ROW_TILE), :] = val[:, s * LANES:(s + 1) * LANES]


def _group_mean(zz, bd):
    hi = zz.astype(BF16)
    lo = (zz - hi.astype(F32)).astype(BF16)
    return (jnp.dot(hi, bd, preferred_element_type=F32) + jnp.dot(lo, bd, preferred_element_type=F32))


def _in_proj_kernel(x_ref, g_ref, w_ref, gb_ref, nq_ref, nk_ref, bd_ref,
                    u_ref, qk_ref, v_ref, o_ref, naq_ref, nak_ref, nav_ref, gt_ref, *, dims):
    cw, mw, nw = dims
    x = x_ref[...]
    ms = jnp.mean(x * x, axis=-1, keepdims=True)
    xn = (x * lax.rsqrt(ms + EPS) * g_ref[...]).astype(BF16)

    def proj(lo, width):
        return jnp.dot(xn, w_ref[:, lo:lo + width], preferred_element_type=F32)

    zc = proj(0, 2 * cw)
    u_ref[...] = zc[:, :cw] * jax.nn.sigmoid(zc[:, cw:])
    off = 2 * cw
    qk_ref[...] = proj(off, 2 * mw)
    v_ref[...] = proj(off + 2 * mw, mw).astype(BF16)
    o_ref[...] = proj(off + 3 * mw, mw)
    off += 4 * mw
    bd = bd_ref[...]
    zq = proj(off, nw)
    zk = proj(off + nw, nw)
    hd = nw // NA_HEADS
    naq_ref[...] = (zq * lax.rsqrt(_group_mean(zq * zq, bd) + EPS) * nq_ref[...] * (hd ** -0.5)).astype(BF16)
    nak_ref[...] = (zk * lax.rsqrt(_group_mean(zk * zk, bd) + EPS) * nk_ref[...]).astype(BF16)
    nav_ref[...] = proj(off + 2 * nw, nw).astype(BF16)
    off += 3 * nw
    gt_ref[...] = proj(off, LANES) + gb_ref[...]


def _in_proj(x, g, w, gate_b, nq, nk, bd, dims):
    n, d = x.shape
    cw, mw, nw = dims
    tm = min(TOKEN_TILE, n)
    outs = [(cw, F32), (2 * mw, F32), (mw, BF16), (mw, F32), (nw, BF16), (nw, BF16), (nw, BF16), (LANES, F32)]
    return pl.pallas_call(
        functools.partial(_in_proj_kernel, dims=dims),
        grid=(n // tm,),
        in_specs=[_rows(tm, d), _full(g.shape), _full(w.shape), _full(gate_b.shape),
                  _full(nq.shape), _full(nk.shape), _full(bd.shape)],
        out_specs=[_rows(tm, wd) for wd, _ in outs],
        out_shape=[jax.ShapeDtypeStruct((n, wd), dt) for wd, dt in outs],
        compiler_params=_cparams("parallel"),
        name="in_proj",
    )(x, g, w, gate_b, nq, nk, bd)


OUT_SUB = 128


def _out_proj_kernel(yc_ref, hf_ref, hb_ref, o_ref, yna_ref, x_ref, mg_ref, w_ref, fg_ref, wrh_ref, wrl_ref, br_ref,
                     x1_ref, xn_ref, ri_ref, rf_ref, cnt_ref, lg_ref, carry_ref, *, dims):
    cw, mw, nw = dims
    hd = mw // ML_HEADS
    for r0 in range(0, x_ref.shape[0], OUT_SUB):
        rs = slice(r0, r0 + OUT_SUB)
        hm = hf_ref[rs, :] + hb_ref[rs, :]
        parts = []
        for h in range(ML_HEADS):
            seg = hm[:, h * hd:(h + 1) * hd]
            parts.append(seg * lax.rsqrt(jnp.mean(seg * seg, axis=-1, keepdims=True) + EPS))
        yml = jnp.concatenate(parts, axis=-1) * mg_ref[...] * jax.nn.sigmoid(o_ref[rs, :])
        y = jnp.dot(yc_ref[rs, :], w_ref[0:cw, :], preferred_element_type=F32)
        y += jnp.dot(yml.astype(BF16), w_ref[cw:cw + mw, :], preferred_element_type=F32)
        y += jnp.dot(yna_ref[rs, :], w_ref[cw + mw:cw + mw + nw, :], preferred_element_type=F32)
        x1 = x_ref[rs, :] + y
        x1_ref[rs, :] = x1
        xn = x1 * lax.rsqrt(jnp.mean(x1 * x1, axis=-1, keepdims=True) + EPS) * fg_ref[...]
        _store_row_tiles(xn_ref, r0, xn)
        xh = xn.astype(BF16)
        xl = (xn - xh.astype(F32)).astype(BF16)
        lg = jnp.dot(xh, wrh_ref[...], preferred_element_type=F32)
        lg += jnp.dot(xl, wrh_ref[...], preferred_element_type=F32)
        lg += jnp.dot(xh, wrl_ref[...], preferred_element_type=F32)
        lg_ref[rs, :] = lg + br_ref[...]
    _route_tile(lg_ref[...], ri_ref, rf_ref, cnt_ref, carry_ref)


def _out_proj(yc, hf, hb, o, yna, x, mg, w, fg, wrh, wrl, br, dims):
    n, d = x.shape
    cw, mw, nw = dims
    tm = min(TOKEN_TILE, n)
    assert tm % OUT_SUB == 0
    return pl.pallas_call(
        functools.partial(_out_proj_kernel, dims=dims),
        grid=(n // tm,),
        in_specs=[_rows(tm, cw), _rows(tm, mw), _rows(tm, mw), _rows(tm, mw), _rows(tm, nw), _rows(tm, d),
                  _full(mg.shape), _full(w.shape), _full(fg.shape), _full(wrh.shape), _full(wrl.shape),
                  _full(br.shape)],
        out_specs=[_rows(tm, d), _rows(tm * ROW_TILE, LANES), _rows(tm, LANES), _rows(tm, LANES),
                   _full((1, LANES))],
        out_shape=[jax.ShapeDtypeStruct((n, d), F32), jax.ShapeDtypeStruct((n * ROW_TILE, LANES), F32),
                   jax.ShapeDtypeStruct((n, LANES), jnp.int32), jax.ShapeDtypeStruct((n, LANES), F32),
                   jax.ShapeDtypeStruct((1, LANES), F32)],
        scratch_shapes=[pltpu.VMEM((tm, LANES), F32), pltpu.VMEM((1, LANES), F32)],
        compiler_params=_cparams("arbitrary"),
        name="out_proj",
    )(yc, hf, hb, o, yna, x, mg, w, fg, wrh, wrl, br)


def _expert_kernel(be_ref, nb_ref, x_ref, wg_ref, wu_ref, wd_ref, y_ref, wgb_ref, wub_ref, wdb_ref):
    i = pl.program_id(0)

    @pl.when((i == 0) | (be_ref[i] != be_ref[jnp.maximum(i - 1, 0)]))
    def _():
        wgb_ref[...] = wg_ref[0].astype(BF16)
        wub_ref[...] = wu_ref[0].astype(BF16)
        wdb_ref[...] = wd_ref[0].astype(BF16)

    @pl.when(i < nb_ref[0])
    def _():
        bm = x_ref.shape[0] // ROW_TILE
        xb = _load_row_tiles(x_ref, 0, bm).astype(BF16)
        g = jnp.dot(xb, wgb_ref[...], preferred_element_type=F32)
        u = jnp.dot(xb, wub_ref[...], preferred_element_type=F32)
        h = (g * jax.nn.sigmoid(g) * u).astype(BF16)
        _store_row_tiles(y_ref, 0, jnp.dot(h, wdb_ref[...], preferred_element_type=F32))

    @pl.when(i >= nb_ref[0])
    def _():
        y_ref[...] = jnp.zeros_like(y_ref)


def _experts(buf, blk_e, n_used, wg, wu, wd, layer):
    d, de = wg.shape[-2:]
    assert d == ROW_TILE * LANES
    bm = EXPERT_BLOCK
    tb = bm * ROW_TILE
    grid_spec = pltpu.PrefetchScalarGridSpec(
        num_scalar_prefetch=2,
        grid=(buf.shape[0] // tb,),
        in_specs=[pl.BlockSpec((tb, LANES), lambda i, be, nb: (jnp.minimum(i, nb[0] - 1), 0)),
                  pl.BlockSpec((None, 1, d, de), lambda i, be, nb: (layer, be[i], 0, 0)),
                  pl.BlockSpec((None, 1, d, de), lambda i, be, nb: (layer, be[i], 0, 0)),
                  pl.BlockSpec((None, 1, de, d), lambda i, be, nb: (layer, be[i], 0, 0))],
        out_specs=pl.BlockSpec((tb, LANES), lambda i, be, nb: (i, 0)),
        scratch_shapes=[pltpu.VMEM((d, de), BF16), pltpu.VMEM((d, de), BF16), pltpu.VMEM((de, d), BF16)],
    )
    return pl.pallas_call(
        _expert_kernel,
        grid_spec=grid_spec,
        out_shape=jax.ShapeDtypeStruct(buf.shape, F32),
        compiler_params=_cparams("arbitrary"),
        name="experts",
    )(blk_e, n_used, buf, wg, wu, wd)


NA_QROWS = 4
NEG_BIG = -1e30


def _bias_expand_kernel(rpb_ref, selr_ref, selc_ref, o_ref):
    rows = jnp.dot(selr_ref[...], rpb_ref[0], preferred_element_type=F32, precision=lax.Precision.HIGHEST)
    o_ref[0] = jnp.dot(rows, selc_ref[...], preferred_element_type=F32, precision=lax.Precision.HIGHEST)


def _na_bias_table(rpb):
    nh = rpb.shape[0]
    a = np.arange(NA_QROWS)[:, None]
    i = np.arange(3 * NA_QROWS)[None, :]
    half = NA_WIN_ROWS // 2
    a0 = 0 * a
    row_ok = np.stack([(i >= a0 + NA_QROWS) & (i < a0 + NA_QROWS + NA_WIN_ROWS),
                       (i >= a + NA_QROWS - half) & (i < a + NA_QROWS - half + NA_WIN_ROWS),
                       (i >= a0 + 2 * NA_QROWS - NA_WIN_ROWS) & (i < a0 + 2 * NA_QROWS)])
    ro = np.clip(i - NA_QROWS - a + (NA_WIN_ROWS - 1), 0, 2 * NA_WIN_ROWS - 2)
    c = np.arange(GRID_W)
    cs = np.clip(c - NA_WIN_COLS // 2, 0, GRID_W - NA_WIN_COLS)
    col_ok = (c[None, :] >= cs[:, None]) & (c[None, :] < cs[:, None] + NA_WIN_COLS)
    co = np.clip(c[None, :] - c[:, None] + (NA_WIN_COLS - 1), 0, 2 * NA_WIN_COLS - 2)
    nr, nc = 2 * SUBLANES, LANES
    sel_r = (ro.reshape(-1)[:, None] == np.arange(nr)[None, :]).astype(np.float32)
    sel_c = (np.arange(nc)[:, None] == co.reshape(-1)[None, :]).astype(np.float32)
    rpb_p = jnp.pad(rpb.astype(F32), ((0, 0), (0, nr - rpb.shape[1]), (0, nc - rpb.shape[2])))
    nx, ny = sel_r.shape[0], sel_c.shape[1]
    bias = pl.pallas_call(
        _bias_expand_kernel,
        grid=(nh,),
        in_specs=[pl.BlockSpec((1, nr, nc), lambda h: (h, 0, 0)), _full(sel_r.shape), _full(sel_c.shape)],
        out_specs=pl.BlockSpec((1, nx, ny), lambda h: (h, 0, 0)),
        out_shape=jax.ShapeDtypeStruct((nh, nx, ny), F32),
        compiler_params=_cparams("parallel"),
        name="na_bias_expand",
    )(rpb_p, jnp.asarray(sel_r), jnp.asarray(sel_c))
    bias = bias.reshape(nh, NA_QROWS, 3 * NA_QROWS, GRID_W, GRID_W).transpose(0, 1, 3, 2, 4)
    ok = row_ok[:, None, :, None, :, None] & col_ok[None, None, None, :, None, :]
    tab = jnp.where(jnp.asarray(ok), bias[None], NEG_BIG)
    return tab.reshape(3, nh, NA_QROWS * GRID_W, 3 * NA_QROWS * GRID_W)


def _na_kernel(q_ref, kp_ref, kc_ref, kn_ref, vp_ref, vc_ref, vn_ref, b_ref, o_ref):
    q = q_ref[...]
    kcat = jnp.concatenate([kp_ref[...], kc_ref[...], kn_ref[...]], axis=0)
    vcat = jnp.concatenate([vp_ref[...], vc_ref[...], vn_ref[...]], axis=0)
    nw = q.shape[-1]
    hd = nw // NA_HEADS
    lane_head = lax.broadcasted_iota(jnp.int32, q.shape, 1) // hd
    acc = jnp.zeros(q.shape, F32)
    for h in range(NA_HEADS):
        sel = lane_head == h
        qm = jnp.where(sel, q, jnp.zeros_like(q))
        s = lax.dot_general(qm, kcat, (((1,), (1,)), ((), ())), preferred_element_type=F32) + b_ref[0, h]
        m = jnp.max(s, axis=-1, keepdims=True)
        p = jnp.exp(s - m)
        l = jnp.sum(p, axis=-1, keepdims=True)
        oh = jnp.dot(p.astype(BF16), vcat, preferred_element_type=F32)
        acc = jnp.where(sel, oh / l, acc)
    o_ref[...] = acc.astype(o_ref.dtype)


def _neigh_attn_pallas(q, k, v, table, B, T):
    n, nw = q.shape
    tq = NA_QROWS * GRID_W
    J = T // tq
    assert T % tq == 0 and J >= 2 and T // GRID_W >= NA_WIN_ROWS
    blk = (tq, nw)
    cur = pl.BlockSpec(blk, lambda b, j: (b * J + j, 0))
    prev = pl.BlockSpec(blk, lambda b, j: (b * J + jnp.maximum(j - 1, 0), 0))
    nxt = pl.BlockSpec(blk, lambda b, j: (b * J + jnp.minimum(j + 1, J - 1), 0))
    tab = pl.BlockSpec((1,) + table.shape[1:],
                       lambda b, j: (jnp.where(j == 0, 0, jnp.where(j == J - 1, 2, 1)), 0, 0, 0))
    return pl.pallas_call(
        _na_kernel,
        grid=(B, J),
        in_specs=[cur, prev, cur, nxt, prev, cur, nxt, tab],
        out_specs=cur,
        out_shape=jax.ShapeDtypeStruct((n, nw), BF16),
        compiler_params=_cparams("parallel", "arbitrary"),
        name="neigh_attn",
    )(q, k, k, k, v, v, v, table)


HALO = 16
CONV_ROWS = 64


def _local_conv_kernel(u_ref, up_ref, un_ref, qk_ref, qkp_ref, qkn_ref, cw_ref, cb_ref, cg_ref, mw_ref, mb_ref,
                       yc_ref, q_ref, k_ref, wu_ref, wqk_ref, sh_ref, *, k_scale):
    t = pl.program_id(1)
    nt = pl.num_programs(1)
    tm = u_ref.shape[0]
    keep_p = (t > 0).astype(F32)
    keep_n = (t < nt - 1).astype(F32)
    wu_ref[0:HALO, :] = up_ref[...] * keep_p
    wu_ref[HALO:HALO + tm, :] = u_ref[...]
    wu_ref[HALO + tm:, :] = un_ref[...] * keep_n
    wqk_ref[0:HALO, :] = qkp_ref[...] * keep_p
    wqk_ref[HALO:HALO + tm, :] = qk_ref[...]
    wqk_ref[HALO + tm:, :] = qkn_ref[...] * keep_n
    kc = cw_ref.shape[0]
    km = mw_ref.shape[0]
    mw2 = q_ref.shape[-1]
    nsh = sh_ref.shape[1]
    for ph in range(1, SUBLANES):
        sh_ref[ph - 1] = wu_ref[ph:ph + nsh, :]
    for r0 in range(0, tm, CONV_ROWS):
        acc = jnp.zeros((CONV_ROWS, u_ref.shape[-1]), F32) + cb_ref[...]
        for k in range(kc):
            s = r0 + HALO - kc // 2 + k
            ph = s % SUBLANES
            tap = wu_ref[s:s + CONV_ROWS, :] if ph == 0 else sh_ref[ph - 1, s - ph:s - ph + CONV_ROWS, :]
            acc = acc + tap * cw_ref[k:k + 1, :]
        y = acc * lax.rsqrt(jnp.mean(acc * acc, axis=-1, keepdims=True) + EPS) * cg_ref[...]
        yc_ref[r0:r0 + CONV_ROWS, :] = (y * jax.nn.sigmoid(y)).astype(yc_ref.dtype)
        acc = jnp.zeros((CONV_ROWS, qk_ref.shape[-1]), F32) + mb_ref[...]
        for k in range(km):
            s = r0 + HALO - km // 2 + k
            acc = acc + wqk_ref[s:s + CONV_ROWS, :] * mw_ref[k:k + 1, :]
        y = acc * jax.nn.sigmoid(acc)
        q_ref[r0:r0 + CONV_ROWS, :] = y[:, :mw2].astype(q_ref.dtype)
        k_ref[r0:r0 + CONV_ROWS, :] = (y[:, mw2:] * k_scale).astype(k_ref.dtype)


def _local_conv(u, qkpre, cw, cb, cg, mw, mb, B, T):
    n, cwid = u.shape
    qw = qkpre.shape[-1]
    tm = min(TOKEN_TILE, T)
    nt = T // tm
    hb = tm // HALO
    nhb = n // HALO
    assert T % tm == 0 and tm % CONV_ROWS == 0 and CONV_K // 2 <= HALO

    def cur(w):
        return pl.BlockSpec((tm, w), lambda b, t: (b * nt + t, 0))

    def prev(w):
        return pl.BlockSpec((HALO, w), lambda b, t: (jnp.maximum((b * nt + t) * hb - 1, 0), 0))

    def nxt(w):
        return pl.BlockSpec((HALO, w), lambda b, t: (jnp.minimum((b * nt + t + 1) * hb, nhb - 1), 0))

    hd = (qw // 2) // ML_HEADS
    return pl.pallas_call(
        functools.partial(_local_conv_kernel, k_scale=hd ** -0.5),
        grid=(B, nt),
        in_specs=[cur(cwid), prev(cwid), nxt(cwid), cur(qw), prev(qw), nxt(qw),
                  _full(cw.shape), _full(cb.shape), _full(cg.shape), _full(mw.shape), _full(mb.shape)],
        out_specs=[cur(cwid), cur(qw // 2), cur(qw // 2)],
        out_shape=[jax.ShapeDtypeStruct((n, cwid), BF16), jax.ShapeDtypeStruct((n, qw // 2), BF16),
                   jax.ShapeDtypeStruct((n, qw // 2), BF16)],
        scratch_shapes=[pltpu.VMEM((tm + 2 * HALO, cwid), F32), pltpu.VMEM((tm + 2 * HALO, qw), F32),
                        pltpu.VMEM((SUBLANES - 1, tm + 2 * HALO - SUBLANES, cwid), F32)],
        compiler_params=_cparams("parallel", "parallel"),
        name="local_conv",
    )(u, u, u, qkpre, qkpre, qkpre, cw, cb, cg, mw, mb)


def _scan_rows(x, reverse, combine, fill):
    L = x.shape[0]
    row = lax.broadcasted_iota(jnp.int32, x.shape, 0)
    s = 1
    while s < L:
        if reverse:
            x = combine(x, jnp.where(row < L - s, pltpu.roll(x, L - s, axis=0), fill))
        else:
            x = combine(x, jnp.where(row >= s, pltpu.roll(x, s, axis=0), fill))
        s *= 2
    return x


def _mlstm_direction(q_ref, k_ref, v_ref, g_ref, h_ref, c_ref, m_ref, reverse):
    L = q_ref.shape[0]
    hd = q_ref.shape[-1] // ML_HEADS
    ci = 2 * ML_HEADS if reverse else 0
    cf = ci + ML_HEADS
    g = g_ref[...]
    gi = pltpu.roll(g, ML_HEADS, axis=1)
    lf = jnp.minimum(g, 0.0) - jnp.log1p(jnp.exp(-jnp.abs(g)))
    b = _scan_rows(lf, reverse, jnp.add, 0.0)
    r = gi - b
    cm = _scan_rows(r, reverse, jnp.maximum, NEG_BIG)
    total = b[0:1, :] if reverse else b[L - 1:L, :]
    m_prev = m_ref[...]
    inter = b + m_prev
    m_j = jnp.maximum(inter, b + cm)
    a_t = b - m_j
    e_t = jnp.exp(inter - m_j)
    z_t = jnp.exp(-m_j)
    w = total - b + gi
    m_loc = jnp.max(w, axis=0, keepdims=True)
    wa_t = jnp.exp(w - m_loc)
    m_new = jnp.maximum(total + m_prev, m_loc)
    dec = jnp.exp(total + m_prev - m_new)
    inc = jnp.exp(m_loc - m_new)
    m_ref[...] = m_new
    rt = jnp.transpose(r)
    jj = lax.broadcasted_iota(jnp.int32, (L, L), 0)
    ss = lax.broadcasted_iota(jnp.int32, (L, L), 1)
    visible = (ss >= jj) if reverse else (ss <= jj)
    ones = jnp.ones((L, hd), BF16)
    for h in range(ML_HEADS):
        sl = slice(h * hd, (h + 1) * hd)
        col = slice(cf + h, cf + h + 1)
        qh = q_ref[:, sl]
        kh = k_ref[:, sl]
        v_ext = jnp.concatenate([v_ref[:, sl], ones], axis=-1)
        c_prev = c_ref[h]
        pm = jnp.exp(jnp.where(visible, a_t[:, col] + rt[col, :], NEG_BIG))
        qk = lax.dot_general(qh, kh, (((1,), (1,)), ((), ())), preferred_element_type=F32) * pm
        tot = (jnp.dot(qk.astype(BF16), v_ext, preferred_element_type=F32)
               + e_t[:, col] * jnp.dot(qh, c_prev.astype(BF16), preferred_element_type=F32))
        h_ref[:, sl] = tot[:, :hd] / jnp.maximum(jnp.abs(tot[:, hd:]), z_t[:, col])
        ka = (kh.astype(F32) * wa_t[:, col]).astype(BF16)
        s_ext = lax.dot_general(ka, v_ext, (((0,), (0,)), ((), ())), preferred_element_type=F32)
        c_ref[h] = dec[:, col] * c_prev + inc[:, col] * s_ext


def _mlstm_kernel(qf_ref, kf_ref, vf_ref, gf_ref, qb_ref, kb_ref, vb_ref, gb_ref, hf_ref, hb_ref,
                  cf_ref, cb_ref, mf_ref, mb_ref):
    @pl.when(pl.program_id(1) == 0)
    def _():
        cf_ref[...] = jnp.zeros_like(cf_ref)
        cb_ref[...] = jnp.zeros_like(cb_ref)
        mf_ref[...] = jnp.zeros_like(mf_ref)
        mb_ref[...] = jnp.zeros_like(mb_ref)

    _mlstm_direction(qf_ref, kf_ref, vf_ref, gf_ref, hf_ref, cf_ref, mf_ref, False)
    _mlstm_direction(qb_ref, kb_ref, vb_ref, gb_ref, hb_ref, cb_ref, mb_ref, True)


def _mlstm(q, k, v, gates, B, T):
    n, mw = q.shape
    L = ML_CHUNK
    nc = T // L
    hd = mw // ML_HEADS
    assert T % L == 0 and hd == LANES

    def fwd(w):
        return pl.BlockSpec((L, w), lambda b, c: (b * nc + c, 0))

    def bwd(w):
        return pl.BlockSpec((L, w), lambda b, c: (b * nc + nc - 1 - c, 0))

    return pl.pallas_call(
        _mlstm_kernel,
        grid=(B, nc),
        in_specs=[fwd(mw), fwd(mw), fwd(mw), fwd(LANES), bwd(mw), bwd(mw), bwd(mw), bwd(LANES)],
        out_specs=[fwd(mw), bwd(mw)],
        out_shape=[jax.ShapeDtypeStruct((n, mw), F32), jax.ShapeDtypeStruct((n, mw), F32)],
        scratch_shapes=[pltpu.VMEM((ML_HEADS, hd, 2 * hd), F32), pltpu.VMEM((ML_HEADS, hd, 2 * hd), F32),
                        pltpu.VMEM((1, LANES), F32), pltpu.VMEM((1, LANES), F32)],
        compiler_params=_cparams("parallel", "arbitrary"),
        name="mlstm",
    )(q, k, v, gates, q, k, v, gates)


def _mixers(B, T, u, qkpre, v, naq, nak, nav, gates, p):
    yc, q, k = _local_conv(u, qkpre, p['conv_dw_w'], p['conv_dw_b'], p['conv_norm_g'],
                           p['ml_qk_conv_w'], p['ml_qk_conv_b'], B, T)
    hf, hb = _mlstm(q, k, v, gates, B, T)
    y_na = _neigh_attn_pallas(naq, nak, nav, p['na_table'], B, T)
    return yc, hf, hb, y_na


def _first_lane(mask, lane):
    return jnp.min(jnp.where(mask, lane, LANES), axis=-1, keepdims=True)


def _route_tile(lg, ri_ref, rf_ref, cnt_ref, carry_ref):
    @pl.when(pl.program_id(0) == 0)
    def _():
        carry_ref[...] = jnp.zeros_like(carry_ref)

    tm = lg.shape[0]
    lane = lax.broadcasted_iota(jnp.int32, lg.shape, 1)
    gl = jnp.where(lane < N_GROUPS, lg, NEG_BIG)
    gmax = jnp.max(gl, axis=-1, keepdims=True)
    pg = 1.0 / jnp.sum(jnp.exp(gl - gmax), axis=-1, keepdims=True)
    gsel = _first_lane(gl == gmax, lane)
    lo = N_GROUPS + gsel * EXPERTS_PER_GROUP
    in_grp = (lane >= lo) & (lane < lo + EXPERTS_PER_GROUP)
    el = jnp.where(in_grp, lg, NEG_BIG)
    emax = jnp.max(el, axis=-1, keepdims=True)
    esum = jnp.sum(jnp.exp(el - emax), axis=-1, keepdims=True)
    idx1 = _first_lane(in_grp & (el == emax), lane)
    el2 = jnp.where(lane == idx1, NEG_BIG, el)
    emax2 = jnp.max(el2, axis=-1, keepdims=True)
    idx2 = _first_lane(in_grp & (el2 == emax2) & (lane != idx1), lane)
    p1 = 1.0 / esum
    p2 = jnp.exp(emax2 - emax) / esum
    g1 = pg * p1 / (p1 + p2)
    g2 = pg * p2 / (p1 + p2)
    hit1 = lane == idx1
    hit2 = lane == idx2
    onehot = jnp.where(hit1 | hit2, 1.0, 0.0)
    r = lax.broadcasted_iota(jnp.int32, (tm, tm), 0)
    c = lax.broadcasted_iota(jnp.int32, (tm, tm), 1)
    tri = jnp.where(r > c, 1.0, 0.0).astype(BF16)
    prefix = jnp.dot(tri, onehot.astype(BF16), preferred_element_type=F32) + carry_ref[...]
    rank1 = jnp.sum(jnp.where(hit1, prefix, 0.0), axis=-1, keepdims=True).astype(jnp.int32)
    rank2 = jnp.sum(jnp.where(hit2, prefix, 0.0), axis=-1, keepdims=True).astype(jnp.int32)
    carry = carry_ref[...] + jnp.sum(onehot, axis=0, keepdims=True)
    carry_ref[...] = carry
    cnt_ref[...] = carry
    ri_ref[...] = jnp.where(lane == 0, idx1 - N_GROUPS, jnp.where(lane == 1, idx2 - N_GROUPS,
                            jnp.where(lane == 2, rank1, jnp.where(lane == 3, rank2, 0))))
    rf_ref[...] = jnp.where(lane == 0, g1, jnp.where(lane == 1, g2, 0.0))


def _dest_kernel(ri_ref, ps_ref, d_ref):
    ri = ri_ref[...]
    lane = lax.broadcasted_iota(jnp.int32, ri.shape, 1)
    ps = ps_ref[...]
    out = jnp.zeros(ri.shape, jnp.int32)
    for k in range(TOP_K):
        start = jnp.sum(jnp.where(lane == ri[:, k:k + 1], ps, 0.0), axis=-1, keepdims=True).astype(jnp.int32)
        out = jnp.where(lane == k, start + ri[:, TOP_K + k:TOP_K + k + 1], out)
    d_ref[...] = out


def _dest(ri, pstart_row):
    n = ri.shape[0]
    tm = min(4 * TOKEN_TILE, n)
    assert n % tm == 0
    return pl.pallas_call(
        _dest_kernel,
        grid=(n // tm,),
        in_specs=[_rows(tm, LANES), _full((1, LANES))],
        out_specs=_rows(tm, LANES),
        out_shape=jax.ShapeDtypeStruct((n, LANES), jnp.int32),
        compiler_params=_cparams("parallel"),
        name="dest",
    )(ri, pstart_row)


ROW_UNROLL = 8


def _rows_of(ref, first, count):
    return ref.at[pl.ds(pl.multiple_of(first * ROW_TILE, ROW_TILE), count * ROW_TILE), :]


def _row_copy(src, i, dst, j, sem):
    return pltpu.make_async_copy(_rows_of(src, i, 1), _rows_of(dst, j, 1), sem)


def _dispatch_kernel(dest_hbm, pend_ref, x_ref, buf_hbm, dest_smem, zero_ref, sem_idx, sem_rows):
    i = pl.program_id(0)
    tm = x_ref.shape[0] // ROW_TILE
    bm = zero_ref.shape[0] // ROW_TILE

    @pl.when(i == 0)
    def _():
        zero_ref[...] = jnp.zeros_like(zero_ref)

        def last_block(e):
            start = pl.multiple_of(jnp.maximum(pend_ref[e] - bm, 0), bm)
            return pltpu.make_async_copy(zero_ref, _rows_of(buf_hbm, start, bm), sem_rows)

        for e in range(N_EXPERTS):
            last_block(e).start()
        for e in range(N_EXPERTS):
            last_block(e).wait()

        def unused_block(blk, carry):
            copy = pltpu.make_async_copy(zero_ref, _rows_of(buf_hbm, pl.multiple_of(blk * bm, bm), bm), sem_rows)
            copy.start()
            copy.wait()
            return carry

        lax.fori_loop(pend_ref[N_EXPERTS - 1] // bm, buf_hbm.shape[0] // (bm * ROW_TILE), unused_block, 0)

    idx_copy = pltpu.make_async_copy(dest_hbm.at[i], dest_smem, sem_idx)
    idx_copy.start()
    idx_copy.wait()

    def body(o, carry):
        for u in range(ROW_UNROLL):
            r = o * ROW_UNROLL + u
            for k in range(TOP_K):
                _row_copy(x_ref, r, buf_hbm, dest_smem[TOP_K * r + k], sem_rows).start()
        return carry

    lax.fori_loop(0, tm // ROW_UNROLL, body, 0)
    for k in range(TOP_K):
        pltpu.make_async_copy(x_ref, _rows_of(buf_hbm, 0, tm), sem_rows).wait()


def _dispatch(xn, dest, pend, n_rows):
    n = xn.shape[0] // ROW_TILE
    tm = min(MOE_TILE, n)
    assert n % tm == 0
    return pl.pallas_call(
        _dispatch_kernel,
        grid=(n // tm,),
        in_specs=[pl.BlockSpec(memory_space=pl.ANY), pl.BlockSpec(memory_space=pltpu.SMEM),
                  _rows(tm * ROW_TILE, LANES)],
        out_specs=pl.BlockSpec(memory_space=pl.ANY),
        out_shape=jax.ShapeDtypeStruct((n_rows * ROW_TILE, LANES), xn.dtype),
        scratch_shapes=[pltpu.SMEM((TOP_K * tm,), jnp.int32), pltpu.VMEM((EXPERT_BLOCK * ROW_TILE, LANES), xn.dtype),
                        pltpu.SemaphoreType.DMA, pltpu.SemaphoreType.DMA],
        compiler_params=_cparams("arbitrary"),
        name="dispatch",
    )(dest.reshape(n // tm, TOP_K * tm), pend, xn)


def _combine_kernel(dest_hbm, y_hbm, x_ref, rf_ref, o_ref, dest_smem, y0_ref, y1_ref, sem_idx, sem_rows):
    i = pl.program_id(0)
    tm = x_ref.shape[0]
    idx_copy = pltpu.make_async_copy(dest_hbm.at[i], dest_smem, sem_idx)
    idx_copy.start()
    idx_copy.wait()
    ybufs = (y0_ref, y1_ref)

    def body(o, carry):
        for u in range(ROW_UNROLL):
            r = o * ROW_UNROLL + u
            for k in range(TOP_K):
                _row_copy(y_hbm, dest_smem[TOP_K * r + k], ybufs[k], r, sem_rows).start()
        return carry

    lax.fori_loop(0, tm // ROW_UNROLL, body, 0)
    for k in range(TOP_K):
        pltpu.make_async_copy(_rows_of(y_hbm, 0, tm), ybufs[k], sem_rows).wait()
    sub = min(tm, EXPERT_BLOCK)
    for r0 in range(0, tm, sub):
        rs = slice(r0, r0 + sub)
        rf = rf_ref[rs, :]
        o_ref[rs, :] = x_ref[rs, :] + (_load_row_tiles(y0_ref, r0, sub) * rf[:, 0:1]
                                       + _load_row_tiles(y1_ref, r0, sub) * rf[:, 1:2])


def _combine(x1, y, dest, rf):
    n, d = x1.shape
    tm = min(MOE_TILE, n)
    assert n % tm == 0
    return pl.pallas_call(
        _combine_kernel,
        grid=(n // tm,),
        in_specs=[pl.BlockSpec(memory_space=pl.ANY), pl.BlockSpec(memory_space=pl.ANY), _rows(tm, d),
                  _rows(tm, LANES)],
        out_specs=_rows(tm, d),
        out_shape=jax.ShapeDtypeStruct((n, d), x1.dtype),
        scratch_shapes=[pltpu.SMEM((TOP_K * tm,), jnp.int32), pltpu.VMEM((tm * ROW_TILE, LANES), y.dtype),
                        pltpu.VMEM((tm * ROW_TILE, LANES), y.dtype), pltpu.SemaphoreType.DMA,
                        pltpu.SemaphoreType.DMA],
        compiler_params=_cparams("arbitrary"),
        name="combine",
    )(dest.reshape(n // tm, TOP_K * tm), y, x1, rf)


def _moe(x1, xn, ri, rf, cnt, wg, wu, wd, layer):
    n, d = x1.shape
    bm = EXPERT_BLOCK
    counts = cnt[0, N_GROUPS:N_GROUPS + N_EXPERTS].astype(jnp.int32)
    psz = (counts + bm - 1) // bm * bm
    pend = jnp.cumsum(psz)
    pstart = pend - psz
    pstart_row = jnp.pad(pstart.astype(F32), (0, LANES - N_EXPERTS))[None, :]
    dest = _dest(ri, pstart_row)[:, :TOP_K]
    n_blocks = -(-(n * TOP_K) // bm) + N_EXPERTS
    blk_start = jnp.arange(n_blocks, dtype=jnp.int32) * bm
    blk_e = jnp.minimum(jnp.sum(pend[None, :] <= blk_start[:, None], axis=1), N_EXPERTS - 1).astype(jnp.int32)
    n_used = (pend[-1] // bm).astype(jnp.int32).reshape(1)
    buf = _dispatch(xn, dest, pend.astype(jnp.int32), n_blocks * bm)
    y = _experts(buf, blk_e, n_used, wg, wu, wd, layer)
    return _combine(x1, y, dest, rf)


def _prep_layer(l, P):
    d = P['w_in'].shape[1]
    cw = P['conv_dw_w'].shape[-1]
    mw = P['ml_norm_g'].shape[-1]
    nw = NA_HEADS * P['na_q_norm_g'].shape[-1]
    off_g = 2 * cw + 4 * mw
    ng = 4 * ML_HEADS
    w = P['w_in'][l]
    w_perm = jnp.concatenate([w[:, :off_g], w[:, off_g + ng:], w[:, off_g:off_g + ng],
                              jnp.zeros((d, LANES - ng), F32)], axis=1).astype(BF16)
    nhd = nw // NA_HEADS
    head_id = jnp.arange(nw) // nhd
    bd = jnp.where(head_id[:, None] == head_id[None, :], 1.0 / nhd, 0.0).astype(BF16)
    wr = jnp.concatenate([P['w_router_group'][l], P['w_router_expert'][l],
                          jnp.zeros((d, LANES - N_GROUPS - N_EXPERTS), F32)], axis=1)
    br = jnp.concatenate([P['b_router_group'][l], P['b_router_expert'][l],
                          jnp.zeros((LANES - N_GROUPS - N_EXPERTS,), F32)])[None, :]
    return dict(
        dims=(cw, mw, nw),
        norm_mix_g=P['norm_mix_g'][l][None, :], w_in=w_perm,
        gate_b=jnp.pad(P['ml_gate_b'][l].reshape(-1), (0, LANES - ng))[None, :],
        nq=jnp.tile(P['na_q_norm_g'][l], NA_HEADS)[None, :], nk=jnp.tile(P['na_k_norm_g'][l], NA_HEADS)[None, :],
        bd=bd,
        conv_dw_w=P['conv_dw_w'][l], conv_dw_b=P['conv_dw_b'][l][None, :], conv_norm_g=P['conv_norm_g'][l][None, :],
        ml_qk_conv_w=P['ml_qk_conv_w'][l], ml_qk_conv_b=P['ml_qk_conv_b'][l][None, :],
        ml_norm_g=P['ml_norm_g'][l][None, :], na_table=_na_bias_table(P['na_rpb'][l]),
        w_out=P['w_out'][l].astype(BF16), norm_ffn_g=P['norm_ffn_g'][l][None, :],
        wrh=wr.astype(BF16), wrl=(wr - wr.astype(BF16).astype(F32)).astype(BF16), br=br,
        wg=P['w_exp_gate'], wu=P['w_exp_up'], wd=P['w_exp_down'], layer=l,
    )


def _layer(x, B, T, p):
    dims = p['dims']
    u, qkpre, v, o, naq, nak, nav, gates = _in_proj(x, p['norm_mix_g'], p['w_in'], p['gate_b'], p['nq'], p['nk'],
                                                    p['bd'], dims)
    yc, hf, hb, yna = _mixers(B, T, u, qkpre, v, naq, nak, nav, gates, p)
    x1, xn, ri, rf, cnt = _out_proj(yc, hf, hb, o, yna, x, p['ml_norm_g'], p['w_out'], p['norm_ffn_g'], p['wrh'],
                                    p['wrl'], p['br'], dims)
    return _moe(x1, xn, ri, rf, cnt, p['wg'], p['wu'], p['wd'], p['layer'])


def kernel(x_prompt, x_sample, norm_mix_g, w_in, conv_dw_w, conv_dw_b, conv_norm_g, ml_qk_conv_w, ml_qk_conv_b,
           ml_gate_b, ml_norm_g, na_q_norm_g, na_k_norm_g, na_rpb, w_out, norm_ffn_g, w_router_group,
           b_router_group, w_router_expert, b_router_expert, w_exp_gate, w_exp_up, w_exp_down):
    P = dict(norm_mix_g=norm_mix_g, w_in=w_in, conv_dw_w=conv_dw_w, conv_dw_b=conv_dw_b, conv_norm_g=conv_norm_g,
             ml_qk_conv_w=ml_qk_conv_w, ml_qk_conv_b=ml_qk_conv_b, ml_gate_b=ml_gate_b, ml_norm_g=ml_norm_g,
             na_q_norm_g=na_q_norm_g, na_k_norm_g=na_k_norm_g, na_rpb=na_rpb, w_out=w_out, norm_ffn_g=norm_ffn_g,
             w_router_group=w_router_group, b_router_group=b_router_group, w_router_expert=w_router_expert,
             b_router_expert=b_router_expert, w_exp_gate=w_exp_gate, w_exp_up=w_exp_up, w_exp_down=w_exp_down)
    depth = w_in.shape[0]
    d = x_prompt.shape[-1]
    groups = [(x_prompt.shape[0], x_prompt.shape[1], x_prompt.reshape(-1, d)),
              (x_sample.shape[0], x_sample.shape[1], x_sample.reshape(-1, d))]
    for l in range(depth):
        p = _prep_layer(l, P)
        groups = [(B, T, _layer(x, B, T, p)) for B, T, x in groups]
    return tuple(x.reshape(B, T, d) for B, T, x in groups)
```

```python
import functools

import numpy as np

import jax
import jax.numpy as jnp
from jax import lax
from jax.experimental import pallas as pl
from jax.experimental.pallas import tpu as pltpu

F32 = jnp.float32
BF16 = jnp.bfloat16

ML_HEADS = 4
NA_HEADS = 4
CONV_K = 31
ML_CHUNK = 128
GRID_W = 64
NA_WIN_ROWS = 8
NA_WIN_COLS = 16
N_GROUPS = 8
EXPERTS_PER_GROUP = 8
N_EXPERTS = N_GROUPS * EXPERTS_PER_GROUP
TOP_K = 2
EPS = 1e-6

LANES = 128
SUBLANES = 8
VMEM_LIMIT = 48 * 1024 * 1024

TOKEN_TILE = 512
EXPERT_BLOCK = 256
MOE_TILE = 1024


def _cparams(*sem):
    return pltpu.CompilerParams(dimension_semantics=sem, vmem_limit_bytes=VMEM_LIMIT)


def _full(shape):
    return pl.BlockSpec(shape, lambda *_: (0,) * len(shape))


def _rows(tm, width):
    return pl.BlockSpec((tm, width), lambda i: (i, 0))


ROW_TILE = SUBLANES


def _load_row_tiles(ref, r0, rows):
    return jnp.concatenate([ref[pl.ds(r0 * ROW_TILE + s, rows, stride=ROW_TILE), :] for s in range(ROW_TILE)],
                           axis=-1)


def _store_row_tiles(ref, r0, val):
    for s in range(ROW_TILE):
        ref[pl.ds(r0 * ROW_TILE + s, val.shape[0], stride=ROW_TILE), :] = val[:, s * LANES:(s + 1) * LANES]


def _group_mean(zz, bd):
    hi = zz.astype(BF16)
    lo = (zz - hi.astype(F32)).astype(BF16)
    return (jnp.dot(hi, bd, preferred_element_type=F32) + jnp.dot(lo, bd, preferred_element_type=F32))


def _combine_in_proj_kernel(dest_hbm, y_hbm, x1_ref, rf_ref, g_ref, w_ref, gb_ref, nq_ref, nk_ref, bd_ref,
                            x_ref, u_ref, qk_ref, v_ref, o_ref, naq_ref, nak_ref, nav_ref, gt_ref,
                            dest_smem, y_ref, sem_idx, sem_rows, *, dims):
    i = pl.program_id(0)
    nt = pl.num_programs(0)
    tm = x1_ref.shape[0]

    def fetch_rows(step, slot):
        base = slot * TOP_K * tm
        idx_copy = pltpu.make_async_copy(dest_hbm.at[step], dest_smem.at[pl.ds(base, TOP_K * tm)], sem_idx.at[slot])
        idx_copy.start()
        idx_copy.wait()

        def body(o, carry):
            for u in range(ROW_UNROLL):
                r = o * ROW_UNROLL + u
                for k in range(TOP_K):
                    _row_copy(y_hbm, dest_smem[base + TOP_K * r + k], y_ref.at[slot, k], r, sem_rows.at[slot]).start()
            return carry

        lax.fori_loop(0, tm // ROW_UNROLL, body, 0)

    @pl.when(i == 0)
    def _():
        fetch_rows(0, 0)

    @pl.when(i + 1 < nt)
    def _():
        fetch_rows(i + 1, (i + 1) % 2)

    slot = i % 2
    for k in range(TOP_K):
        pltpu.make_async_copy(_rows_of(y_hbm, 0, tm), y_ref.at[slot, k], sem_rows.at[slot]).wait()
    sub = min(tm, EXPERT_BLOCK)
    for r0 in range(0, tm, sub):
        rs = slice(r0, r0 + sub)
        rf = rf_ref[rs, :]
        x_ref[rs, :] = x1_ref[rs, :] + (_load_row_tiles(y_ref.at[slot, 0], r0, sub) * rf[:, 0:1]
                                        + _load_row_tiles(y_ref.at[slot, 1], r0, sub) * rf[:, 1:2])
    _in_proj_kernel(x_ref, g_ref, w_ref, gb_ref, nq_ref, nk_ref, bd_ref,
                    u_ref, qk_ref, v_ref, o_ref, naq_ref, nak_ref, nav_ref, gt_ref, dims=dims)


def _in_proj_kernel(x_ref, g_ref, w_ref, gb_ref, nq_ref, nk_ref, bd_ref,
                    u_ref, qk_ref, v_ref, o_ref, naq_ref, nak_ref, nav_ref, gt_ref, *, dims):
    cw, mw, nw = dims
    x = x_ref[...]
    ms = jnp.mean(x * x, axis=-1, keepdims=True)
    xn = (x * lax.rsqrt(ms + EPS) * g_ref[...]).astype(BF16)

    def proj(lo, width):
        return jnp.dot(xn, w_ref[:, lo:lo + width], preferred_element_type=F32)

    zc = proj(0, 2 * cw)
    u_ref[...] = zc[:, :cw] * jax.nn.sigmoid(zc[:, cw:])
    off = 2 * cw
    qk_ref[...] = proj(off, 2 * mw)
    v_ref[...] = proj(off + 2 * mw, mw).astype(BF16)
    o_ref[...] = proj(off + 3 * mw, mw)
    off += 4 * mw
    bd = bd_ref[...]
    zq = proj(off, nw)
    zk = proj(off + nw, nw)
    hd = nw // NA_HEADS
    naq_ref[...] = (zq * lax.rsqrt(_group_mean(zq * zq, bd) + EPS) * nq_ref[...] * (hd ** -0.5)).astype(BF16)
    nak_ref[...] = (zk * lax.rsqrt(_group_mean(zk * zk, bd) + EPS) * nk_ref[...]).astype(BF16)
    nav_ref[...] = proj(off + 2 * nw, nw).astype(BF16)
    off += 3 * nw
    gt_ref[...] = proj(off, LANES) + gb_ref[...]


def _combine_in_proj(pending, g, w, gate_b, nq, nk, bd, dims):
    x1, y, dest, rf = pending
    n, d = x1.shape
    cw, mw, nw = dims
    tm = min(TOKEN_TILE, n)
    outs = [(d, F32), (cw, F32), (2 * mw, F32), (mw, BF16), (mw, F32), (nw, BF16), (nw, BF16), (nw, BF16),
            (LANES, F32)]
    res = pl.pallas_call(
        functools.partial(_combine_in_proj_kernel, dims=dims),
        grid=(n // tm,),
        in_specs=[pl.BlockSpec(memory_space=pl.ANY), pl.BlockSpec(memory_space=pl.ANY), _rows(tm, d),
                  _rows(tm, LANES), _full(g.shape), pl.BlockSpec(w.shape, lambda i: (0, 0), pipeline_mode=pl.Buffered(1)),
                  _full(gate_b.shape), _full(nq.shape), _full(nk.shape), _full(bd.shape)],
        out_specs=[_rows(tm, wd) for wd, _ in outs],
        out_shape=[jax.ShapeDtypeStruct((n, wd), dt) for wd, dt in outs],
        scratch_shapes=[pltpu.SMEM((2 * TOP_K * tm,), jnp.int32),
                        pltpu.VMEM((2, TOP_K, tm * ROW_TILE, LANES), y.dtype),
                        pltpu.SemaphoreType.DMA((2,)), pltpu.SemaphoreType.DMA((2,))],
        compiler_params=_cparams("arbitrary"),
        name="combine_in_proj",
    )(dest.reshape(n // tm, TOP_K * tm), y, x1, rf, g, w, gate_b, nq, nk, bd)
    return res[0], res[1:]


def _in_proj(x, g, w, gate_b, nq, nk, bd, dims):
    n, d = x.shape
    cw, mw, nw = dims
    tm = min(TOKEN_TILE, n)
    outs = [(cw, F32), (2 * mw, F32), (mw, BF16), (mw, F32), (nw, BF16), (nw, BF16), (nw, BF16), (LANES, F32)]
    return pl.pallas_call(
        functools.partial(_in_proj_kernel, dims=dims),
        grid=(n // tm,),
        in_specs=[_rows(tm, d), _full(g.shape), _full(w.shape), _full(gate_b.shape),
                  _full(nq.shape), _full(nk.shape), _full(bd.shape)],
        out_specs=[_rows(tm, wd) for wd, _ in outs],
        out_shape=[jax.ShapeDtypeStruct((n, wd), dt) for wd, dt in outs],
        compiler_params=_cparams("parallel"),
        name="in_proj",
    )(x, g, w, gate_b, nq, nk, bd)


OUT_SUB = 128


def _out_proj_kernel(yc_ref, hf_ref, hb_ref, o_ref, yna_ref, x_ref, mg_ref, w_ref, fg_ref, wrh_ref, wrl_ref, br_ref,
                     x1_ref, xn_ref, ri_ref, rf_ref, cnt_ref, lg_ref, carry_ref, *, dims):
    cw, mw, nw = dims
    hd = mw // ML_HEADS
    for r0 in range(0, x_ref.shape[0], OUT_SUB):
        rs = slice(r0, r0 + OUT_SUB)
        hm = hf_ref[rs, :] + hb_ref[rs, :]
        parts = []
        for h in range(ML_HEADS):
            seg = hm[:, h * hd:(h + 1) * hd]
            parts.append(seg * lax.rsqrt(jnp.mean(seg * seg, axis=-1, keepdims=True) + EPS))
        yml = jnp.concatenate(parts, axis=-1) * mg_ref[...] * jax.nn.sigmoid(o_ref[rs, :])
        y = jnp.dot(yc_ref[rs, :], w_ref[0:cw, :], preferred_element_type=F32)
        y += jnp.dot(yml.astype(BF16), w_ref[cw:cw + mw, :], preferred_element_type=F32)
        y += jnp.dot(yna_ref[rs, :], w_ref[cw + mw:cw + mw + nw, :], preferred_element_type=F32)
        x1 = x_ref[rs, :] + y
        x1_ref[rs, :] = x1
        xn = x1 * lax.rsqrt(jnp.mean(x1 * x1, axis=-1, keepdims=True) + EPS) * fg_ref[...]
        _store_row_tiles(xn_ref, r0, xn)
        xh = xn.astype(BF16)
        xl = (xn - xh.astype(F32)).astype(BF16)
        lg = jnp.dot(xh, wrh_ref[...], preferred_element_type=F32)
        lg += jnp.dot(xl, wrh_ref[...], preferred_element_type=F32)
        lg += jnp.dot(xh, wrl_ref[...], preferred_element_type=F32)
        lg_ref[rs, :] = lg + br_ref[...]
    _route_tile(lg_ref[...], ri_ref, rf_ref, cnt_ref, carry_ref)


def _out_proj(yc, hf, hb, o, yna, x, mg, w, fg, wrh, wrl, br, dims):
    n, d = x.shape
    cw, mw, nw = dims
    tm = min(TOKEN_TILE, n)
    assert tm % OUT_SUB == 0
    return pl.pallas_call(
        functools.partial(_out_proj_kernel, dims=dims),
        grid=(n // tm,),
        in_specs=[_rows(tm, cw), _rows(tm, mw), _rows(tm, mw), _rows(tm, mw), _rows(tm, nw), _rows(tm, d),
                  _full(mg.shape), _full(w.shape), _full(fg.shape), _full(wrh.shape), _full(wrl.shape),
                  _full(br.shape)],
        out_specs=[_rows(tm, d), _rows(tm * ROW_TILE, LANES), _rows(tm, LANES), _rows(tm, LANES),
                   _full((1, LANES))],
        out_shape=[jax.ShapeDtypeStruct((n, d), F32), jax.ShapeDtypeStruct((n * ROW_TILE, LANES), F32),
                   jax.ShapeDtypeStruct((n, LANES), jnp.int32), jax.ShapeDtypeStruct((n, LANES), F32),
                   jax.ShapeDtypeStruct((1, LANES), F32)],
        scratch_shapes=[pltpu.VMEM((tm, LANES), F32), pltpu.VMEM((1, LANES), F32)],
        compiler_params=_cparams("arbitrary"),
        name="out_proj",
    )(yc, hf, hb, o, yna, x, mg, w, fg, wrh, wrl, br)


def _expert_kernel(be_ref, nb_ref, x_ref, wg_ref, wu_ref, wd_ref, y_ref, wgb_ref, wub_ref, wdb_ref):
    i = pl.program_id(0)

    @pl.when((i == 0) | (be_ref[i] != be_ref[jnp.maximum(i - 1, 0)]))
    def _():
        wgb_ref[...] = wg_ref[0].astype(BF16)
        wub_ref[...] = wu_ref[0].astype(BF16)
        wdb_ref[...] = wd_ref[0].astype(BF16)

    @pl.when(i < nb_ref[0])
    def _():
        bm = x_ref.shape[0] // ROW_TILE
        xb = _load_row_tiles(x_ref, 0, bm).astype(BF16)
        g = jnp.dot(xb, wgb_ref[...], preferred_element_type=F32)
        u = jnp.dot(xb, wub_ref[...], preferred_element_type=F32)
        h = (g * jax.nn.sigmoid(g) * u).astype(BF16)
        _store_row_tiles(y_ref, 0, jnp.dot(h, wdb_ref[...], preferred_element_type=F32))

    @pl.when(i >= nb_ref[0])
    def _():
        y_ref[...] = jnp.zeros_like(y_ref)


def _experts(buf, blk_e, n_used, wg, wu, wd, layer):
    d, de = wg.shape[-2:]
    assert d == ROW_TILE * LANES
    bm = EXPERT_BLOCK
    tb = bm * ROW_TILE
    grid_spec = pltpu.PrefetchScalarGridSpec(
        num_scalar_prefetch=2,
        grid=(buf.shape[0] // tb,),
        in_specs=[pl.BlockSpec((tb, LANES), lambda i, be, nb: (jnp.minimum(i, nb[0] - 1), 0)),
                  pl.BlockSpec((None, 1, d, de), lambda i, be, nb: (layer, be[i], 0, 0)),
                  pl.BlockSpec((None, 1, d, de), lambda i, be, nb: (layer, be[i], 0, 0)),
                  pl.BlockSpec((None, 1, de, d), lambda i, be, nb: (layer, be[i], 0, 0))],
        out_specs=pl.BlockSpec((tb, LANES), lambda i, be, nb: (i, 0)),
        scratch_shapes=[pltpu.VMEM((d, de), BF16), pltpu.VMEM((d, de), BF16), pltpu.VMEM((de, d), BF16)],
    )
    return pl.pallas_call(
        _expert_kernel,
        grid_spec=grid_spec,
        out_shape=jax.ShapeDtypeStruct(buf.shape, F32),
        compiler_params=_cparams("arbitrary"),
        name="experts",
    )(blk_e, n_used, buf, wg, wu, wd)


NA_QROWS = 4
NEG_BIG = -1e30


def _bias_expand_kernel(rpb_ref, selr_ref, selc_ref, o_ref):
    rows = jnp.dot(selr_ref[...], rpb_ref[0], preferred_element_type=F32, precision=lax.Precision.HIGHEST)
    o_ref[0] = jnp.dot(rows, selc_ref[...], preferred_element_type=F32, precision=lax.Precision.HIGHEST)


def _na_bias_table(rpb):
    nh = rpb.shape[0]
    a = np.arange(NA_QROWS)[:, None]
    i = np.arange(3 * NA_QROWS)[None, :]
    half = NA_WIN_ROWS // 2
    a0 = 0 * a
    row_ok = np.stack([(i >= a0 + NA_QROWS) & (i < a0 + NA_QROWS + NA_WIN_ROWS),
                       (i >= a + NA_QROWS - half) & (i < a + NA_QROWS - half + NA_WIN_ROWS),
                       (i >= a0 + 2 * NA_QROWS - NA_WIN_ROWS) & (i < a0 + 2 * NA_QROWS)])
    ro = np.clip(i - NA_QROWS - a + (NA_WIN_ROWS - 1), 0, 2 * NA_WIN_ROWS - 2)
    c = np.arange(GRID_W)
    cs = np.clip(c - NA_WIN_COLS // 2, 0, GRID_W - NA_WIN_COLS)
    col_ok = (c[None, :] >= cs[:, None]) & (c[None, :] < cs[:, None] + NA_WIN_COLS)
    co = np.clip(c[None, :] - c[:, None] + (NA_WIN_COLS - 1), 0, 2 * NA_WIN_COLS - 2)
    nr, nc = 2 * SUBLANES, LANES
    sel_r = (ro.reshape(-1)[:, None] == np.arange(nr)[None, :]).astype(np.float32)
    sel_c = (np.arange(nc)[:, None] == co.reshape(-1)[None, :]).astype(np.float32)
    rpb_p = jnp.pad(rpb.astype(F32), ((0, 0), (0, nr - rpb.shape[1]), (0, nc - rpb.shape[2])))
    nx, ny = sel_r.shape[0], sel_c.shape[1]
    bias = pl.pallas_call(
        _bias_expand_kernel,
        grid=(nh,),
        in_specs=[pl.BlockSpec((1, nr, nc), lambda h: (h, 0, 0)), _full(sel_r.shape), _full(sel_c.shape)],
        out_specs=pl.BlockSpec((1, nx, ny), lambda h: (h, 0, 0)),
        out_shape=jax.ShapeDtypeStruct((nh, nx, ny), F32),
        compiler_params=_cparams("parallel"),
        name="na_bias_expand",
    )(rpb_p, jnp.asarray(sel_r), jnp.asarray(sel_c))
    bias = bias.reshape(nh, NA_QROWS, 3 * NA_QROWS, GRID_W, GRID_W).transpose(0, 1, 3, 2, 4)
    ok = row_ok[:, None, :, None, :, None] & col_ok[None, None, None, :, None, :]
    tab = jnp.where(jnp.asarray(ok), bias[None], NEG_BIG)
    return tab.reshape(3, nh, NA_QROWS * GRID_W, 3 * NA_QROWS * GRID_W)


def _na_kernel(q_ref, kp_ref, kc_ref, kn_ref, vp_ref, vc_ref, vn_ref, b_ref, o_ref):
    q = q_ref[...]
    kcat = jnp.concatenate([kp_ref[...], kc_ref[...], kn_ref[...]], axis=0)
    vcat = jnp.concatenate([vp_ref[...], vc_ref[...], vn_ref[...]], axis=0)
    nw = q.shape[-1]
    hd = nw // NA_HEADS
    lane_head = lax.broadcasted_iota(jnp.int32, q.shape, 1) // hd
    acc = jnp.zeros(q.shape, F32)
    for h in range(NA_HEADS):
        sel = lane_head == h
        qm = jnp.where(sel, q, jnp.zeros_like(q))
        s = lax.dot_general(qm, kcat, (((1,), (1,)), ((), ())), preferred_element_type=F32) + b_ref[0, h]
        m = jnp.max(s, axis=-1, keepdims=True)
        p = jnp.exp(s - m)
        l = jnp.sum(p, axis=-1, keepdims=True)
        oh = jnp.dot(p.astype(BF16), vcat, preferred_element_type=F32)
        acc = jnp.where(sel, oh / l, acc)
    o_ref[...] = acc.astype(o_ref.dtype)


def _neigh_attn_pallas(q, k, v, table, B, T):
    n, nw = q.shape
    tq = NA_QROWS * GRID_W
    J = T // tq
    assert T % tq == 0 and J >= 2 and T // GRID_W >= NA_WIN_ROWS
    blk = (tq, nw)
    cur = pl.BlockSpec(blk, lambda b, j: (b * J + j, 0))
    prev = pl.BlockSpec(blk, lambda b, j: (b * J + jnp.maximum(j - 1, 0), 0))
    nxt = pl.BlockSpec(blk, lambda b, j: (b * J + jnp.minimum(j + 1, J - 1), 0))
    tab = pl.BlockSpec((1,) + table.shape[1:],
                       lambda b, j: (jnp.where(j == 0, 0, jnp.where(j == J - 1, 2, 1)), 0, 0, 0))
    return pl.pallas_call(
        _na_kernel,
        grid=(B, J),
        in_specs=[cur, prev, cur, nxt, prev, cur, nxt, tab],
        out_specs=cur,
        out_shape=jax.ShapeDtypeStruct((n, nw), BF16),
        compiler_params=_cparams("parallel", "arbitrary"),
        name="neigh_attn",
    )(q, k, k, k, v, v, v, table)


HALO = 16
CONV_ROWS = 64


def _local_conv_kernel(u_ref, up_ref, un_ref, qk_ref, qkp_ref, qkn_ref, cw_ref, cb_ref, cg_ref, mw_ref, mb_ref,
                       yc_ref, q_ref, k_ref, wu_ref, wqk_ref, sh_ref, *, k_scale):
    t = pl.program_id(1)
    nt = pl.num_programs(1)
    tm = u_ref.shape[0]
    keep_p = (t > 0).astype(F32)
    keep_n = (t < nt - 1).astype(F32)
    wu_ref[0:HALO, :] = up_ref[...] * keep_p
    wu_ref[HALO:HALO + tm, :] = u_ref[...]
    wu_ref[HALO + tm:, :] = un_ref[...] * keep_n
    wqk_ref[0:HALO, :] = qkp_ref[...] * keep_p
    wqk_ref[HALO:HALO + tm, :] = qk_ref[...]
    wqk_ref[HALO + tm:, :] = qkn_ref[...] * keep_n
    kc = cw_ref.shape[0]
    km = mw_ref.shape[0]
    mw2 = q_ref.shape[-1]
    nsh = sh_ref.shape[1]
    for ph in range(1, SUBLANES):
        sh_ref[ph - 1] = wu_ref[ph:ph + nsh, :]
    for r0 in range(0, tm, CONV_ROWS):
        acc = jnp.zeros((CONV_ROWS, u_ref.shape[-1]), F32) + cb_ref[...]
        for k in range(kc):
            s = r0 + HALO - kc // 2 + k
            ph = s % SUBLANES
            tap = wu_ref[s:s + CONV_ROWS, :] if ph == 0 else sh_ref[ph - 1, s - ph:s - ph + CONV_ROWS, :]
            acc = acc + tap * cw_ref[k:k + 1, :]
        y = acc * lax.rsqrt(jnp.mean(acc * acc, axis=-1, keepdims=True) + EPS) * cg_ref[...]
        yc_ref[r0:r0 + CONV_ROWS, :] = (y * jax.nn.sigmoid(y)).astype(yc_ref.dtype)
        acc = jnp.zeros((CONV_ROWS, qk_ref.shape[-1]), F32) + mb_ref[...]
        for k in range(km):
            s = r0 + HALO - km // 2 + k
            acc = acc + wqk_ref[s:s + CONV_ROWS, :] * mw_ref[k:k + 1, :]
        y = acc * jax.nn.sigmoid(acc)
        q_ref[r0:r0 + CONV_ROWS, :] = y[:, :mw2].astype(q_ref.dtype)
        k_ref[r0:r0 + CONV_ROWS, :] = (y[:, mw2:] * k_scale).astype(k_ref.dtype)


def _local_conv(u, qkpre, cw, cb, cg, mw, mb, B, T):
    n, cwid = u.shape
    qw = qkpre.shape[-1]
    tm = min(TOKEN_TILE, T)
    nt = T // tm
    hb = tm // HALO
    nhb = n // HALO
    assert T % tm == 0 and tm % CONV_ROWS == 0 and CONV_K // 2 <= HALO

    def cur(w):
        return pl.BlockSpec((tm, w), lambda b, t: (b * nt + t, 0))

    def prev(w):
        return pl.BlockSpec((HALO, w), lambda b, t: (jnp.maximum((b * nt + t) * hb - 1, 0), 0))

    def nxt(w):
        return pl.BlockSpec((HALO, w), lambda b, t: (jnp.minimum((b * nt + t + 1) * hb, nhb - 1), 0))

    hd = (qw // 2) // ML_HEADS
    return pl.pallas_call(
        functools.partial(_local_conv_kernel, k_scale=hd ** -0.5),
        grid=(B, nt),
        in_specs=[cur(cwid), prev(cwid), nxt(cwid), cur(qw), prev(qw), nxt(qw),
                  _full(cw.shape), _full(cb.shape), _full(cg.shape), _full(mw.shape), _full(mb.shape)],
        out_specs=[cur(cwid), cur(qw // 2), cur(qw // 2)],
        out_shape=[jax.ShapeDtypeStruct((n, cwid), BF16), jax.ShapeDtypeStruct((n, qw // 2), BF16),
                   jax.ShapeDtypeStruct((n, qw // 2), BF16)],
        scratch_shapes=[pltpu.VMEM((tm + 2 * HALO, cwid), F32), pltpu.VMEM((tm + 2 * HALO, qw), F32),
                        pltpu.VMEM((SUBLANES - 1, tm + 2 * HALO - SUBLANES, cwid), F32)],
        compiler_params=_cparams("parallel", "parallel"),
        name="local_conv",
    )(u, u, u, qkpre, qkpre, qkpre, cw, cb, cg, mw, mb)


def _scan_rows(x, reverse, combine, fill):
    L = x.shape[0]
    row = lax.broadcasted_iota(jnp.int32, x.shape, 0)
    s = 1
    while s < L:
        if reverse:
            x = combine(x, jnp.where(row < L - s, pltpu.roll(x, L - s, axis=0), fill))
        else:
            x = combine(x, jnp.where(row >= s, pltpu.roll(x, s, axis=0), fill))
        s *= 2
    return x


def _mlstm_direction(q_ref, k_ref, v_ref, g_ref, h_ref, c_ref, m_ref, reverse):
    L = q_ref.shape[0]
    hd = q_ref.shape[-1] // ML_HEADS
    ci = 2 * ML_HEADS if reverse else 0
    cf = ci + ML_HEADS
    g = g_ref[...]
    gi = pltpu.roll(g, ML_HEADS, axis=1)
    lf = jnp.minimum(g, 0.0) - jnp.log1p(jnp.exp(-jnp.abs(g)))
    b = _scan_rows(lf, reverse, jnp.add, 0.0)
    r = gi - b
    cm = _scan_rows(r, reverse, jnp.maximum, NEG_BIG)
    total = b[0:1, :] if reverse else b[L - 1:L, :]
    m_prev = m_ref[...]
    inter = b + m_prev
    m_j = jnp.maximum(inter, b + cm)
    a_t = b - m_j
    e_t = jnp.exp(inter - m_j)
    z_t = jnp.exp(-m_j)
    w = total - b + gi
    m_loc = jnp.max(w, axis=0, keepdims=True)
    wa_t = jnp.exp(w - m_loc)
    m_new = jnp.maximum(total + m_prev, m_loc)
    dec = jnp.exp(total + m_prev - m_new)
    inc = jnp.exp(m_loc - m_new)
    m_ref[...] = m_new
    rt = jnp.transpose(r)
    jj = lax.broadcasted_iota(jnp.int32, (L, L), 0)
    ss = lax.broadcasted_iota(jnp.int32, (L, L), 1)
    visible = (ss >= jj) if reverse else (ss <= jj)
    ones = jnp.ones((L, hd), BF16)
    for h in range(ML_HEADS):
        sl = slice(h * hd, (h + 1) * hd)
        col = slice(cf + h, cf + h + 1)
        qh = q_ref[:, sl]
        kh = k_ref[:, sl]
        v_ext = jnp.concatenate([v_ref[:, sl], ones], axis=-1)
        c_prev = c_ref[h]
        pm = jnp.exp(jnp.where(visible, a_t[:, col] + rt[col, :], NEG_BIG))
        qk = lax.dot_general(qh, kh, (((1,), (1,)), ((), ())), preferred_element_type=F32) * pm
        tot = (jnp.dot(qk.astype(BF16), v_ext, preferred_element_type=F32)
               + e_t[:, col] * jnp.dot(qh, c_prev.astype(BF16), preferred_element_type=F32))
        h_ref[:, sl] = tot[:, :hd] / jnp.maximum(jnp.abs(tot[:, hd:]), z_t[:, col])
        ka = (kh.astype(F32) * wa_t[:, col]).astype(BF16)
        s_ext = lax.dot_general(ka, v_ext, (((0,), (0,)), ((), ())), preferred_element_type=F32)
        c_ref[h] = dec[:, col] * c_prev + inc[:, col] * s_ext


def _mlstm_kernel(qf_ref, kf_ref, vf_ref, gf_ref, qb_ref, kb_ref, vb_ref, gb_ref, hf_ref, hb_ref,
                  cf_ref, cb_ref, mf_ref, mb_ref):
    @pl.when(pl.program_id(1) == 0)
    def _():
        cf_ref[...] = jnp.zeros_like(cf_ref)
        cb_ref[...] = jnp.zeros_like(cb_ref)
        mf_ref[...] = jnp.zeros_like(mf_ref)
        mb_ref[...] = jnp.zeros_like(mb_ref)

    _mlstm_direction(qf_ref, kf_ref, vf_ref, gf_ref, hf_ref, cf_ref, mf_ref, False)
    _mlstm_direction(qb_ref, kb_ref, vb_ref, gb_ref, hb_ref, cb_ref, mb_ref, True)


def _mlstm(q, k, v, gates, B, T):
    n, mw = q.shape
    L = ML_CHUNK
    nc = T // L
    hd = mw // ML_HEADS
    assert T % L == 0 and hd == LANES

    def fwd(w):
        return pl.BlockSpec((L, w), lambda b, c: (b * nc + c, 0))

    def bwd(w):
        return pl.BlockSpec((L, w), lambda b, c: (b * nc + nc - 1 - c, 0))

    return pl.pallas_call(
        _mlstm_kernel,
        grid=(B, nc),
        in_specs=[fwd(mw), fwd(mw), fwd(mw), fwd(LANES), bwd(mw), bwd(mw), bwd(mw), bwd(LANES)],
        out_specs=[fwd(mw), bwd(mw)],
        out_shape=[jax.ShapeDtypeStruct((n, mw), F32), jax.ShapeDtypeStruct((n, mw), F32)],
        scratch_shapes=[pltpu.VMEM((ML_HEADS, hd, 2 * hd), F32), pltpu.VMEM((ML_HEADS, hd, 2 * hd), F32),
                        pltpu.VMEM((1, LANES), F32), pltpu.VMEM((1, LANES), F32)],
        compiler_params=_cparams("parallel", "arbitrary"),
        name="mlstm",
    )(q, k, v, gates, q, k, v, gates)


def _mixers(B, T, u, qkpre, v, naq, nak, nav, gates, p):
    yc, q, k = _local_conv(u, qkpre, p['conv_dw_w'], p['conv_dw_b'], p['conv_norm_g'],
                           p['ml_qk_conv_w'], p['ml_qk_conv_b'], B, T)
    hf, hb = _mlstm(q, k, v, gates, B, T)
    y_na = _neigh_attn_pallas(naq, nak, nav, p['na_table'], B, T)
    return yc, hf, hb, y_na


def _first_lane(mask, lane):
    return jnp.min(jnp.where(mask, lane, LANES), axis=-1, keepdims=True)


def _route_tile(lg, ri_ref, rf_ref, cnt_ref, carry_ref):
    @pl.when(pl.program_id(0) == 0)
    def _():
        carry_ref[...] = jnp.zeros_like(carry_ref)

    tm = lg.shape[0]
    lane = lax.broadcasted_iota(jnp.int32, lg.shape, 1)
    gl = jnp.where(lane < N_GROUPS, lg, NEG_BIG)
    gmax = jnp.max(gl, axis=-1, keepdims=True)
    pg = 1.0 / jnp.sum(jnp.exp(gl - gmax), axis=-1, keepdims=True)
    gsel = _first_lane(gl == gmax, lane)
    lo = N_GROUPS + gsel * EXPERTS_PER_GROUP
    in_grp = (lane >= lo) & (lane < lo + EXPERTS_PER_GROUP)
    el = jnp.where(in_grp, lg, NEG_BIG)
    emax = jnp.max(el, axis=-1, keepdims=True)
    esum = jnp.sum(jnp.exp(el - emax), axis=-1, keepdims=True)
    idx1 = _first_lane(in_grp & (el == emax), lane)
    el2 = jnp.where(lane == idx1, NEG_BIG, el)
    emax2 = jnp.max(el2, axis=-1, keepdims=True)
    idx2 = _first_lane(in_grp & (el2 == emax2) & (lane != idx1), lane)
    p1 = 1.0 / esum
    p2 = jnp.exp(emax2 - emax) / esum
    g1 = pg * p1 / (p1 + p2)
    g2 = pg * p2 / (p1 + p2)
    hit1 = lane == idx1
    hit2 = lane == idx2
    onehot = jnp.where(hit1 | hit2, 1.0, 0.0)
    r = lax.broadcasted_iota(jnp.int32, (tm, tm), 0)
    c = lax.broadcasted_iota(jnp.int32, (tm, tm), 1)
    tri = jnp.where(r > c, 1.0, 0.0).astype(BF16)
    prefix = jnp.dot(tri, onehot.astype(BF16), preferred_element_type=F32) + carry_ref[...]
    rank1 = jnp.sum(jnp.where(hit1, prefix, 0.0), axis=-1, keepdims=True).astype(jnp.int32)
    rank2 = jnp.sum(jnp.where(hit2, prefix, 0.0), axis=-1, keepdims=True).astype(jnp.int32)
    carry = carry_ref[...] + jnp.sum(onehot, axis=0, keepdims=True)
    carry_ref[...] = carry
    cnt_ref[...] = carry
    ri_ref[...] = jnp.where(lane == 0, idx1 - N_GROUPS, jnp.where(lane == 1, idx2 - N_GROUPS,
                            jnp.where(lane == 2, rank1, jnp.where(lane == 3, rank2, 0))))
    rf_ref[...] = jnp.where(lane == 0, g1, jnp.where(lane == 1, g2, 0.0))


def _dest_kernel(ri_ref, ps_ref, d_ref):
    ri = ri_ref[...]
    lane = lax.broadcasted_iota(jnp.int32, ri.shape, 1)
    ps = ps_ref[...]
    out = jnp.zeros(ri.shape, jnp.int32)
    for k in range(TOP_K):
        start = jnp.sum(jnp.where(lane == ri[:, k:k + 1], ps, 0.0), axis=-1, keepdims=True).astype(jnp.int32)
        out = jnp.where(lane == k, start + ri[:, TOP_K + k:TOP_K + k + 1], out)
    d_ref[...] = out


def _dest(ri, pstart_row):
    n = ri.shape[0]
    tm = min(4 * TOKEN_TILE, n)
    assert n % tm == 0
    return pl.pallas_call(
        _dest_kernel,
        grid=(n // tm,),
        in_specs=[_rows(tm, LANES), _full((1, LANES))],
        out_specs=_rows(tm, LANES),
        out_shape=jax.ShapeDtypeStruct((n, LANES), jnp.int32),
        compiler_params=_cparams("parallel"),
        name="dest",
    )(ri, pstart_row)


ROW_UNROLL = 8


def _rows_of(ref, first, count):
    return ref.at[pl.ds(pl.multiple_of(first * ROW_TILE, ROW_TILE), count * ROW_TILE), :]


def _row_copy(src, i, dst, j, sem):
    return pltpu.make_async_copy(_rows_of(src, i, 1), _rows_of(dst, j, 1), sem)


def _dispatch_kernel(dest_hbm, pend_ref, x_ref, buf_hbm, dest_smem, zero_ref, sem_idx, sem_rows):
    i = pl.program_id(0)
    tm = x_ref.shape[0] // ROW_TILE
    bm = zero_ref.shape[0] // ROW_TILE

    @pl.when(i == 0)
    def _():
        zero_ref[...] = jnp.zeros_like(zero_ref)

        def last_block(e):
            start = pl.multiple_of(jnp.maximum(pend_ref[e] - bm, 0), bm)
            return pltpu.make_async_copy(zero_ref, _rows_of(buf_hbm, start, bm), sem_rows)

        for e in range(N_EXPERTS):
            last_block(e).start()
        for e in range(N_EXPERTS):
            last_block(e).wait()

        def unused_block(blk, carry):
            copy = pltpu.make_async_copy(zero_ref, _rows_of(buf_hbm, pl.multiple_of(blk * bm, bm), bm), sem_rows)
            copy.start()
            copy.wait()
            return carry

        lax.fori_loop(pend_ref[N_EXPERTS - 1] // bm, buf_hbm.shape[0] // (bm * ROW_TILE), unused_block, 0)

    idx_copy = pltpu.make_async_copy(dest_hbm.at[i], dest_smem, sem_idx)
    idx_copy.start()
    idx_copy.wait()

    def body(o, carry):
        for u in range(ROW_UNROLL):
            r = o * ROW_UNROLL + u
            for k in range(TOP_K):
                _row_copy(x_ref, r, buf_hbm, dest_smem[TOP_K * r + k], sem_rows).start()
        return carry

    lax.fori_loop(0, tm // ROW_UNROLL, body, 0)
    for k in range(TOP_K):
        pltpu.make_async_copy(x_ref, _rows_of(buf_hbm, 0, tm), sem_rows).wait()


def _dispatch(xn, dest, pend, n_rows):
    n = xn.shape[0] // ROW_TILE
    tm = min(MOE_TILE, n)
    assert n % tm == 0
    return pl.pallas_call(
        _dispatch_kernel,
        grid=(n // tm,),
        in_specs=[pl.BlockSpec(memory_space=pl.ANY), pl.BlockSpec(memory_space=pltpu.SMEM),
                  _rows(tm * ROW_TILE, LANES)],
        out_specs=pl.BlockSpec(memory_space=pl.ANY),
        out_shape=jax.ShapeDtypeStruct((n_rows * ROW_TILE, LANES), xn.dtype),
        scratch_shapes=[pltpu.SMEM((TOP_K * tm,), jnp.int32), pltpu.VMEM((EXPERT_BLOCK * ROW_TILE, LANES), xn.dtype),
                        pltpu.SemaphoreType.DMA, pltpu.SemaphoreType.DMA],
        compiler_params=_cparams("arbitrary"),
        name="dispatch",
    )(dest.reshape(n // tm, TOP_K * tm), pend, xn)


def _combine_kernel(dest_hbm, y_hbm, x_ref, rf_ref, o_ref, dest_smem, y0_ref, y1_ref, sem_idx, sem_rows):
    i = pl.program_id(0)
    tm = x_ref.shape[0]
    idx_copy = pltpu.make_async_copy(dest_hbm.at[i], dest_smem, sem_idx)
    idx_copy.start()
    idx_copy.wait()
    ybufs = (y0_ref, y1_ref)

    def body(o, carry):
        for u in range(ROW_UNROLL):
            r = o * ROW_UNROLL + u
            for k in range(TOP_K):
                _row_copy(y_hbm, dest_smem[TOP_K * r + k], ybufs[k], r, sem_rows).start()
        return carry

    lax.fori_loop(0, tm // ROW_UNROLL, body, 0)
    for k in range(TOP_K):
        pltpu.make_async_copy(_rows_of(y_hbm, 0, tm), ybufs[k], sem_rows).wait()
    sub = min(tm, EXPERT_BLOCK)
    for r0 in range(0, tm, sub):
        rs = slice(r0, r0 + sub)
        rf = rf_ref[rs, :]
        o_ref[rs, :] = x_ref[rs, :] + (_load_row_tiles(y0_ref, r0, sub) * rf[:, 0:1]
                                       + _load_row_tiles(y1_ref, r0, sub) * rf[:, 1:2])


def _combine(x1, y, dest, rf):
    n, d = x1.shape
    tm = min(MOE_TILE, n)
    assert n % tm == 0
    return pl.pallas_call(
        _combine_kernel,
        grid=(n // tm,),
        in_specs=[pl.BlockSpec(memory_space=pl.ANY), pl.BlockSpec(memory_space=pl.ANY), _rows(tm, d),
                  _rows(tm, LANES)],
        out_specs=_rows(tm, d),
        out_shape=jax.ShapeDtypeStruct((n, d), x1.dtype),
        scratch_shapes=[pltpu.SMEM((TOP_K * tm,), jnp.int32), pltpu.VMEM((tm * ROW_TILE, LANES), y.dtype),
                        pltpu.VMEM((tm * ROW_TILE, LANES), y.dtype), pltpu.SemaphoreType.DMA,
                        pltpu.SemaphoreType.DMA],
        compiler_params=_cparams("arbitrary"),
        name="combine",
    )(dest.reshape(n // tm, TOP_K * tm), y, x1, rf)


def _moe(x1, xn, ri, rf, cnt, wg, wu, wd, layer):
    n, d = x1.shape
    bm = EXPERT_BLOCK
    counts = cnt[0, N_GROUPS:N_GROUPS + N_EXPERTS].astype(jnp.int32)
    psz = (counts + bm - 1) // bm * bm
    pend = jnp.cumsum(psz)
    pstart = pend - psz
    pstart_row = jnp.pad(pstart.astype(F32), (0, LANES - N_EXPERTS))[None, :]
    dest = _dest(ri, pstart_row)[:, :TOP_K]
    n_blocks = -(-(n * TOP_K) // bm) + N_EXPERTS
    blk_start = jnp.arange(n_blocks, dtype=jnp.int32) * bm
    blk_e = jnp.minimum(jnp.sum(pend[None, :] <= blk_start[:, None], axis=1), N_EXPERTS - 1).astype(jnp.int32)
    n_used = (pend[-1] // bm).astype(jnp.int32).reshape(1)
    buf = _dispatch(xn, dest, pend.astype(jnp.int32), n_blocks * bm)
    y = _experts(buf, blk_e, n_used, wg, wu, wd, layer)
    return x1, y, dest, rf


def _prep_layer(l, P):
    d = P['w_in'].shape[1]
    cw = P['conv_dw_w'].shape[-1]
    mw = P['ml_norm_g'].shape[-1]
    nw = NA_HEADS * P['na_q_norm_g'].shape[-1]
    off_g = 2 * cw + 4 * mw
    ng = 4 * ML_HEADS
    w = P['w_in'][l]
    w_perm = jnp.concatenate([w[:, :off_g], w[:, off_g + ng:], w[:, off_g:off_g + ng],
                              jnp.zeros((d, LANES - ng), F32)], axis=1).astype(BF16)
    nhd = nw // NA_HEADS
    head_id = jnp.arange(nw) // nhd
    bd = jnp.where(head_id[:, None] == head_id[None, :], 1.0 / nhd, 0.0).astype(BF16)
    wr = jnp.concatenate([P['w_router_group'][l], P['w_router_expert'][l],
                          jnp.zeros((d, LANES - N_GROUPS - N_EXPERTS), F32)], axis=1)
    br = jnp.concatenate([P['b_router_group'][l], P['b_router_expert'][l],
                          jnp.zeros((LANES - N_GROUPS - N_EXPERTS,), F32)])[None, :]
    return dict(
        dims=(cw, mw, nw),
        norm_mix_g=P['norm_mix_g'][l][None, :], w_in=w_perm,
        gate_b=jnp.pad(P['ml_gate_b'][l].reshape(-1), (0, LANES - ng))[None, :],
        nq=jnp.tile(P['na_q_norm_g'][l], NA_HEADS)[None, :], nk=jnp.tile(P['na_k_norm_g'][l], NA_HEADS)[None, :],
        bd=bd,
        conv_dw_w=P['conv_dw_w'][l], conv_dw_b=P['conv_dw_b'][l][None, :], conv_norm_g=P['conv_norm_g'][l][None, :],
        ml_qk_conv_w=P['ml_qk_conv_w'][l], ml_qk_conv_b=P['ml_qk_conv_b'][l][None, :],
        ml_norm_g=P['ml_norm_g'][l][None, :], na_table=_na_bias_table(P['na_rpb'][l]),
        w_out=P['w_out'][l].astype(BF16), norm_ffn_g=P['norm_ffn_g'][l][None, :],
        wrh=wr.astype(BF16), wrl=(wr - wr.astype(BF16).astype(F32)).astype(BF16), br=br,
        wg=P['w_exp_gate'], wu=P['w_exp_up'], wd=P['w_exp_down'], layer=l,
    )


def _layer(x, pending, B, T, p):
    dims = p['dims']
    proj_params = (p['norm_mix_g'], p['w_in'], p['gate_b'], p['nq'], p['nk'], p['bd'], dims)
    if pending is None:
        u, qkpre, v, o, naq, nak, nav, gates = _in_proj(x, *proj_params)
    else:
        x, (u, qkpre, v, o, naq, nak, nav, gates) = _combine_in_proj(pending, *proj_params)
    yc, hf, hb, yna = _mixers(B, T, u, qkpre, v, naq, nak, nav, gates, p)
    x1, xn, ri, rf, cnt = _out_proj(yc, hf, hb, o, yna, x, p['ml_norm_g'], p['w_out'], p['norm_ffn_g'], p['wrh'],
                                    p['wrl'], p['br'], dims)
    return _moe(x1, xn, ri, rf, cnt, p['wg'], p['wu'], p['wd'], p['layer'])


def kernel(x_prompt, x_sample, norm_mix_g, w_in, conv_dw_w, conv_dw_b, conv_norm_g, ml_qk_conv_w, ml_qk_conv_b,
           ml_gate_b, ml_norm_g, na_q_norm_g, na_k_norm_g, na_rpb, w_out, norm_ffn_g, w_router_group,
           b_router_group, w_router_expert, b_router_expert, w_exp_gate, w_exp_up, w_exp_down):
    P = dict(norm_mix_g=norm_mix_g, w_in=w_in, conv_dw_w=conv_dw_w, conv_dw_b=conv_dw_b, conv_norm_g=conv_norm_g,
             ml_qk_conv_w=ml_qk_conv_w, ml_qk_conv_b=ml_qk_conv_b, ml_gate_b=ml_gate_b, ml_norm_g=ml_norm_g,
             na_q_norm_g=na_q_norm_g, na_k_norm_g=na_k_norm_g, na_rpb=na_rpb, w_out=w_out, norm_ffn_g=norm_ffn_g,
             w_router_group=w_router_group, b_router_group=b_router_group, w_router_expert=w_router_expert,
             b_router_expert=b_router_expert, w_exp_gate=w_exp_gate, w_exp_up=w_exp_up, w_exp_down=w_exp_down)
    depth = w_in.shape[0]
    d = x_prompt.shape[-1]
    groups = [(x_prompt.shape[0], x_prompt.shape[1], x_prompt.reshape(-1, d)),
              (x_sample.shape[0], x_sample.shape[1], x_sample.reshape(-1, d))]
    pendings = [None] * len(groups)
    for l in range(depth):
        p = _prep_layer(l, P)
        pendings = [_layer(x, pending, B, T, p) for (B, T, x), pending in zip(groups, pendings)]
        groups = [(B, T, None) for B, T, _ in groups]
    return tuple(_combine(*pending).reshape(B, T, d) for (B, T, _), pending in zip(groups, pendings))
```

```python
import functools

import numpy as np

import jax
import jax.numpy as jnp
from jax import lax
from jax.experimental import pallas as pl
from jax.experimental.pallas import tpu as pltpu

F32 = jnp.float32
BF16 = jnp.bfloat16

ML_HEADS = 4
NA_HEADS = 4
CONV_K = 31
ML_CHUNK = 128
GRID_W = 64
NA_WIN_ROWS = 8
NA_WIN_COLS = 16
N_GROUPS = 8
EXPERTS_PER_GROUP = 8
N_EXPERTS = N_GROUPS * EXPERTS_PER_GROUP
TOP_K = 2
EPS = 1e-6

LANES = 128
SUBLANES = 8
VMEM_LIMIT = 48 * 1024 * 1024

TOKEN_TILE = 512
EXPERT_BLOCK = 256
MOE_TILE = 1024


def _cparams(*sem):
    return pltpu.CompilerParams(dimension_semantics=sem, vmem_limit_bytes=VMEM_LIMIT)


def _full(shape):
    return pl.BlockSpec(shape, lambda *_: (0,) * len(shape))


def _rows(tm, width):
    return pl.BlockSpec((tm, width), lambda i: (i, 0))


ROW_TILE = SUBLANES


def _load_row_tiles(ref, r0, rows):
    return jnp.concatenate([ref[pl.ds(r0 * ROW_TILE + s, rows, stride=ROW_TILE), :] for s in range(ROW_TILE)],
                           axis=-1)


def _store_row_tiles(ref, r0, val):
    for s in range(ROW_TILE):
        ref[pl.ds(r0 * ROW_TILE + s, val.shape[0], stride=ROW_TILE), :] = val[:, s * LANES:(s + 1) * LANES]


def _group_mean(zz, bd):
    hi = zz.astype(BF16)
    lo = (zz - hi.astype(F32)).astype(BF16)
    return (jnp.dot(hi, bd, preferred_element_type=F32) + jnp.dot(lo, bd, preferred_element_type=F32))


def _combine_in_proj_kernel(dest_hbm, y_hbm, x1_ref, rf_ref, g_ref, w_ref, gb_ref, nq_ref, nk_ref, bd_ref,
                            x_ref, u_ref, qk_ref, v_ref, o_ref, naq_ref, nak_ref, nav_ref, gt_ref,
                            dest_smem, y_ref, sem_idx, sem_rows, *, dims):
    i = pl.program_id(0)
    nt = pl.num_programs(0)
    tm = x1_ref.shape[0]

    def fetch_rows(step, slot):
        base = slot * TOP_K * tm
        idx_copy = pltpu.make_async_copy(dest_hbm.at[step], dest_smem.at[pl.ds(base, TOP_K * tm)], sem_idx.at[slot])
        idx_copy.start()
        idx_copy.wait()

        def body(o, carry):
            for u in range(ROW_UNROLL):
                r = o * ROW_UNROLL + u
                for k in range(TOP_K):
                    _row_copy(y_hbm, dest_smem[base + TOP_K * r + k], y_ref.at[slot, k], r, sem_rows.at[slot]).start()
            return carry

        lax.fori_loop(0, tm // ROW_UNROLL, body, 0)

    @pl.when(i == 0)
    def _():
        fetch_rows(0, 0)

    @pl.when(i + 1 < nt)
    def _():
        fetch_rows(i + 1, (i + 1) % 2)

    slot = i % 2
    for k in range(TOP_K):
        pltpu.make_async_copy(_rows_of(y_hbm, 0, tm), y_ref.at[slot, k], sem_rows.at[slot]).wait()
    sub = min(tm, EXPERT_BLOCK)
    for r0 in range(0, tm, sub):
        rs = slice(r0, r0 + sub)
        rf = rf_ref[rs, :]
        x_ref[rs, :] = x1_ref[rs, :] + (_load_row_tiles(y_ref.at[slot, 0], r0, sub) * rf[:, 0:1]
                                        + _load_row_tiles(y_ref.at[slot, 1], r0, sub) * rf[:, 1:2])
    _in_proj_kernel(x_ref, g_ref, w_ref, gb_ref, nq_ref, nk_ref, bd_ref,
                    u_ref, qk_ref, v_ref, o_ref, naq_ref, nak_ref, nav_ref, gt_ref, dims=dims)


def _in_proj_kernel(x_ref, g_ref, w_ref, gb_ref, nq_ref, nk_ref, bd_ref,
                    u_ref, qk_ref, v_ref, o_ref, naq_ref, nak_ref, nav_ref, gt_ref, *, dims):
    cw, mw, nw = dims
    x = x_ref[...]
    ms = jnp.mean(x * x, axis=-1, keepdims=True)
    xn = (x * lax.rsqrt(ms + EPS) * g_ref[...]).astype(BF16)

    def proj(lo, width):
        return jnp.dot(xn, w_ref[:, lo:lo + width], preferred_element_type=F32)

    zc = proj(0, 2 * cw)
    u_ref[...] = zc[:, :cw] * jax.nn.sigmoid(zc[:, cw:])
    off = 2 * cw
    qk_ref[...] = proj(off, 2 * mw)
    v_ref[...] = proj(off + 2 * mw, mw).astype(BF16)
    o_ref[...] = proj(off + 3 * mw, mw)
    off += 4 * mw
    bd = bd_ref[...]
    zq = proj(off, nw)
    zk = proj(off + nw, nw)
    hd = nw // NA_HEADS
    naq_ref[...] = (zq * lax.rsqrt(_group_mean(zq * zq, bd) + EPS) * nq_ref[...] * (hd ** -0.5)).astype(BF16)
    nak_ref[...] = (zk * lax.rsqrt(_group_mean(zk * zk, bd) + EPS) * nk_ref[...]).astype(BF16)
    nav_ref[...] = proj(off + 2 * nw, nw).astype(BF16)
    off += 3 * nw
    gt_ref[...] = proj(off, LANES) + gb_ref[...]


def _combine_in_proj(pending, g, w, gate_b, nq, nk, bd, dims):
    x1, y, dest, rf = pending
    n, d = x1.shape
    cw, mw, nw = dims
    tm = min(TOKEN_TILE, n)
    outs = [(d, F32), (cw, F32), (2 * mw, F32), (mw, BF16), (mw, F32), (nw, BF16), (nw, BF16), (nw, BF16),
            (LANES, F32)]
    res = pl.pallas_call(
        functools.partial(_combine_in_proj_kernel, dims=dims),
        grid=(n // tm,),
        in_specs=[pl.BlockSpec(memory_space=pl.ANY), pl.BlockSpec(memory_space=pl.ANY), _rows(tm, d),
                  _rows(tm, LANES), _full(g.shape), pl.BlockSpec(w.shape, lambda i: (0, 0), pipeline_mode=pl.Buffered(1)),
                  _full(gate_b.shape), _full(nq.shape), _full(nk.shape), _full(bd.shape)],
        out_specs=[_rows(tm, wd) for wd, _ in outs],
        out_shape=[jax.ShapeDtypeStruct((n, wd), dt) for wd, dt in outs],
        scratch_shapes=[pltpu.SMEM((2 * TOP_K * tm,), jnp.int32),
                        pltpu.VMEM((2, TOP_K, tm * ROW_TILE, LANES), y.dtype),
                        pltpu.SemaphoreType.DMA((2,)), pltpu.SemaphoreType.DMA((2,))],
        compiler_params=_cparams("arbitrary"),
        name="combine_in_proj",
    )(dest.reshape(n // tm, TOP_K * tm), y, x1, rf, g, w, gate_b, nq, nk, bd)
    return res[0], res[1:]


def _in_proj(x, g, w, gate_b, nq, nk, bd, dims):
    n, d = x.shape
    cw, mw, nw = dims
    tm = min(TOKEN_TILE, n)
    outs = [(cw, F32), (2 * mw, F32), (mw, BF16), (mw, F32), (nw, BF16), (nw, BF16), (nw, BF16), (LANES, F32)]
    return pl.pallas_call(
        functools.partial(_in_proj_kernel, dims=dims),
        grid=(n // tm,),
        in_specs=[_rows(tm, d), _full(g.shape), _full(w.shape), _full(gate_b.shape),
                  _full(nq.shape), _full(nk.shape), _full(bd.shape)],
        out_specs=[_rows(tm, wd) for wd, _ in outs],
        out_shape=[jax.ShapeDtypeStruct((n, wd), dt) for wd, dt in outs],
        compiler_params=_cparams("parallel"),
        name="in_proj",
    )(x, g, w, gate_b, nq, nk, bd)


OUT_SUB = 128


def _out_proj_kernel(yc_ref, hf_ref, hb_ref, o_ref, yna_ref, x_ref, mg_ref, w_ref, fg_ref, wrh_ref, wrl_ref, br_ref,
                     x1_ref, xn_ref, ri_ref, rf_ref, cnt_ref, lg_ref, carry_ref, *, dims):
    cw, mw, nw = dims
    hd = mw // ML_HEADS
    for r0 in range(0, x_ref.shape[0], OUT_SUB):
        rs = slice(r0, r0 + OUT_SUB)
        hm = hf_ref[rs, :] + hb_ref[rs, :]
        parts = []
        for h in range(ML_HEADS):
            seg = hm[:, h * hd:(h + 1) * hd]
            parts.append(seg * lax.rsqrt(jnp.mean(seg * seg, axis=-1, keepdims=True) + EPS))
        yml = jnp.concatenate(parts, axis=-1) * mg_ref[...] * jax.nn.sigmoid(o_ref[rs, :])
        y = jnp.dot(yc_ref[rs, :], w_ref[0:cw, :], preferred_element_type=F32)
        y += jnp.dot(yml.astype(BF16), w_ref[cw:cw + mw, :], preferred_element_type=F32)
        y += jnp.dot(yna_ref[rs, :], w_ref[cw + mw:cw + mw + nw, :], preferred_element_type=F32)
        x1 = x_ref[rs, :] + y
        x1_ref[rs, :] = x1
        xn = x1 * lax.rsqrt(jnp.mean(x1 * x1, axis=-1, keepdims=True) + EPS) * fg_ref[...]
        _store_row_tiles(xn_ref, r0, xn)
        xh = xn.astype(BF16)
        xl = (xn - xh.astype(F32)).astype(BF16)
        lg = jnp.dot(xh, wrh_ref[...], preferred_element_type=F32)
        lg += jnp.dot(xl, wrh_ref[...], preferred_element_type=F32)
        lg += jnp.dot(xh, wrl_ref[...], preferred_element_type=F32)
        lg_ref[rs, :] = lg + br_ref[...]
    _route_tile(lg_ref[...], ri_ref, rf_ref, cnt_ref, carry_ref)


def _out_proj(yc, hf, hb, o, yna, x, mg, w, fg, wrh, wrl, br, dims):
    n, d = x.shape
    cw, mw, nw = dims
    tm = min(TOKEN_TILE, n)
    assert tm % OUT_SUB == 0
    return pl.pallas_call(
        functools.partial(_out_proj_kernel, dims=dims),
        grid=(n // tm,),
        in_specs=[_rows(tm, cw), _rows(tm, mw), _rows(tm, mw), _rows(tm, mw), _rows(tm, nw), _rows(tm, d),
                  _full(mg.shape), _full(w.shape), _full(fg.shape), _full(wrh.shape), _full(wrl.shape),
                  _full(br.shape)],
        out_specs=[_rows(tm, d), _rows(tm * ROW_TILE, LANES), _rows(tm, LANES), _rows(tm, LANES),
                   _full((1, LANES))],
        out_shape=[jax.ShapeDtypeStruct((n, d), F32), jax.ShapeDtypeStruct((n * ROW_TILE, LANES), F32),
                   jax.ShapeDtypeStruct((n, LANES), jnp.int32), jax.ShapeDtypeStruct((n, LANES), F32),
                   jax.ShapeDtypeStruct((1, LANES), F32)],
        scratch_shapes=[pltpu.VMEM((tm, LANES), F32), pltpu.VMEM((1, LANES), F32)],
        compiler_params=_cparams("arbitrary"),
        name="out_proj",
    )(yc, hf, hb, o, yna, x, mg, w, fg, wrh, wrl, br)


def _expert_kernel(be_ref, nb_ref, x_ref, wg_ref, wu_ref, wd_ref, y_ref, wgb_ref, wub_ref, wdb_ref):
    i = pl.program_id(0)

    @pl.when((i == 0) | (be_ref[i] != be_ref[jnp.maximum(i - 1, 0)]))
    def _():
        wgb_ref[...] = wg_ref[0].astype(BF16)
        wub_ref[...] = wu_ref[0].astype(BF16)
        wdb_ref[...] = wd_ref[0].astype(BF16)

    @pl.when(i < nb_ref[0])
    def _():
        bm = x_ref.shape[0] // ROW_TILE
        xb = _load_row_tiles(x_ref, 0, bm).astype(BF16)
        g = jnp.dot(xb, wgb_ref[...], preferred_element_type=F32)
        u = jnp.dot(xb, wub_ref[...], preferred_element_type=F32)
        h = (g * jax.nn.sigmoid(g) * u).astype(BF16)
        _store_row_tiles(y_ref, 0, jnp.dot(h, wdb_ref[...], preferred_element_type=F32))

    @pl.when(i >= nb_ref[0])
    def _():
        y_ref[...] = jnp.zeros_like(y_ref)


def _experts(buf, blk_e, n_used, wg, wu, wd, layer):
    d, de = wg.shape[-2:]
    assert d == ROW_TILE * LANES
    bm = EXPERT_BLOCK
    tb = bm * ROW_TILE
    grid_spec = pltpu.PrefetchScalarGridSpec(
        num_scalar_prefetch=2,
        grid=(buf.shape[0] // tb,),
        in_specs=[pl.BlockSpec((tb, LANES), lambda i, be, nb: (jnp.minimum(i, nb[0] - 1), 0)),
                  pl.BlockSpec((None, 1, d, de), lambda i, be, nb: (layer, be[i], 0, 0)),
                  pl.BlockSpec((None, 1, d, de), lambda i, be, nb: (layer, be[i], 0, 0)),
                  pl.BlockSpec((None, 1, de, d), lambda i, be, nb: (layer, be[i], 0, 0))],
        out_specs=pl.BlockSpec((tb, LANES), lambda i, be, nb: (i, 0)),
        scratch_shapes=[pltpu.VMEM((d, de), BF16), pltpu.VMEM((d, de), BF16), pltpu.VMEM((de, d), BF16)],
    )
    return pl.pallas_call(
        _expert_kernel,
        grid_spec=grid_spec,
        out_shape=jax.ShapeDtypeStruct(buf.shape, F32),
        compiler_params=_cparams("arbitrary"),
        name="experts",
    )(blk_e, n_used, buf, wg, wu, wd)


NA_QROWS = 4
NEG_BIG = -1e30


def _bias_expand_kernel(rpb_ref, selr_ref, selc_ref, o_ref):
    rows = jnp.dot(selr_ref[...], rpb_ref[0], preferred_element_type=F32, precision=lax.Precision.HIGHEST)
    o_ref[0] = jnp.dot(rows, selc_ref[...], preferred_element_type=F32, precision=lax.Precision.HIGHEST)


def _na_bias_table(rpb):
    nh = rpb.shape[0]
    a = np.arange(NA_QROWS)[:, None]
    i = np.arange(3 * NA_QROWS)[None, :]
    half = NA_WIN_ROWS // 2
    a0 = 0 * a
    row_ok = np.stack([(i >= a0 + NA_QROWS) & (i < a0 + NA_QROWS + NA_WIN_ROWS),
                       (i >= a + NA_QROWS - half) & (i < a + NA_QROWS - half + NA_WIN_ROWS),
                       (i >= a0 + 2 * NA_QROWS - NA_WIN_ROWS) & (i < a0 + 2 * NA_QROWS)])
    ro = np.clip(i - NA_QROWS - a + (NA_WIN_ROWS - 1), 0, 2 * NA_WIN_ROWS - 2)
    c = np.arange(GRID_W)
    cs = np.clip(c - NA_WIN_COLS // 2, 0, GRID_W - NA_WIN_COLS)
    col_ok = (c[None, :] >= cs[:, None]) & (c[None, :] < cs[:, None] + NA_WIN_COLS)
    co = np.clip(c[None, :] - c[:, None] + (NA_WIN_COLS - 1), 0, 2 * NA_WIN_COLS - 2)
    nr, nc = 2 * SUBLANES, LANES
    sel_r = (ro.reshape(-1)[:, None] == np.arange(nr)[None, :]).astype(np.float32)
    sel_c = (np.arange(nc)[:, None] == co.reshape(-1)[None, :]).astype(np.float32)
    rpb_p = jnp.pad(rpb.astype(F32), ((0, 0), (0, nr - rpb.shape[1]), (0, nc - rpb.shape[2])))
    nx, ny = sel_r.shape[0], sel_c.shape[1]
    bias = pl.pallas_call(
        _bias_expand_kernel,
        grid=(nh,),
        in_specs=[pl.BlockSpec((1, nr, nc), lambda h: (h, 0, 0)), _full(sel_r.shape), _full(sel_c.shape)],
        out_specs=pl.BlockSpec((1, nx, ny), lambda h: (h, 0, 0)),
        out_shape=jax.ShapeDtypeStruct((nh, nx, ny), F32),
        compiler_params=_cparams("parallel"),
        name="na_bias_expand",
    )(rpb_p, jnp.asarray(sel_r), jnp.asarray(sel_c))
    bias = bias.reshape(nh, NA_QROWS, 3 * NA_QROWS, GRID_W, GRID_W).transpose(0, 1, 3, 2, 4)
    ok = row_ok[:, None, :, None, :, None] & col_ok[None, None, None, :, None, :]
    tab = jnp.where(jnp.asarray(ok), bias[None], NEG_BIG)
    return tab.reshape(3, nh, NA_QROWS * GRID_W, 3 * NA_QROWS * GRID_W)


def _na_kernel(q_ref, kp_ref, kc_ref, kn_ref, vp_ref, vc_ref, vn_ref, b_ref, o_ref):
    q = q_ref[...]
    kcat = jnp.concatenate([kp_ref[...], kc_ref[...], kn_ref[...]], axis=0)
    vcat = jnp.concatenate([vp_ref[...], vc_ref[...], vn_ref[...]], axis=0)
    nw = q.shape[-1]
    hd = nw // NA_HEADS
    lane_head = lax.broadcasted_iota(jnp.int32, q.shape, 1) // hd
    acc = jnp.zeros(q.shape, F32)
    for h in range(NA_HEADS):
        sel = lane_head == h
        qm = jnp.where(sel, q, jnp.zeros_like(q))
        s = lax.dot_general(qm, kcat, (((1,), (1,)), ((), ())), preferred_element_type=F32) + b_ref[0, h]
        m = jnp.max(s, axis=-1, keepdims=True)
        p = jnp.exp(s - m)
        l = jnp.sum(p, axis=-1, keepdims=True)
        oh = jnp.dot(p.astype(BF16), vcat, preferred_element_type=F32)
        acc = jnp.where(sel, oh / l, acc)
    o_ref[...] = acc.astype(o_ref.dtype)


def _neigh_attn_pallas(q, k, v, table, B, T):
    n, nw = q.shape
    tq = NA_QROWS * GRID_W
    J = T // tq
    assert T % tq == 0 and J >= 2 and T // GRID_W >= NA_WIN_ROWS
    blk = (tq, nw)
    cur = pl.BlockSpec(blk, lambda b, j: (b * J + j, 0))
    prev = pl.BlockSpec(blk, lambda b, j: (b * J + jnp.maximum(j - 1, 0), 0))
    nxt = pl.BlockSpec(blk, lambda b, j: (b * J + jnp.minimum(j + 1, J - 1), 0))
    tab = pl.BlockSpec((1,) + table.shape[1:],
                       lambda b, j: (jnp.where(j == 0, 0, jnp.where(j == J - 1, 2, 1)), 0, 0, 0))
    return pl.pallas_call(
        _na_kernel,
        grid=(B, J),
        in_specs=[cur, prev, cur, nxt, prev, cur, nxt, tab],
        out_specs=cur,
        out_shape=jax.ShapeDtypeStruct((n, nw), BF16),
        compiler_params=_cparams("parallel", "arbitrary"),
        name="neigh_attn",
    )(q, k, k, k, v, v, v, table)


HALO = 16
CONV_ROWS = 64


def _local_conv_kernel(u_ref, up_ref, un_ref, qk_ref, qkp_ref, qkn_ref, cw_ref, cb_ref, cg_ref, mw_ref, mb_ref,
                       yc_ref, q_ref, k_ref, wu_ref, wqk_ref, sh_ref, *, k_scale):
    t = pl.program_id(1)
    nt = pl.num_programs(1)
    tm = u_ref.shape[0]
    keep_p = (t > 0).astype(F32)
    keep_n = (t < nt - 1).astype(F32)
    wu_ref[0:HALO, :] = up_ref[...] * keep_p
    wu_ref[HALO:HALO + tm, :] = u_ref[...]
    wu_ref[HALO + tm:, :] = un_ref[...] * keep_n
    wqk_ref[0:HALO, :] = qkp_ref[...] * keep_p
    wqk_ref[HALO:HALO + tm, :] = qk_ref[...]
    wqk_ref[HALO + tm:, :] = qkn_ref[...] * keep_n
    kc = cw_ref.shape[0]
    km = mw_ref.shape[0]
    mw2 = q_ref.shape[-1]
    nsh = sh_ref.shape[1]
    for ph in range(1, SUBLANES):
        sh_ref[ph - 1] = wu_ref[ph:ph + nsh, :]
    for r0 in range(0, tm, CONV_ROWS):
        acc = jnp.zeros((CONV_ROWS, u_ref.shape[-1]), F32) + cb_ref[...]
        for k in range(kc):
            s = r0 + HALO - kc // 2 + k
            ph = s % SUBLANES
            tap = wu_ref[s:s + CONV_ROWS, :] if ph == 0 else sh_ref[ph - 1, s - ph:s - ph + CONV_ROWS, :]
            acc = acc + tap * cw_ref[k:k + 1, :]
        y = acc * lax.rsqrt(jnp.mean(acc * acc, axis=-1, keepdims=True) + EPS) * cg_ref[...]
        yc_ref[r0:r0 + CONV_ROWS, :] = (y * jax.nn.sigmoid(y)).astype(yc_ref.dtype)
        acc = jnp.zeros((CONV_ROWS, qk_ref.shape[-1]), F32) + mb_ref[...]
        for k in range(km):
            s = r0 + HALO - km // 2 + k
            acc = acc + wqk_ref[s:s + CONV_ROWS, :] * mw_ref[k:k + 1, :]
        y = acc * jax.nn.sigmoid(acc)
        q_ref[r0:r0 + CONV_ROWS, :] = y[:, :mw2].astype(q_ref.dtype)
        k_ref[r0:r0 + CONV_ROWS, :] = (y[:, mw2:] * k_scale).astype(k_ref.dtype)


def _local_conv(u, qkpre, cw, cb, cg, mw, mb, B, T):
    n, cwid = u.shape
    qw = qkpre.shape[-1]
    tm = min(TOKEN_TILE, T)
    nt = T // tm
    hb = tm // HALO
    nhb = n // HALO
    assert T % tm == 0 and tm % CONV_ROWS == 0 and CONV_K // 2 <= HALO

    def cur(w):
        return pl.BlockSpec((tm, w), lambda b, t: (b * nt + t, 0))

    def prev(w):
        return pl.BlockSpec((HALO, w), lambda b, t: (jnp.maximum((b * nt + t) * hb - 1, 0), 0))

    def nxt(w):
        return pl.BlockSpec((HALO, w), lambda b, t: (jnp.minimum((b * nt + t + 1) * hb, nhb - 1), 0))

    hd = (qw // 2) // ML_HEADS
    return pl.pallas_call(
        functools.partial(_local_conv_kernel, k_scale=hd ** -0.5),
        grid=(B, nt),
        in_specs=[cur(cwid), prev(cwid), nxt(cwid), cur(qw), prev(qw), nxt(qw),
                  _full(cw.shape), _full(cb.shape), _full(cg.shape), _full(mw.shape), _full(mb.shape)],
        out_specs=[cur(cwid), cur(qw // 2), cur(qw // 2)],
        out_shape=[jax.ShapeDtypeStruct((n, cwid), BF16), jax.ShapeDtypeStruct((n, qw // 2), BF16),
                   jax.ShapeDtypeStruct((n, qw // 2), BF16)],
        scratch_shapes=[pltpu.VMEM((tm + 2 * HALO, cwid), F32), pltpu.VMEM((tm + 2 * HALO, qw), F32),
                        pltpu.VMEM((SUBLANES - 1, tm + 2 * HALO - SUBLANES, cwid), F32)],
        compiler_params=_cparams("parallel", "parallel"),
        name="local_conv",
    )(u, u, u, qkpre, qkpre, qkpre, cw, cb, cg, mw, mb)


def _scan_rows(x, reverse, combine, fill):
    L = x.shape[0]
    row = lax.broadcasted_iota(jnp.int32, x.shape, 0)
    s = 1
    while s < L:
        if reverse:
            x = combine(x, jnp.where(row < L - s, pltpu.roll(x, L - s, axis=0), fill))
        else:
            x = combine(x, jnp.where(row >= s, pltpu.roll(x, s, axis=0), fill))
        s *= 2
    return x


def _mlstm_direction(q_ref, k_ref, v_ref, g_ref, h_ref, c_ref, m_ref, reverse):
    L = q_ref.shape[0]
    hd = q_ref.shape[-1] // ML_HEADS
    ci = 2 * ML_HEADS if reverse else 0
    cf = ci + ML_HEADS
    g = g_ref[...]
    gi = pltpu.roll(g, ML_HEADS, axis=1)
    lf = jnp.minimum(g, 0.0) - jnp.log1p(jnp.exp(-jnp.abs(g)))
    b = _scan_rows(lf, reverse, jnp.add, 0.0)
    r = gi - b
    cm = _scan_rows(r, reverse, jnp.maximum, NEG_BIG)
    total = b[0:1, :] if reverse else b[L - 1:L, :]
    m_prev = m_ref[...]
    inter = b + m_prev
    m_j = jnp.maximum(inter, b + cm)
    a_t = b - m_j
    e_t = jnp.exp(inter - m_j)
    z_t = jnp.exp(-m_j)
    w = total - b + gi
    m_loc = jnp.max(w, axis=0, keepdims=True)
    wa_t = jnp.exp(w - m_loc)
    m_new = jnp.maximum(total + m_prev, m_loc)
    dec = jnp.exp(total + m_prev - m_new)
    inc = jnp.exp(m_loc - m_new)
    m_ref[...] = m_new
    rt = jnp.transpose(r)
    jj = lax.broadcasted_iota(jnp.int32, (L, L), 0)
    ss = lax.broadcasted_iota(jnp.int32, (L, L), 1)
    visible = (ss >= jj) if reverse else (ss <= jj)
    ones = jnp.ones((L, hd), BF16)
    for h in range(ML_HEADS):
        sl = slice(h * hd, (h + 1) * hd)
        col = slice(cf + h, cf + h + 1)
        qh = q_ref[:, sl]
        kh = k_ref[:, sl]
        v_ext = jnp.concatenate([v_ref[:, sl], ones], axis=-1)
        c_prev = c_ref[h]
        pm = jnp.exp(jnp.where(visible, a_t[:, col] + rt[col, :], NEG_BIG))
        qk = lax.dot_general(qh, kh, (((1,), (1,)), ((), ())), preferred_element_type=F32) * pm
        tot = (jnp.dot(qk.astype(BF16), v_ext, preferred_element_type=F32)
               + e_t[:, col] * jnp.dot(qh, c_prev.astype(BF16), preferred_element_type=F32))
        h_ref[:, sl] = tot[:, :hd] / jnp.maximum(jnp.abs(tot[:, hd:]), z_t[:, col])
        ka = (kh.astype(F32) * wa_t[:, col]).astype(BF16)
        s_ext = lax.dot_general(ka, v_ext, (((0,), (0,)), ((), ())), preferred_element_type=F32)
        c_ref[h] = dec[:, col] * c_prev + inc[:, col] * s_ext


def _mlstm_kernel(qf_ref, kf_ref, vf_ref, gf_ref, qb_ref, kb_ref, vb_ref, gb_ref, hf_ref, hb_ref,
                  cf_ref, cb_ref, mf_ref, mb_ref):
    @pl.when(pl.program_id(1) == 0)
    def _():
        cf_ref[...] = jnp.zeros_like(cf_ref)
        cb_ref[...] = jnp.zeros_like(cb_ref)
        mf_ref[...] = jnp.zeros_like(mf_ref)
        mb_ref[...] = jnp.zeros_like(mb_ref)

    L = ML_CHUNK

    def chunk(ref, j):
        return ref.at[pl.ds(j * L, L), :]

    for j in range(ML_STEP_CHUNKS):
        jb = ML_STEP_CHUNKS - 1 - j
        _mlstm_direction(chunk(qf_ref, j), chunk(kf_ref, j), chunk(vf_ref, j), chunk(gf_ref, j), chunk(hf_ref, j),
                         cf_ref, mf_ref, False)
        _mlstm_direction(chunk(qb_ref, jb), chunk(kb_ref, jb), chunk(vb_ref, jb), chunk(gb_ref, jb),
                         chunk(hb_ref, jb), cb_ref, mb_ref, True)


ML_STEP_CHUNKS = 2


def _mlstm(q, k, v, gates, B, T):
    n, mw = q.shape
    L = ML_CHUNK * ML_STEP_CHUNKS
    nc = T // L
    hd = mw // ML_HEADS
    assert T % L == 0 and hd == LANES

    def fwd(w):
        return pl.BlockSpec((L, w), lambda b, c: (b * nc + c, 0))

    def bwd(w):
        return pl.BlockSpec((L, w), lambda b, c: (b * nc + nc - 1 - c, 0))

    return pl.pallas_call(
        _mlstm_kernel,
        grid=(B, nc),
        in_specs=[fwd(mw), fwd(mw), fwd(mw), fwd(LANES), bwd(mw), bwd(mw), bwd(mw), bwd(LANES)],
        out_specs=[fwd(mw), bwd(mw)],
        out_shape=[jax.ShapeDtypeStruct((n, mw), F32), jax.ShapeDtypeStruct((n, mw), F32)],
        scratch_shapes=[pltpu.VMEM((ML_HEADS, hd, 2 * hd), F32), pltpu.VMEM((ML_HEADS, hd, 2 * hd), F32),
                        pltpu.VMEM((1, LANES), F32), pltpu.VMEM((1, LANES), F32)],
        compiler_params=_cparams("parallel", "arbitrary"),
        name="mlstm",
    )(q, k, v, gates, q, k, v, gates)


def _mixers(B, T, u, qkpre, v, naq, nak, nav, gates, p):
    yc, q, k = _local_conv(u, qkpre, p['conv_dw_w'], p['conv_dw_b'], p['conv_norm_g'],
                           p['ml_qk_conv_w'], p['ml_qk_conv_b'], B, T)
    hf, hb = _mlstm(q, k, v, gates, B, T)
    y_na = _neigh_attn_pallas(naq, nak, nav, p['na_table'], B, T)
    return yc, hf, hb, y_na


def _first_lane(mask, lane):
    return jnp.min(jnp.where(mask, lane, LANES), axis=-1, keepdims=True)


def _route_tile(lg, ri_ref, rf_ref, cnt_ref, carry_ref):
    @pl.when(pl.program_id(0) == 0)
    def _():
        carry_ref[...] = jnp.zeros_like(carry_ref)

    tm = lg.shape[0]
    lane = lax.broadcasted_iota(jnp.int32, lg.shape, 1)
    gl = jnp.where(lane < N_GROUPS, lg, NEG_BIG)
    gmax = jnp.max(gl, axis=-1, keepdims=True)
    pg = 1.0 / jnp.sum(jnp.exp(gl - gmax), axis=-1, keepdims=True)
    gsel = _first_lane(gl == gmax, lane)
    lo = N_GROUPS + gsel * EXPERTS_PER_GROUP
    in_grp = (lane >= lo) & (lane < lo + EXPERTS_PER_GROUP)
    el = jnp.where(in_grp, lg, NEG_BIG)
    emax = jnp.max(el, axis=-1, keepdims=True)
    esum = jnp.sum(jnp.exp(el - emax), axis=-1, keepdims=True)
    idx1 = _first_lane(in_grp & (el == emax), lane)
    el2 = jnp.where(lane == idx1, NEG_BIG, el)
    emax2 = jnp.max(el2, axis=-1, keepdims=True)
    idx2 = _first_lane(in_grp & (el2 == emax2) & (lane != idx1), lane)
    p1 = 1.0 / esum
    p2 = jnp.exp(emax2 - emax) / esum
    g1 = pg * p1 / (p1 + p2)
    g2 = pg * p2 / (p1 + p2)
    hit1 = lane == idx1
    hit2 = lane == idx2
    onehot = jnp.where(hit1 | hit2, 1.0, 0.0)
    r = lax.broadcasted_iota(jnp.int32, (tm, tm), 0)
    c = lax.broadcasted_iota(jnp.int32, (tm, tm), 1)
    tri = jnp.where(r > c, 1.0, 0.0).astype(BF16)
    prefix = jnp.dot(tri, onehot.astype(BF16), preferred_element_type=F32) + carry_ref[...]
    rank1 = jnp.sum(jnp.where(hit1, prefix, 0.0), axis=-1, keepdims=True).astype(jnp.int32)
    rank2 = jnp.sum(jnp.where(hit2, prefix, 0.0), axis=-1, keepdims=True).astype(jnp.int32)
    carry = carry_ref[...] + jnp.sum(onehot, axis=0, keepdims=True)
    carry_ref[...] = carry
    cnt_ref[...] = carry
    ri_ref[...] = jnp.where(lane == 0, idx1 - N_GROUPS, jnp.where(lane == 1, idx2 - N_GROUPS,
                            jnp.where(lane == 2, rank1, jnp.where(lane == 3, rank2, 0))))
    rf_ref[...] = jnp.where(lane == 0, g1, jnp.where(lane == 1, g2, 0.0))


def _dest_kernel(ri_ref, ps_ref, d_ref):
    ri = ri_ref[...]
    lane = lax.broadcasted_iota(jnp.int32, ri.shape, 1)
    ps = ps_ref[...]
    out = jnp.zeros(ri.shape, jnp.int32)
    for k in range(TOP_K):
        start = jnp.sum(jnp.where(lane == ri[:, k:k + 1], ps, 0.0), axis=-1, keepdims=True).astype(jnp.int32)
        out = jnp.where(lane == k, start + ri[:, TOP_K + k:TOP_K + k + 1], out)
    d_ref[...] = out


def _dest(ri, pstart_row):
    n = ri.shape[0]
    tm = min(4 * TOKEN_TILE, n)
    assert n % tm == 0
    return pl.pallas_call(
        _dest_kernel,
        grid=(n // tm,),
        in_specs=[_rows(tm, LANES), _full((1, LANES))],
        out_specs=_rows(tm, LANES),
        out_shape=jax.ShapeDtypeStruct((n, LANES), jnp.int32),
        compiler_params=_cparams("parallel"),
        name="dest",
    )(ri, pstart_row)


ROW_UNROLL = 8


def _rows_of(ref, first, count):
    return ref.at[pl.ds(pl.multiple_of(first * ROW_TILE, ROW_TILE), count * ROW_TILE), :]


def _row_copy(src, i, dst, j, sem):
    return pltpu.make_async_copy(_rows_of(src, i, 1), _rows_of(dst, j, 1), sem)


def _prefetched_indices(dest_hbm, dest_smem, sem_idx, width):
    i = pl.program_id(0)

    def copy(step):
        half = step % 2
        return pltpu.make_async_copy(dest_hbm.at[step], dest_smem.at[pl.ds(half * width, width)], sem_idx.at[half])

    @pl.when(i == 0)
    def _():
        copy(0).start()

    copy(i).wait()

    @pl.when(i + 1 < pl.num_programs(0))
    def _():
        copy(i + 1).start()

    return (i % 2) * width


def _dispatch_kernel(dest_hbm, pend_ref, x_ref, buf_hbm, dest_smem, zero_ref, sem_idx, sem_rows):
    i = pl.program_id(0)
    tm = x_ref.shape[0] // ROW_TILE
    bm = zero_ref.shape[0] // ROW_TILE

    @pl.when(i == 0)
    def _():
        zero_ref[...] = jnp.zeros_like(zero_ref)

        def last_block(e):
            start = pl.multiple_of(jnp.maximum(pend_ref[e] - bm, 0), bm)
            return pltpu.make_async_copy(zero_ref, _rows_of(buf_hbm, start, bm), sem_rows)

        for e in range(N_EXPERTS):
            last_block(e).start()
        for e in range(N_EXPERTS):
            last_block(e).wait()

        def unused_block(blk, carry):
            copy = pltpu.make_async_copy(zero_ref, _rows_of(buf_hbm, pl.multiple_of(blk * bm, bm), bm), sem_rows)
            copy.start()
            copy.wait()
            return carry

        lax.fori_loop(pend_ref[N_EXPERTS - 1] // bm, buf_hbm.shape[0] // (bm * ROW_TILE), unused_block, 0)

    base = _prefetched_indices(dest_hbm, dest_smem, sem_idx, TOP_K * tm)

    def body(o, carry):
        for u in range(ROW_UNROLL):
            r = o * ROW_UNROLL + u
            for k in range(TOP_K):
                _row_copy(x_ref, r, buf_hbm, dest_smem[base + TOP_K * r + k], sem_rows).start()
        return carry

    lax.fori_loop(0, tm // ROW_UNROLL, body, 0)
    for k in range(TOP_K):
        pltpu.make_async_copy(x_ref, _rows_of(buf_hbm, 0, tm), sem_rows).wait()


def _dispatch(xn, dest, pend, n_rows):
    n = xn.shape[0] // ROW_TILE
    tm = min(MOE_TILE, n)
    assert n % tm == 0
    return pl.pallas_call(
        _dispatch_kernel,
        grid=(n // tm,),
        in_specs=[pl.BlockSpec(memory_space=pl.ANY), pl.BlockSpec(memory_space=pltpu.SMEM),
                  _rows(tm * ROW_TILE, LANES)],
        out_specs=pl.BlockSpec(memory_space=pl.ANY),
        out_shape=jax.ShapeDtypeStruct((n_rows * ROW_TILE, LANES), xn.dtype),
        scratch_shapes=[pltpu.SMEM((2 * TOP_K * tm,), jnp.int32),
                        pltpu.VMEM((EXPERT_BLOCK * ROW_TILE, LANES), xn.dtype),
                        pltpu.SemaphoreType.DMA((2,)), pltpu.SemaphoreType.DMA],
        compiler_params=_cparams("arbitrary"),
        name="dispatch",
    )(dest.reshape(n // tm, TOP_K * tm), pend, xn)


def _combine_kernel(dest_hbm, y_hbm, x_ref, rf_ref, o_ref, dest_smem, y0_ref, y1_ref, sem_idx, sem_rows):
    tm = x_ref.shape[0]
    base = _prefetched_indices(dest_hbm, dest_smem, sem_idx, TOP_K * tm)
    ybufs = (y0_ref, y1_ref)

    def body(o, carry):
        for u in range(ROW_UNROLL):
            r = o * ROW_UNROLL + u
            for k in range(TOP_K):
                _row_copy(y_hbm, dest_smem[base + TOP_K * r + k], ybufs[k], r, sem_rows).start()
        return carry

    lax.fori_loop(0, tm // ROW_UNROLL, body, 0)
    for k in range(TOP_K):
        pltpu.make_async_copy(_rows_of(y_hbm, 0, tm), ybufs[k], sem_rows).wait()
    sub = min(tm, EXPERT_BLOCK)
    for r0 in range(0, tm, sub):
        rs = slice(r0, r0 + sub)
        rf = rf_ref[rs, :]
        o_ref[rs, :] = x_ref[rs, :] + (_load_row_tiles(y0_ref, r0, sub) * rf[:, 0:1]
                                       + _load_row_tiles(y1_ref, r0, sub) * rf[:, 1:2])


def _combine(x1, y, dest, rf):
    n, d = x1.shape
    tm = min(MOE_TILE, n)
    assert n % tm == 0
    return pl.pallas_call(
        _combine_kernel,
        grid=(n // tm,),
        in_specs=[pl.BlockSpec(memory_space=pl.ANY), pl.BlockSpec(memory_space=pl.ANY), _rows(tm, d),
                  _rows(tm, LANES)],
        out_specs=_rows(tm, d),
        out_shape=jax.ShapeDtypeStruct((n, d), x1.dtype),
        scratch_shapes=[pltpu.SMEM((2 * TOP_K * tm,), jnp.int32), pltpu.VMEM((tm * ROW_TILE, LANES), y.dtype),
                        pltpu.VMEM((tm * ROW_TILE, LANES), y.dtype), pltpu.SemaphoreType.DMA((2,)),
                        pltpu.SemaphoreType.DMA],
        compiler_params=_cparams("arbitrary"),
        name="combine",
    )(dest.reshape(n // tm, TOP_K * tm), y, x1, rf)


def _moe(x1, xn, ri, rf, cnt, wg, wu, wd, layer):
    n, d = x1.shape
    bm = EXPERT_BLOCK
    counts = cnt[0, N_GROUPS:N_GROUPS + N_EXPERTS].astype(jnp.int32)
    psz = (counts + bm - 1) // bm * bm
    pend = jnp.cumsum(psz)
    pstart = pend - psz
    pstart_row = jnp.pad(pstart.astype(F32), (0, LANES - N_EXPERTS))[None, :]
    dest = _dest(ri, pstart_row)[:, :TOP_K]
    n_blocks = -(-(n * TOP_K) // bm) + N_EXPERTS
    blk_start = jnp.arange(n_blocks, dtype=jnp.int32) * bm
    blk_e = jnp.minimum(jnp.sum(pend[None, :] <= blk_start[:, None], axis=1), N_EXPERTS - 1).astype(jnp.int32)
    n_used = (pend[-1] // bm).astype(jnp.int32).reshape(1)
    buf = _dispatch(xn, dest, pend.astype(jnp.int32), n_blocks * bm)
    y = _experts(buf, blk_e, n_used, wg, wu, wd, layer)
    return x1, y, dest, rf


def _prep_layer(l, P):
    d = P['w_in'].shape[1]
    cw = P['conv_dw_w'].shape[-1]
    mw = P['ml_norm_g'].shape[-1]
    nw = NA_HEADS * P['na_q_norm_g'].shape[-1]
    off_g = 2 * cw + 4 * mw
    ng = 4 * ML_HEADS
    w = P['w_in'][l]
    w_perm = jnp.concatenate([w[:, :off_g], w[:, off_g + ng:], w[:, off_g:off_g + ng],
                              jnp.zeros((d, LANES - ng), F32)], axis=1).astype(BF16)
    nhd = nw // NA_HEADS
    head_id = jnp.arange(nw) // nhd
    bd = jnp.where(head_id[:, None] == head_id[None, :], 1.0 / nhd, 0.0).astype(BF16)
    wr = jnp.concatenate([P['w_router_group'][l], P['w_router_expert'][l],
                          jnp.zeros((d, LANES - N_GROUPS - N_EXPERTS), F32)], axis=1)
    br = jnp.concatenate([P['b_router_group'][l], P['b_router_expert'][l],
                          jnp.zeros((LANES - N_GROUPS - N_EXPERTS,), F32)])[None, :]
    return dict(
        dims=(cw, mw, nw),
        norm_mix_g=P['norm_mix_g'][l][None, :], w_in=w_perm,
        gate_b=jnp.pad(P['ml_gate_b'][l].reshape(-1), (0, LANES - ng))[None, :],
        nq=jnp.tile(P['na_q_norm_g'][l], NA_HEADS)[None, :], nk=jnp.tile(P['na_k_norm_g'][l], NA_HEADS)[None, :],
        bd=bd,
        conv_dw_w=P['conv_dw_w'][l], conv_dw_b=P['conv_dw_b'][l][None, :], conv_norm_g=P['conv_norm_g'][l][None, :],
        ml_qk_conv_w=P['ml_qk_conv_w'][l], ml_qk_conv_b=P['ml_qk_conv_b'][l][None, :],
        ml_norm_g=P['ml_norm_g'][l][None, :], na_table=_na_bias_table(P['na_rpb'][l]),
        w_out=P['w_out'][l].astype(BF16), norm_ffn_g=P['norm_ffn_g'][l][None, :],
        wrh=wr.astype(BF16), wrl=(wr - wr.astype(BF16).astype(F32)).astype(BF16), br=br,
        wg=P['w_exp_gate'], wu=P['w_exp_up'], wd=P['w_exp_down'], layer=l,
    )


def _layer(x, pending, B, T, p):
    dims = p['dims']
    proj_params = (p['norm_mix_g'], p['w_in'], p['gate_b'], p['nq'], p['nk'], p['bd'], dims)
    if pending is None:
        u, qkpre, v, o, naq, nak, nav, gates = _in_proj(x, *proj_params)
    else:
        x, (u, qkpre, v, o, naq, nak, nav, gates) = _combine_in_proj(pending, *proj_params)
    yc, hf, hb, yna = _mixers(B, T, u, qkpre, v, naq, nak, nav, gates, p)
    x1, xn, ri, rf, cnt = _out_proj(yc, hf, hb, o, yna, x, p['ml_norm_g'], p['w_out'], p['norm_ffn_g'], p['wrh'],
                                    p['wrl'], p['br'], dims)
    return _moe(x1, xn, ri, rf, cnt, p['wg'], p['wu'], p['wd'], p['layer'])


def kernel(x_prompt, x_sample, norm_mix_g, w_in, conv_dw_w, conv_dw_b, conv_norm_g, ml_qk_conv_w, ml_qk_conv_b,
           ml_gate_b, ml_norm_g, na_q_norm_g, na_k_norm_g, na_rpb, w_out, norm_ffn_g, w_router_group,
           b_router_group, w_router_expert, b_router_expert, w_exp_gate, w_exp_up, w_exp_down):
    P = dict(norm_mix_g=norm_mix_g, w_in=w_in, conv_dw_w=conv_dw_w, conv_dw_b=conv_dw_b, conv_norm_g=conv_norm_g,
             ml_qk_conv_w=ml_qk_conv_w, ml_qk_conv_b=ml_qk_conv_b, ml_gate_b=ml_gate_b, ml_norm_g=ml_norm_g,
             na_q_norm_g=na_q_norm_g, na_k_norm_g=na_k_norm_g, na_rpb=na_rpb, w_out=w_out, norm_ffn_g=norm_ffn_g,
             w_router_group=w_router_group, b_router_group=b_router_group, w_router_expert=w_router_expert,
             b_router_expert=b_router_expert, w_exp_gate=w_exp_gate, w_exp_up=w_exp_up, w_exp_down=w_exp_down)
    depth = w_in.shape[0]
    d = x_prompt.shape[-1]
    groups = [(x_prompt.shape[0], x_prompt.shape[1], x_prompt.reshape(-1, d)),
              (x_sample.shape[0], x_sample.shape[1], x_sample.reshape(-1, d))]
    pendings = [None] * len(groups)
    for l in range(depth):
        p = _prep_layer(l, P)
        pendings = [_layer(x, pending, B, T, p) for (B, T, x), pending in zip(groups, pendings)]
        groups = [(B, T, None) for B, T, _ in groups]
    return tuple(_combine(*pending).reshape(B, T, d) for (B, T, _), pending in zip(groups, pendings))
```

```python
import functools

import numpy as np

import jax
import jax.numpy as jnp
from jax import lax
from jax.experimental import pallas as pl
from jax.experimental.pallas import tpu as pltpu

F32 = jnp.float32
BF16 = jnp.bfloat16

ML_HEADS = 4
NA_HEADS = 4
CONV_K = 31
ML_CHUNK = 128
GRID_W = 64
NA_WIN_ROWS = 8
NA_WIN_COLS = 16
N_GROUPS = 8
EXPERTS_PER_GROUP = 8
N_EXPERTS = N_GROUPS * EXPERTS_PER_GROUP
TOP_K = 2
EPS = 1e-6

LANES = 128
SUBLANES = 8
VMEM_LIMIT = 48 * 1024 * 1024

TOKEN_TILE = 512
EXPERT_BLOCK = 256
MOE_TILE = 1024


def _cparams(*sem):
    return pltpu.CompilerParams(dimension_semantics=sem, vmem_limit_bytes=VMEM_LIMIT)


def _full(shape):
    return pl.BlockSpec(shape, lambda *_: (0,) * len(shape))


def _rows(tm, width):
    return pl.BlockSpec((tm, width), lambda i: (i, 0))


ROW_TILE = SUBLANES


def _load_row_tiles(ref, r0, rows):
    return jnp.concatenate([ref[pl.ds(r0 * ROW_TILE + s, rows, stride=ROW_TILE), :] for s in range(ROW_TILE)],
                           axis=-1)


def _store_row_tiles(ref, r0, val):
    for s in range(ROW_TILE):
        ref[pl.ds(r0 * ROW_TILE + s, val.shape[0], stride=ROW_TILE), :] = val[:, s * LANES:(s + 1) * LANES]


def _group_mean(zz, bd):
    hi = zz.astype(BF16)
    lo = (zz - hi.astype(F32)).astype(BF16)
    return (jnp.dot(hi, bd, preferred_element_type=F32) + jnp.dot(lo, bd, preferred_element_type=F32))


def _combine_in_proj_kernel(dest_hbm, y_hbm, x1_ref, rf_ref, g_ref, w_ref, gb_ref, nq_ref, nk_ref, bd_ref,
                            x_ref, u_ref, qk_ref, v_ref, o_ref, naq_ref, nak_ref, nav_ref, gt_ref,
                            dest_smem, y_ref, sem_idx, sem_rows, *, dims):
    i = pl.program_id(0)
    nt = pl.num_programs(0)
    tm = x1_ref.shape[0]

    def fetch_rows(step, slot):
        base = slot * TOP_K * tm
        idx_copy = pltpu.make_async_copy(dest_hbm.at[step], dest_smem.at[pl.ds(base, TOP_K * tm)], sem_idx.at[slot])
        idx_copy.start()
        idx_copy.wait()

        def body(o, carry):
            for u in range(ROW_UNROLL):
                r = o * ROW_UNROLL + u
                for k in range(TOP_K):
                    _row_copy(y_hbm, dest_smem[base + TOP_K * r + k], y_ref.at[slot, k], r, sem_rows.at[slot]).start()
            return carry

        lax.fori_loop(0, tm // ROW_UNROLL, body, 0)

    @pl.when(i == 0)
    def _():
        fetch_rows(0, 0)

    @pl.when(i + 1 < nt)
    def _():
        fetch_rows(i + 1, (i + 1) % 2)

    slot = i % 2
    for k in range(TOP_K):
        pltpu.make_async_copy(_rows_of(y_hbm, 0, tm), y_ref.at[slot, k], sem_rows.at[slot]).wait()
    sub = min(tm, EXPERT_BLOCK)
    for r0 in range(0, tm, sub):
        rs = slice(r0, r0 + sub)
        rf = rf_ref[rs, :]
        x_ref[rs, :] = x1_ref[rs, :] + (_load_row_tiles(y_ref.at[slot, 0], r0, sub) * rf[:, 0:1]
                                        + _load_row_tiles(y_ref.at[slot, 1], r0, sub) * rf[:, 1:2])
    _in_proj_kernel(x_ref, g_ref, w_ref, gb_ref, nq_ref, nk_ref, bd_ref,
                    u_ref, qk_ref, v_ref, o_ref, naq_ref, nak_ref, nav_ref, gt_ref, dims=dims)


def _in_proj_kernel(x_ref, g_ref, w_ref, gb_ref, nq_ref, nk_ref, bd_ref,
                    u_ref, qk_ref, v_ref, o_ref, naq_ref, nak_ref, nav_ref, gt_ref, *, dims):
    cw, mw, nw = dims
    x = x_ref[...]
    ms = jnp.mean(x * x, axis=-1, keepdims=True)
    xn = (x * lax.rsqrt(ms + EPS) * g_ref[...]).astype(BF16)

    def proj(lo, width):
        return jnp.dot(xn, w_ref[:, lo:lo + width], preferred_element_type=F32)

    zc = proj(0, 2 * cw)
    u_ref[...] = zc[:, :cw] * jax.nn.sigmoid(zc[:, cw:])
    off = 2 * cw
    qk_ref[...] = proj(off, 2 * mw)
    v_ref[...] = proj(off + 2 * mw, mw).astype(BF16)
    o_ref[...] = proj(off + 3 * mw, mw)
    off += 4 * mw
    bd = bd_ref[...]
    zq = proj(off, nw)
    zk = proj(off + nw, nw)
    hd = nw // NA_HEADS
    naq_ref[...] = (zq * lax.rsqrt(_group_mean(zq * zq, bd) + EPS) * nq_ref[...] * (hd ** -0.5)).astype(BF16)
    nak_ref[...] = (zk * lax.rsqrt(_group_mean(zk * zk, bd) + EPS) * nk_ref[...]).astype(BF16)
    nav_ref[...] = proj(off + 2 * nw, nw).astype(BF16)
    off += 3 * nw
    gt_ref[...] = proj(off, LANES) + gb_ref[...]


def _combine_in_proj(pending, g, w, gate_b, nq, nk, bd, dims):
    x1, y, dest, rf = pending
    n, d = x1.shape
    cw, mw, nw = dims
    tm = min(TOKEN_TILE, n)
    outs = [(d, F32), (cw, F32), (2 * mw, F32), (mw, BF16), (mw, F32), (nw, BF16), (nw, BF16), (nw, BF16),
            (LANES, F32)]
    res = pl.pallas_call(
        functools.partial(_combine_in_proj_kernel, dims=dims),
        grid=(n // tm,),
        in_specs=[pl.BlockSpec(memory_space=pl.ANY), pl.BlockSpec(memory_space=pl.ANY), _rows(tm, d),
                  _rows(tm, LANES), _full(g.shape), pl.BlockSpec(w.shape, lambda i: (0, 0), pipeline_mode=pl.Buffered(1)),
                  _full(gate_b.shape), _full(nq.shape), _full(nk.shape), _full(bd.shape)],
        out_specs=[_rows(tm, wd) for wd, _ in outs],
        out_shape=[jax.ShapeDtypeStruct((n, wd), dt) for wd, dt in outs],
        scratch_shapes=[pltpu.SMEM((2 * TOP_K * tm,), jnp.int32),
                        pltpu.VMEM((2, TOP_K, tm * ROW_TILE, LANES), y.dtype),
                        pltpu.SemaphoreType.DMA((2,)), pltpu.SemaphoreType.DMA((2,))],
        compiler_params=_cparams("arbitrary"),
        name="combine_in_proj",
    )(dest.reshape(n // tm, TOP_K * tm), y, x1, rf, g, w, gate_b, nq, nk, bd)
    return res[0], res[1:]


def _in_proj(x, g, w, gate_b, nq, nk, bd, dims):
    n, d = x.shape
    cw, mw, nw = dims
    tm = min(TOKEN_TILE, n)
    outs = [(cw, F32), (2 * mw, F32), (mw, BF16), (mw, F32), (nw, BF16), (nw, BF16), (nw, BF16), (LANES, F32)]
    return pl.pallas_call(
        functools.partial(_in_proj_kernel, dims=dims),
        grid=(n // tm,),
        in_specs=[_rows(tm, d), _full(g.shape), _full(w.shape), _full(gate_b.shape),
                  _full(nq.shape), _full(nk.shape), _full(bd.shape)],
        out_specs=[_rows(tm, wd) for wd, _ in outs],
        out_shape=[jax.ShapeDtypeStruct((n, wd), dt) for wd, dt in outs],
        compiler_params=_cparams("parallel"),
        name="in_proj",
    )(x, g, w, gate_b, nq, nk, bd)


OUT_SUB = 128


def _out_proj_kernel(yc_ref, hf_ref, hb_ref, o_ref, yna_ref, x_ref, mg_ref, w_ref, fg_ref, wrh_ref, wrl_ref, br_ref,
                     x1_ref, xn_ref, ri_ref, rf_ref, cnt_ref, lg_ref, carry_ref, *, dims):
    cw, mw, nw = dims
    hd = mw // ML_HEADS
    for r0 in range(0, x_ref.shape[0], OUT_SUB):
        rs = slice(r0, r0 + OUT_SUB)
        hm = hf_ref[rs, :] + hb_ref[rs, :]
        parts = []
        for h in range(ML_HEADS):
            seg = hm[:, h * hd:(h + 1) * hd]
            parts.append(seg * lax.rsqrt(jnp.mean(seg * seg, axis=-1, keepdims=True) + EPS))
        yml = jnp.concatenate(parts, axis=-1) * mg_ref[...] * jax.nn.sigmoid(o_ref[rs, :])
        y = jnp.dot(yc_ref[rs, :], w_ref[0:cw, :], preferred_element_type=F32)
        y += jnp.dot(yml.astype(BF16), w_ref[cw:cw + mw, :], preferred_element_type=F32)
        y += jnp.dot(yna_ref[rs, :], w_ref[cw + mw:cw + mw + nw, :], preferred_element_type=F32)
        x1 = x_ref[rs, :] + y
        x1_ref[rs, :] = x1
        xn = x1 * lax.rsqrt(jnp.mean(x1 * x1, axis=-1, keepdims=True) + EPS) * fg_ref[...]
        _store_row_tiles(xn_ref, r0, xn)
        xh = xn.astype(BF16)
        xl = (xn - xh.astype(F32)).astype(BF16)
        lg = jnp.dot(xh, wrh_ref[...], preferred_element_type=F32)
        lg += jnp.dot(xl, wrh_ref[...], preferred_element_type=F32)
        lg += jnp.dot(xh, wrl_ref[...], preferred_element_type=F32)
        lg_ref[rs, :] = lg + br_ref[...]
    _route_tile(lg_ref[...], ri_ref, rf_ref, cnt_ref, carry_ref)


def _out_proj(yc, hf, hb, o, yna, x, mg, w, fg, wrh, wrl, br, dims):
    n, d = x.shape
    cw, mw, nw = dims
    tm = min(TOKEN_TILE, n)
    assert tm % OUT_SUB == 0
    return pl.pallas_call(
        functools.partial(_out_proj_kernel, dims=dims),
        grid=(n // tm,),
        in_specs=[_rows(tm, cw), _rows(tm, mw), _rows(tm, mw), _rows(tm, mw), _rows(tm, nw), _rows(tm, d),
                  _full(mg.shape), _full(w.shape), _full(fg.shape), _full(wrh.shape), _full(wrl.shape),
                  _full(br.shape)],
        out_specs=[_rows(tm, d), _rows(tm * ROW_TILE, LANES), _rows(tm, LANES), _rows(tm, LANES),
                   _full((1, LANES))],
        out_shape=[jax.ShapeDtypeStruct((n, d), F32), jax.ShapeDtypeStruct((n * ROW_TILE, LANES), F32),
                   jax.ShapeDtypeStruct((n, LANES), jnp.int32), jax.ShapeDtypeStruct((n, LANES), F32),
                   jax.ShapeDtypeStruct((1, LANES), F32)],
        scratch_shapes=[pltpu.VMEM((tm, LANES), F32), pltpu.VMEM((1, LANES), F32)],
        compiler_params=_cparams("arbitrary"),
        name="out_proj",
    )(yc, hf, hb, o, yna, x, mg, w, fg, wrh, wrl, br)


def _expert_kernel(be_ref, nb_ref, x_ref, wg_ref, wu_ref, wd_ref, y_ref, wgb_ref, wub_ref, wdb_ref):
    i = pl.program_id(0)

    @pl.when((i == 0) | (be_ref[i] != be_ref[jnp.maximum(i - 1, 0)]))
    def _():
        wgb_ref[...] = wg_ref[0].astype(BF16)
        wub_ref[...] = wu_ref[0].astype(BF16)
        wdb_ref[...] = wd_ref[0].astype(BF16)

    @pl.when(i < nb_ref[0])
    def _():
        bm = x_ref.shape[0] // ROW_TILE
        xb = _load_row_tiles(x_ref, 0, bm).astype(BF16)
        g = jnp.dot(xb, wgb_ref[...], preferred_element_type=F32)
        u = jnp.dot(xb, wub_ref[...], preferred_element_type=F32)
        h = (g * jax.nn.sigmoid(g) * u).astype(BF16)
        _store_row_tiles(y_ref, 0, jnp.dot(h, wdb_ref[...], preferred_element_type=F32))

    @pl.when(i >= nb_ref[0])
    def _():
        y_ref[...] = jnp.zeros_like(y_ref)


def _experts(buf, blk_e, n_used, wg, wu, wd, layer):
    d, de = wg.shape[-2:]
    assert d == ROW_TILE * LANES
    bm = EXPERT_BLOCK
    tb = bm * ROW_TILE
    grid_spec = pltpu.PrefetchScalarGridSpec(
        num_scalar_prefetch=2,
        grid=(buf.shape[0] // tb,),
        in_specs=[pl.BlockSpec((tb, LANES), lambda i, be, nb: (jnp.minimum(i, nb[0] - 1), 0)),
                  pl.BlockSpec((None, 1, d, de), lambda i, be, nb: (layer, be[i], 0, 0)),
                  pl.BlockSpec((None, 1, d, de), lambda i, be, nb: (layer, be[i], 0, 0)),
                  pl.BlockSpec((None, 1, de, d), lambda i, be, nb: (layer, be[i], 0, 0))],
        out_specs=pl.BlockSpec((tb, LANES), lambda i, be, nb: (i, 0)),
        scratch_shapes=[pltpu.VMEM((d, de), BF16), pltpu.VMEM((d, de), BF16), pltpu.VMEM((de, d), BF16)],
    )
    return pl.pallas_call(
        _expert_kernel,
        grid_spec=grid_spec,
        out_shape=jax.ShapeDtypeStruct(buf.shape, F32),
        compiler_params=_cparams("arbitrary"),
        name="experts",
    )(blk_e, n_used, buf, wg, wu, wd)


NA_QROWS = 4
NEG_BIG = -1e30


def _bias_expand_kernel(rpb_ref, selr_ref, selc_ref, o_ref):
    rows = jnp.dot(selr_ref[...], rpb_ref[0], preferred_element_type=F32, precision=lax.Precision.HIGHEST)
    o_ref[0] = jnp.dot(rows, selc_ref[...], preferred_element_type=F32, precision=lax.Precision.HIGHEST)


def _na_bias_table(rpb):
    nh = rpb.shape[0]
    a = np.arange(NA_QROWS)[:, None]
    i = np.arange(3 * NA_QROWS)[None, :]
    half = NA_WIN_ROWS // 2
    a0 = 0 * a
    row_ok = np.stack([(i >= a0 + NA_QROWS) & (i < a0 + NA_QROWS + NA_WIN_ROWS),
                       (i >= a + NA_QROWS - half) & (i < a + NA_QROWS - half + NA_WIN_ROWS),
                       (i >= a0 + 2 * NA_QROWS - NA_WIN_ROWS) & (i < a0 + 2 * NA_QROWS)])
    ro = np.clip(i - NA_QROWS - a + (NA_WIN_ROWS - 1), 0, 2 * NA_WIN_ROWS - 2)
    c = np.arange(GRID_W)
    cs = np.clip(c - NA_WIN_COLS // 2, 0, GRID_W - NA_WIN_COLS)
    col_ok = (c[None, :] >= cs[:, None]) & (c[None, :] < cs[:, None] + NA_WIN_COLS)
    co = np.clip(c[None, :] - c[:, None] + (NA_WIN_COLS - 1), 0, 2 * NA_WIN_COLS - 2)
    nr, nc = 2 * SUBLANES, LANES
    sel_r = (ro.reshape(-1)[:, None] == np.arange(nr)[None, :]).astype(np.float32)
    sel_c = (np.arange(nc)[:, None] == co.reshape(-1)[None, :]).astype(np.float32)
    rpb_p = jnp.pad(rpb.astype(F32), ((0, 0), (0, nr - rpb.shape[1]), (0, nc - rpb.shape[2])))
    nx, ny = sel_r.shape[0], sel_c.shape[1]
    bias = pl.pallas_call(
        _bias_expand_kernel,
        grid=(nh,),
        in_specs=[pl.BlockSpec((1, nr, nc), lambda h: (h, 0, 0)), _full(sel_r.shape), _full(sel_c.shape)],
        out_specs=pl.BlockSpec((1, nx, ny), lambda h: (h, 0, 0)),
        out_shape=jax.ShapeDtypeStruct((nh, nx, ny), F32),
        compiler_params=_cparams("parallel"),
        name="na_bias_expand",
    )(rpb_p, jnp.asarray(sel_r), jnp.asarray(sel_c))
    bias = bias.reshape(nh, NA_QROWS, 3 * NA_QROWS, GRID_W, GRID_W).transpose(0, 1, 3, 2, 4)
    ok = row_ok[:, None, :, None, :, None] & col_ok[None, None, None, :, None, :]
    tab = jnp.where(jnp.asarray(ok), bias[None], NEG_BIG)
    return tab.reshape(3, nh, NA_QROWS * GRID_W, 3 * NA_QROWS * GRID_W)


def _na_kernel(q_ref, kp_ref, kc_ref, kn_ref, vp_ref, vc_ref, vn_ref, b_ref, o_ref):
    q = q_ref[...]
    kcat = jnp.concatenate([kp_ref[...], kc_ref[...], kn_ref[...]], axis=0)
    vcat = jnp.concatenate([vp_ref[...], vc_ref[...], vn_ref[...]], axis=0)
    nw = q.shape[-1]
    hd = nw // NA_HEADS
    lane_head = lax.broadcasted_iota(jnp.int32, q.shape, 1) // hd
    acc = jnp.zeros(q.shape, F32)
    for h in range(NA_HEADS):
        sel = lane_head == h
        qm = jnp.where(sel, q, jnp.zeros_like(q))
        s = lax.dot_general(qm, kcat, (((1,), (1,)), ((), ())), preferred_element_type=F32) + b_ref[0, h]
        m = jnp.max(s, axis=-1, keepdims=True)
        p = jnp.exp(s - m)
        l = jnp.sum(p, axis=-1, keepdims=True)
        oh = jnp.dot(p.astype(BF16), vcat, preferred_element_type=F32)
        acc = jnp.where(sel, oh / l, acc)
    o_ref[...] = acc.astype(o_ref.dtype)


def _neigh_attn_pallas(q, k, v, table, B, T):
    n, nw = q.shape
    tq = NA_QROWS * GRID_W
    J = T // tq
    assert T % tq == 0 and J >= 2 and T // GRID_W >= NA_WIN_ROWS
    blk = (tq, nw)
    cur = pl.BlockSpec(blk, lambda b, j: (b * J + j, 0))
    prev = pl.BlockSpec(blk, lambda b, j: (b * J + jnp.maximum(j - 1, 0), 0))
    nxt = pl.BlockSpec(blk, lambda b, j: (b * J + jnp.minimum(j + 1, J - 1), 0))
    tab = pl.BlockSpec((1,) + table.shape[1:],
                       lambda b, j: (jnp.where(j == 0, 0, jnp.where(j == J - 1, 2, 1)), 0, 0, 0))
    return pl.pallas_call(
        _na_kernel,
        grid=(B, J),
        in_specs=[cur, prev, cur, nxt, prev, cur, nxt, tab],
        out_specs=cur,
        out_shape=jax.ShapeDtypeStruct((n, nw), BF16),
        compiler_params=_cparams("parallel", "arbitrary"),
        name="neigh_attn",
    )(q, k, k, k, v, v, v, table)


HALO = 16
CONV_ROWS = 64


def _local_conv_kernel(u_ref, up_ref, un_ref, qk_ref, qkp_ref, qkn_ref, cw_ref, cb_ref, cg_ref, mw_ref, mb_ref,
                       yc_ref, q_ref, k_ref, wu_ref, wqk_ref, sh_ref, *, k_scale):
    t = pl.program_id(1)
    nt = pl.num_programs(1)
    tm = u_ref.shape[0]
    keep_p = (t > 0).astype(F32)
    keep_n = (t < nt - 1).astype(F32)
    wu_ref[0:HALO, :] = up_ref[...] * keep_p
    wu_ref[HALO:HALO + tm, :] = u_ref[...]
    wu_ref[HALO + tm:, :] = un_ref[...] * keep_n
    wqk_ref[0:HALO, :] = qkp_ref[...] * keep_p
    wqk_ref[HALO:HALO + tm, :] = qk_ref[...]
    wqk_ref[HALO + tm:, :] = qkn_ref[...] * keep_n
    kc = cw_ref.shape[0]
    km = mw_ref.shape[0]
    mw2 = q_ref.shape[-1]
    nsh = sh_ref.shape[1]
    for ph in range(1, SUBLANES):
        sh_ref[ph - 1] = wu_ref[ph:ph + nsh, :]
    for r0 in range(0, tm, CONV_ROWS):
        acc = jnp.zeros((CONV_ROWS, u_ref.shape[-1]), F32) + cb_ref[...]
        for k in range(kc):
            s = r0 + HALO - kc // 2 + k
            ph = s % SUBLANES
            tap = wu_ref[s:s + CONV_ROWS, :] if ph == 0 else sh_ref[ph - 1, s - ph:s - ph + CONV_ROWS, :]
            acc = acc + tap * cw_ref[k:k + 1, :]
        y = acc * lax.rsqrt(jnp.mean(acc * acc, axis=-1, keepdims=True) + EPS) * cg_ref[...]
        yc_ref[r0:r0 + CONV_ROWS, :] = (y * jax.nn.sigmoid(y)).astype(yc_ref.dtype)
        acc = jnp.zeros((CONV_ROWS, qk_ref.shape[-1]), F32) + mb_ref[...]
        for k in range(km):
            s = r0 + HALO - km // 2 + k
            acc = acc + wqk_ref[s:s + CONV_ROWS, :] * mw_ref[k:k + 1, :]
        y = acc * jax.nn.sigmoid(acc)
        q_ref[r0:r0 + CONV_ROWS, :] = y[:, :mw2].astype(q_ref.dtype)
        k_ref[r0:r0 + CONV_ROWS, :] = (y[:, mw2:] * k_scale).astype(k_ref.dtype)


def _local_conv(u, qkpre, cw, cb, cg, mw, mb, B, T):
    n, cwid = u.shape
    qw = qkpre.shape[-1]
    tm = min(TOKEN_TILE, T)
    nt = T // tm
    hb = tm // HALO
    nhb = n // HALO
    assert T % tm == 0 and tm % CONV_ROWS == 0 and CONV_K // 2 <= HALO

    def cur(w):
        return pl.BlockSpec((tm, w), lambda b, t: (b * nt + t, 0))

    def prev(w):
        return pl.BlockSpec((HALO, w), lambda b, t: (jnp.maximum((b * nt + t) * hb - 1, 0), 0))

    def nxt(w):
        return pl.BlockSpec((HALO, w), lambda b, t: (jnp.minimum((b * nt + t + 1) * hb, nhb - 1), 0))

    hd = (qw // 2) // ML_HEADS
    return pl.pallas_call(
        functools.partial(_local_conv_kernel, k_scale=hd ** -0.5),
        grid=(B, nt),
        in_specs=[cur(cwid), prev(cwid), nxt(cwid), cur(qw), prev(qw), nxt(qw),
                  _full(cw.shape), _full(cb.shape), _full(cg.shape), _full(mw.shape), _full(mb.shape)],
        out_specs=[cur(cwid), cur(qw // 2), cur(qw // 2)],
        out_shape=[jax.ShapeDtypeStruct((n, cwid), BF16), jax.ShapeDtypeStruct((n, qw // 2), BF16),
                   jax.ShapeDtypeStruct((n, qw // 2), BF16)],
        scratch_shapes=[pltpu.VMEM((tm + 2 * HALO, cwid), F32), pltpu.VMEM((tm + 2 * HALO, qw), F32),
                        pltpu.VMEM((SUBLANES - 1, tm + 2 * HALO - SUBLANES, cwid), F32)],
        compiler_params=_cparams("parallel", "parallel"),
        name="local_conv",
    )(u, u, u, qkpre, qkpre, qkpre, cw, cb, cg, mw, mb)


def _scan_rows(x, reverse, combine, fill):
    L = x.shape[0]
    row = lax.broadcasted_iota(jnp.int32, x.shape, 0)
    s = 1
    while s < L:
        if reverse:
            x = combine(x, jnp.where(row < L - s, pltpu.roll(x, L - s, axis=0), fill))
        else:
            x = combine(x, jnp.where(row >= s, pltpu.roll(x, s, axis=0), fill))
        s *= 2
    return x


def _mlstm_direction(q_ref, k_ref, v_ref, g_ref, h_ref, c_ref, m_ref, reverse):
    L = q_ref.shape[0]
    hd = q_ref.shape[-1] // ML_HEADS
    ci = 2 * ML_HEADS if reverse else 0
    cf = ci + ML_HEADS
    g = g_ref[...]
    gi = pltpu.roll(g, ML_HEADS, axis=1)
    lf = jnp.minimum(g, 0.0) - jnp.log1p(jnp.exp(-jnp.abs(g)))
    b = _scan_rows(lf, reverse, jnp.add, 0.0)
    r = gi - b
    cm = _scan_rows(r, reverse, jnp.maximum, NEG_BIG)
    total = b[0:1, :] if reverse else b[L - 1:L, :]
    m_prev = m_ref[...]
    inter = b + m_prev
    m_j = jnp.maximum(inter, b + cm)
    a_t = b - m_j
    e_t = jnp.exp(inter - m_j)
    z_t = jnp.exp(-m_j)
    w = total - b + gi
    m_loc = jnp.max(w, axis=0, keepdims=True)
    wa_t = jnp.exp(w - m_loc)
    m_new = jnp.maximum(total + m_prev, m_loc)
    dec = jnp.exp(total + m_prev - m_new)
    inc = jnp.exp(m_loc - m_new)
    m_ref[...] = m_new
    rt = jnp.transpose(r)
    jj = lax.broadcasted_iota(jnp.int32, (L, L), 0)
    ss = lax.broadcasted_iota(jnp.int32, (L, L), 1)
    visible = (ss >= jj) if reverse else (ss <= jj)
    ones = jnp.ones((L, hd), BF16)
    for h in range(ML_HEADS):
        sl = slice(h * hd, (h + 1) * hd)
        col = slice(cf + h, cf + h + 1)
        qh = q_ref[:, sl]
        kh = k_ref[:, sl]
        v_ext = jnp.concatenate([v_ref[:, sl], ones], axis=-1)
        c_prev = c_ref[h]
        pm = jnp.exp(jnp.where(visible, a_t[:, col] + rt[col, :], NEG_BIG))
        qk = lax.dot_general(qh, kh, (((1,), (1,)), ((), ())), preferred_element_type=F32) * pm
        tot = (jnp.dot(qk.astype(BF16), v_ext, preferred_element_type=F32)
               + e_t[:, col] * jnp.dot(qh, c_prev.astype(BF16), preferred_element_type=F32))
        h_ref[:, sl] = tot[:, :hd] / jnp.maximum(jnp.abs(tot[:, hd:]), z_t[:, col])
        ka = (kh.astype(F32) * wa_t[:, col]).astype(BF16)
        s_ext = lax.dot_general(ka, v_ext, (((0,), (0,)), ((), ())), preferred_element_type=F32)
        c_ref[h] = dec[:, col] * c_prev + inc[:, col] * s_ext


def _mlstm_kernel(qf_ref, kf_ref, vf_ref, gf_ref, qb_ref, kb_ref, vb_ref, gb_ref, hf_ref, hb_ref,
                  cf_ref, cb_ref, mf_ref, mb_ref):
    @pl.when(pl.program_id(1) == 0)
    def _():
        cf_ref[...] = jnp.zeros_like(cf_ref)
        cb_ref[...] = jnp.zeros_like(cb_ref)
        mf_ref[...] = jnp.zeros_like(mf_ref)
        mb_ref[...] = jnp.zeros_like(mb_ref)

    L = ML_CHUNK

    def chunk(ref, j):
        return ref.at[pl.ds(j * L, L), :]

    for j in range(ML_STEP_CHUNKS):
        jb = ML_STEP_CHUNKS - 1 - j
        _mlstm_direction(chunk(qf_ref, j), chunk(kf_ref, j), chunk(vf_ref, j), chunk(gf_ref, j), chunk(hf_ref, j),
                         cf_ref, mf_ref, False)
        _mlstm_direction(chunk(qb_ref, jb), chunk(kb_ref, jb), chunk(vb_ref, jb), chunk(gb_ref, jb),
                         chunk(hb_ref, jb), cb_ref, mb_ref, True)


ML_STEP_CHUNKS = 4


def _mlstm(q, k, v, gates, B, T):
    n, mw = q.shape
    L = ML_CHUNK * ML_STEP_CHUNKS
    nc = T // L
    hd = mw // ML_HEADS
    assert T % L == 0 and hd == LANES

    def fwd(w):
        return pl.BlockSpec((L, w), lambda b, c: (b * nc + c, 0))

    def bwd(w):
        return pl.BlockSpec((L, w), lambda b, c: (b * nc + nc - 1 - c, 0))

    return pl.pallas_call(
        _mlstm_kernel,
        grid=(B, nc),
        in_specs=[fwd(mw), fwd(mw), fwd(mw), fwd(LANES), bwd(mw), bwd(mw), bwd(mw), bwd(LANES)],
        out_specs=[fwd(mw), bwd(mw)],
        out_shape=[jax.ShapeDtypeStruct((n, mw), F32), jax.ShapeDtypeStruct((n, mw), F32)],
        scratch_shapes=[pltpu.VMEM((ML_HEADS, hd, 2 * hd), F32), pltpu.VMEM((ML_HEADS, hd, 2 * hd), F32),
                        pltpu.VMEM((1, LANES), F32), pltpu.VMEM((1, LANES), F32)],
        compiler_params=_cparams("parallel", "arbitrary"),
        name="mlstm",
    )(q, k, v, gates, q, k, v, gates)


def _mixers(B, T, u, qkpre, v, naq, nak, nav, gates, p):
    yc, q, k = _local_conv(u, qkpre, p['conv_dw_w'], p['conv_dw_b'], p['conv_norm_g'],
                           p['ml_qk_conv_w'], p['ml_qk_conv_b'], B, T)
    hf, hb = _mlstm(q, k, v, gates, B, T)
    y_na = _neigh_attn_pallas(naq, nak, nav, p['na_table'], B, T)
    return yc, hf, hb, y_na


def _first_lane(mask, lane):
    return jnp.min(jnp.where(mask, lane, LANES), axis=-1, keepdims=True)


def _route_tile(lg, ri_ref, rf_ref, cnt_ref, carry_ref):
    @pl.when(pl.program_id(0) == 0)
    def _():
        carry_ref[...] = jnp.zeros_like(carry_ref)

    tm = lg.shape[0]
    lane = lax.broadcasted_iota(jnp.int32, lg.shape, 1)
    gl = jnp.where(lane < N_GROUPS, lg, NEG_BIG)
    gmax = jnp.max(gl, axis=-1, keepdims=True)
    pg = 1.0 / jnp.sum(jnp.exp(gl - gmax), axis=-1, keepdims=True)
    gsel = _first_lane(gl == gmax, lane)
    lo = N_GROUPS + gsel * EXPERTS_PER_GROUP
    in_grp = (lane >= lo) & (lane < lo + EXPERTS_PER_GROUP)
    el = jnp.where(in_grp, lg, NEG_BIG)
    emax = jnp.max(el, axis=-1, keepdims=True)
    esum = jnp.sum(jnp.exp(el - emax), axis=-1, keepdims=True)
    idx1 = _first_lane(in_grp & (el == emax), lane)
    el2 = jnp.where(lane == idx1, NEG_BIG, el)
    emax2 = jnp.max(el2, axis=-1, keepdims=True)
    idx2 = _first_lane(in_grp & (el2 == emax2) & (lane != idx1), lane)
    p1 = 1.0 / esum
    p2 = jnp.exp(emax2 - emax) / esum
    g1 = pg * p1 / (p1 + p2)
    g2 = pg * p2 / (p1 + p2)
    hit1 = lane == idx1
    hit2 = lane == idx2
    onehot = jnp.where(hit1 | hit2, 1.0, 0.0)
    r = lax.broadcasted_iota(jnp.int32, (tm, tm), 0)
    c = lax.broadcasted_iota(jnp.int32, (tm, tm), 1)
    tri = jnp.where(r > c, 1.0, 0.0).astype(BF16)
    prefix = jnp.dot(tri, onehot.astype(BF16), preferred_element_type=F32) + carry_ref[...]
    rank1 = jnp.sum(jnp.where(hit1, prefix, 0.0), axis=-1, keepdims=True).astype(jnp.int32)
    rank2 = jnp.sum(jnp.where(hit2, prefix, 0.0), axis=-1, keepdims=True).astype(jnp.int32)
    carry = carry_ref[...] + jnp.sum(onehot, axis=0, keepdims=True)
    carry_ref[...] = carry
    cnt_ref[...] = carry
    ri_ref[...] = jnp.where(lane == 0, idx1 - N_GROUPS, jnp.where(lane == 1, idx2 - N_GROUPS,
                            jnp.where(lane == 2, rank1, jnp.where(lane == 3, rank2, 0))))
    rf_ref[...] = jnp.where(lane == 0, g1, jnp.where(lane == 1, g2, 0.0))


def _dest_kernel(ri_ref, ps_ref, d_ref):
    ri = ri_ref[...]
    lane = lax.broadcasted_iota(jnp.int32, ri.shape, 1)
    ps = ps_ref[...]
    out = jnp.zeros(ri.shape, jnp.int32)
    for k in range(TOP_K):
        start = jnp.sum(jnp.where(lane == ri[:, k:k + 1], ps, 0.0), axis=-1, keepdims=True).astype(jnp.int32)
        out = jnp.where(lane == k, start + ri[:, TOP_K + k:TOP_K + k + 1], out)
    d_ref[...] = out


def _dest(ri, pstart_row):
    n = ri.shape[0]
    tm = min(4 * TOKEN_TILE, n)
    assert n % tm == 0
    return pl.pallas_call(
        _dest_kernel,
        grid=(n // tm,),
        in_specs=[_rows(tm, LANES), _full((1, LANES))],
        out_specs=_rows(tm, LANES),
        out_shape=jax.ShapeDtypeStruct((n, LANES), jnp.int32),
        compiler_params=_cparams("parallel"),
        name="dest",
    )(ri, pstart_row)


ROW_UNROLL = 8


def _rows_of(ref, first, count):
    return ref.at[pl.ds(pl.multiple_of(first * ROW_TILE, ROW_TILE), count * ROW_TILE), :]


def _row_copy(src, i, dst, j, sem):
    return pltpu.make_async_copy(_rows_of(src, i, 1), _rows_of(dst, j, 1), sem)


def _prefetched_indices(dest_hbm, dest_smem, sem_idx, width):
    i = pl.program_id(0)

    def copy(step):
        half = step % 2
        return pltpu.make_async_copy(dest_hbm.at[step], dest_smem.at[pl.ds(half * width, width)], sem_idx.at[half])

    @pl.when(i == 0)
    def _():
        copy(0).start()

    copy(i).wait()

    @pl.when(i + 1 < pl.num_programs(0))
    def _():
        copy(i + 1).start()

    return (i % 2) * width


def _dispatch_kernel(dest_hbm, pend_ref, x_ref, buf_hbm, dest_smem, zero_ref, sem_idx, sem_rows):
    i = pl.program_id(0)
    tm = x_ref.shape[0] // ROW_TILE
    bm = zero_ref.shape[0] // ROW_TILE

    @pl.when(i == 0)
    def _():
        zero_ref[...] = jnp.zeros_like(zero_ref)

        def last_block(e):
            start = pl.multiple_of(jnp.maximum(pend_ref[e] - bm, 0), bm)
            return pltpu.make_async_copy(zero_ref, _rows_of(buf_hbm, start, bm), sem_rows)

        for e in range(N_EXPERTS):
            last_block(e).start()
        for e in range(N_EXPERTS):
            last_block(e).wait()

        def unused_block(blk, carry):
            copy = pltpu.make_async_copy(zero_ref, _rows_of(buf_hbm, pl.multiple_of(blk * bm, bm), bm), sem_rows)
            copy.start()
            copy.wait()
            return carry

        lax.fori_loop(pend_ref[N_EXPERTS - 1] // bm, buf_hbm.shape[0] // (bm * ROW_TILE), unused_block, 0)

    base = _prefetched_indices(dest_hbm, dest_smem, sem_idx, TOP_K * tm)

    def body(o, carry):
        for u in range(ROW_UNROLL):
            r = o * ROW_UNROLL + u
            for k in range(TOP_K):
                _row_copy(x_ref, r, buf_hbm, dest_smem[base + TOP_K * r + k], sem_rows).start()
        return carry

    lax.fori_loop(0, tm // ROW_UNROLL, body, 0)
    for k in range(TOP_K):
        pltpu.make_async_copy(x_ref, _rows_of(buf_hbm, 0, tm), sem_rows).wait()


def _dispatch(xn, dest, pend, n_rows):
    n = xn.shape[0] // ROW_TILE
    tm = min(MOE_TILE, n)
    assert n % tm == 0
    return pl.pallas_call(
        _dispatch_kernel,
        grid=(n // tm,),
        in_specs=[pl.BlockSpec(memory_space=pl.ANY), pl.BlockSpec(memory_space=pltpu.SMEM),
                  _rows(tm * ROW_TILE, LANES)],
        out_specs=pl.BlockSpec(memory_space=pl.ANY),
        out_shape=jax.ShapeDtypeStruct((n_rows * ROW_TILE, LANES), xn.dtype),
        scratch_shapes=[pltpu.SMEM((2 * TOP_K * tm,), jnp.int32),
                        pltpu.VMEM((EXPERT_BLOCK * ROW_TILE, LANES), xn.dtype),
                        pltpu.SemaphoreType.DMA((2,)), pltpu.SemaphoreType.DMA],
        compiler_params=_cparams("arbitrary"),
        name="dispatch",
    )(dest.reshape(n // tm, TOP_K * tm), pend, xn)


def _combine_kernel(dest_hbm, y_hbm, x_ref, rf_ref, o_ref, dest_smem, y0_ref, y1_ref, sem_idx, sem_rows):
    tm = x_ref.shape[0]
    base = _prefetched_indices(dest_hbm, dest_smem, sem_idx, TOP_K * tm)
    ybufs = (y0_ref, y1_ref)

    def body(o, carry):
        for u in range(ROW_UNROLL):
            r = o * ROW_UNROLL + u
            for k in range(TOP_K):
                _row_copy(y_hbm, dest_smem[base + TOP_K * r + k], ybufs[k], r, sem_rows).start()
        return carry

    lax.fori_loop(0, tm // ROW_UNROLL, body, 0)
    for k in range(TOP_K):
        pltpu.make_async_copy(_rows_of(y_hbm, 0, tm), ybufs[k], sem_rows).wait()
    sub = min(tm, EXPERT_BLOCK)
    for r0 in range(0, tm, sub):
        rs = slice(r0, r0 + sub)
        rf = rf_ref[rs, :]
        o_ref[rs, :] = x_ref[rs, :] + (_load_row_tiles(y0_ref, r0, sub) * rf[:, 0:1]
                                       + _load_row_tiles(y1_ref, r0, sub) * rf[:, 1:2])


def _combine(x1, y, dest, rf):
    n, d = x1.shape
    tm = min(MOE_TILE, n)
    assert n % tm == 0
    return pl.pallas_call(
        _combine_kernel,
        grid=(n // tm,),
        in_specs=[pl.BlockSpec(memory_space=pl.ANY), pl.BlockSpec(memory_space=pl.ANY), _rows(tm, d),
                  _rows(tm, LANES)],
        out_specs=_rows(tm, d),
        out_shape=jax.ShapeDtypeStruct((n, d), x1.dtype),
        scratch_shapes=[pltpu.SMEM((2 * TOP_K * tm,), jnp.int32), pltpu.VMEM((tm * ROW_TILE, LANES), y.dtype),
                        pltpu.VMEM((tm * ROW_TILE, LANES), y.dtype), pltpu.SemaphoreType.DMA((2,)),
                        pltpu.SemaphoreType.DMA],
        compiler_params=_cparams("arbitrary"),
        name="combine",
    )(dest.reshape(n // tm, TOP_K * tm), y, x1, rf)


def _moe(x1, xn, ri, rf, cnt, wg, wu, wd, layer):
    n, d = x1.shape
    bm = EXPERT_BLOCK
    counts = cnt[0, N_GROUPS:N_GROUPS + N_EXPERTS].astype(jnp.int32)
    psz = (counts + bm - 1) // bm * bm
    pend = jnp.cumsum(psz)
    pstart = pend - psz
    pstart_row = jnp.pad(pstart.astype(F32), (0, LANES - N_EXPERTS))[None, :]
    dest = _dest(ri, pstart_row)[:, :TOP_K]
    n_blocks = -(-(n * TOP_K) // bm) + N_EXPERTS
    blk_start = jnp.arange(n_blocks, dtype=jnp.int32) * bm
    blk_e = jnp.minimum(jnp.sum(pend[None, :] <= blk_start[:, None], axis=1), N_EXPERTS - 1).astype(jnp.int32)
    n_used = (pend[-1] // bm).astype(jnp.int32).reshape(1)
    buf = _dispatch(xn, dest, pend.astype(jnp.int32), n_blocks * bm)
    y = _experts(buf, blk_e, n_used, wg, wu, wd, layer)
    return x1, y, dest, rf


def _prep_layer(l, P):
    d = P['w_in'].shape[1]
    cw = P['conv_dw_w'].shape[-1]
    mw = P['ml_norm_g'].shape[-1]
    nw = NA_HEADS * P['na_q_norm_g'].shape[-1]
    off_g = 2 * cw + 4 * mw
    ng = 4 * ML_HEADS
    w = P['w_in'][l]
    w_perm = jnp.concatenate([w[:, :off_g], w[:, off_g + ng:], w[:, off_g:off_g + ng],
                              jnp.zeros((d, LANES - ng), F32)], axis=1).astype(BF16)
    nhd = nw // NA_HEADS
    head_id = jnp.arange(nw) // nhd
    bd = jnp.where(head_id[:, None] == head_id[None, :], 1.0 / nhd, 0.0).astype(BF16)
    wr = jnp.concatenate([P['w_router_group'][l], P['w_router_expert'][l],
                          jnp.zeros((d, LANES - N_GROUPS - N_EXPERTS), F32)], axis=1)
    br = jnp.concatenate([P['b_router_group'][l], P['b_router_expert'][l],
                          jnp.zeros((LANES - N_GROUPS - N_EXPERTS,), F32)])[None, :]
    return dict(
        dims=(cw, mw, nw),
        norm_mix_g=P['norm_mix_g'][l][None, :], w_in=w_perm,
        gate_b=jnp.pad(P['ml_gate_b'][l].reshape(-1), (0, LANES - ng))[None, :],
        nq=jnp.tile(P['na_q_norm_g'][l], NA_HEADS)[None, :], nk=jnp.tile(P['na_k_norm_g'][l], NA_HEADS)[None, :],
        bd=bd,
        conv_dw_w=P['conv_dw_w'][l], conv_dw_b=P['conv_dw_b'][l][None, :], conv_norm_g=P['conv_norm_g'][l][None, :],
        ml_qk_conv_w=P['ml_qk_conv_w'][l], ml_qk_conv_b=P['ml_qk_conv_b'][l][None, :],
        ml_norm_g=P['ml_norm_g'][l][None, :], na_table=_na_bias_table(P['na_rpb'][l]),
        w_out=P['w_out'][l].astype(BF16), norm_ffn_g=P['norm_ffn_g'][l][None, :],
        wrh=wr.astype(BF16), wrl=(wr - wr.astype(BF16).astype(F32)).astype(BF16), br=br,
        wg=P['w_exp_gate'], wu=P['w_exp_up'], wd=P['w_exp_down'], layer=l,
    )


def _layer(x, pending, B, T, p):
    dims = p['dims']
    proj_params = (p['norm_mix_g'], p['w_in'], p['gate_b'], p['nq'], p['nk'], p['bd'], dims)
    if pending is None:
        u, qkpre, v, o, naq, nak, nav, gates = _in_proj(x, *proj_params)
    else:
        x, (u, qkpre, v, o, naq, nak, nav, gates) = _combine_in_proj(pending, *proj_params)
    yc, hf, hb, yna = _mixers(B, T, u, qkpre, v, naq, nak, nav, gates, p)
    x1, xn, ri, rf, cnt = _out_proj(yc, hf, hb, o, yna, x, p['ml_norm_g'], p['w_out'], p['norm_ffn_g'], p['wrh'],
                                    p['wrl'], p['br'], dims)
    return _moe(x1, xn, ri, rf, cnt, p['wg'], p['wu'], p['wd'], p['layer'])


def kernel(x_prompt, x_sample, norm_mix_g, w_in, conv_dw_w, conv_dw_b, conv_norm_g, ml_qk_conv_w, ml_qk_conv_b,
           ml_gate_b, ml_norm_g, na_q_norm_g, na_k_norm_g, na_rpb, w_out, norm_ffn_g, w_router_group,
           b_router_group, w_router_expert, b_router_expert, w_exp_gate, w_exp_up, w_exp_down):
    P = dict(norm_mix_g=norm_mix_g, w_in=w_in, conv_dw_w=conv_dw_w, conv_dw_b=conv_dw_b, conv_norm_g=conv_norm_g,
             ml_qk_conv_w=ml_qk_conv_w, ml_qk_conv_b=ml_qk_conv_b, ml_gate_b=ml_gate_b, ml_norm_g=ml_norm_g,
             na_q_norm_g=na_q_norm_g, na_k_norm_g=na_k_norm_g, na_rpb=na_rpb, w_out=w_out, norm_ffn_g=norm_ffn_g,
             w_router_group=w_router_group, b_router_group=b_router_group, w_router_expert=w_router_expert,
             b_router_expert=b_router_expert, w_exp_gate=w_exp_gate, w_exp_up=w_exp_up, w_exp_down=w_exp_down)
    depth = w_in.shape[0]
    d = x_prompt.shape[-1]
    groups = [(x_prompt.shape[0], x_prompt.shape[1], x_prompt.reshape(-1, d)),
              (x_sample.shape[0], x_sample.shape[1], x_sample.reshape(-1, d))]
    pendings = [None] * len(groups)
    for l in range(depth):
        p = _prep_layer(l, P)
        pendings = [_layer(x, pending, B, T, p) for (B, T, x), pending in zip(groups, pendings)]
        groups = [(B, T, None) for B, T, _ in groups]
    return tuple(_combine(*pending).reshape(B, T, d) for (B, T, _), pending in zip(groups, pendings))
```
